```python
import math
import numpy as np
import jax
import jax.numpy as jnp
from jax import lax

D_MODEL = 2048
BATCH = 8
SEQ = 4096
DEPTH = 2

CTX_LEN = 256
GRID_W = 64
HEAD_DIM = 128
Q_BLOCK = 128
ROPE_THETA = 10000.0
EPS = 1e-6
NEG_INF = -1e30

A_HEADS = 8
A_KV_HEADS = 2
A_WINDOW = 128
B_HEADS = 4
C_HEADS = 8
C_KV_HEADS = 2
D_HEADS = 8
NA_ROWS = 8
NA_COLS = 16

EVEN_SPLITS = (A_HEADS * HEAD_DIM, A_KV_HEADS * HEAD_DIM, A_KV_HEADS * HEAD_DIM,
               B_HEADS * 2 * HEAD_DIM, B_HEADS * 2 * HEAD_DIM, B_HEADS * 2 * HEAD_DIM)
ODD_SPLITS = (C_HEADS * HEAD_DIM, C_KV_HEADS * HEAD_DIM, C_KV_HEADS * HEAD_DIM,
              D_HEADS * HEAD_DIM, D_HEADS * HEAD_DIM, D_HEADS * HEAD_DIM)
IN_WIDTH = 4608
MIX_WIDTH = 2048

N_EXPERTS = 16
EC_CAPACITY_FACTOR = 2
D_EXPERT = 1536

kernel_name = 'hybrid_window_diff_axial_natten_ecmoe_dit'


def rmsnorm(x, g):
    xf = x.astype(jnp.float32)
    y = xf * lax.rsqrt(jnp.mean(xf * xf, axis=-1, keepdims=True) + EPS)
    return y.astype(x.dtype) * g


def axial_rope_tables(n, dtype):
    t = jnp.arange(n, dtype=jnp.int32)
    row = (t // GRID_W).astype(jnp.float32)
    col = (t % GRID_W).astype(jnp.float32)
    n_freq = HEAD_DIM // 4
    inv = ROPE_THETA ** (-jnp.arange(n_freq, dtype=jnp.float32) / n_freq)
    ar = row[:, None] * inv
    ac = col[:, None] * inv
    ang = jnp.concatenate([ar, ar, ac, ac], axis=-1)
    return jnp.cos(ang).astype(dtype), jnp.sin(ang).astype(dtype)


def apply_rope(x, cos, sin):
    nf = HEAD_DIM // 4
    xr = x.reshape(x.shape[:-1] + (2, 2, nf))
    rot = jnp.stack([-xr[..., 1, :], xr[..., 0, :]], axis=-2).reshape(x.shape)
    shp = (x.shape[1],) + (1,) * (x.ndim - 3) + (HEAD_DIM,)
    return x * cos.reshape(shp) + rot * sin.reshape(shp)


def split_cols(p, sizes):
    offsets = np.cumsum(np.array(sizes))[:-1].tolist()
    return jnp.split(p, offsets, axis=-1)


def sweep_query_blocks(fn, *qs):
    b, s = qs[0].shape[:2]
    nb = s // Q_BLOCK
    blocks = tuple(jnp.moveaxis(q.reshape((b, nb, Q_BLOCK) + q.shape[2:]), 1, 0) for q in qs)
    out = lax.map(lambda a: fn(*a[0], a[1]), (blocks, jnp.arange(nb)))
    return jnp.moveaxis(out, 0, 1).reshape((b, s) + out.shape[3:])


def attend(q, k, v, scale, sink=None, mask=None):
    s = jnp.einsum('bqgrd,bkgd->bgrqk', q, k).astype(jnp.float32) * scale
    if mask is not None:
        s = jnp.where(mask, s, NEG_INF)
    if sink is not None:
        sk = jnp.broadcast_to(sink.astype(jnp.float32)[None, :, :, None, None], s.shape[:-1] + (1,))
        p = jax.nn.softmax(jnp.concatenate([s, sk], axis=-1), axis=-1)[..., :-1]
    else:
        p = jax.nn.softmax(s, axis=-1)
    return jnp.einsum('bgrqk,bkgd->bqgrd', p.astype(v.dtype), v)


def diff_attend(q1, q2, k1, k2, v, lam, scale):
    p1 = jax.nn.softmax(jnp.einsum('bqhd,bkhd->bhqk', q1, k1).astype(jnp.float32) * scale, axis=-1)
    p2 = jax.nn.softmax(jnp.einsum('bqhd,bkhd->bhqk', q2, k2).astype(jnp.float32) * scale, axis=-1)
    p = p1 - lam * p2
    return jnp.einsum('bhqk,bkhe->bqhe', p.astype(v.dtype), v)


def window_attention(q, k, v, k_ctx, v_ctx, sink, scale):
    s = q.shape[1]
    band = Q_BLOCK + 2 * A_WINDOW
    pad = ((0, 0), (A_WINDOW, A_WINDOW), (0, 0), (0, 0))
    k_pad = jnp.pad(k, pad)
    v_pad = jnp.pad(v, pad)
    ctx_mask = jnp.ones((Q_BLOCK, k_ctx.shape[1]), dtype=bool)

    def block(qb, i):
        start = i * Q_BLOCK
        kb = lax.dynamic_slice_in_dim(k_pad, start, band, axis=1)
        vb = lax.dynamic_slice_in_dim(v_pad, start, band, axis=1)
        qpos = start + jnp.arange(Q_BLOCK)
        kpos = start - A_WINDOW + jnp.arange(band)
        in_band = ((jnp.abs(qpos[:, None] - kpos[None, :]) <= A_WINDOW)
                   & (kpos >= 0)[None, :] & (kpos < s)[None, :])
        mask = jnp.concatenate([ctx_mask, in_band], axis=1)
        return attend(qb, jnp.concatenate([k_ctx, kb], axis=1), jnp.concatenate([v_ctx, vb], axis=1),
                      scale, sink, mask)

    return sweep_query_blocks(block, q)


def neighbourhood_attention(q, k, v, k_ctx, v_ctx, rpb, scale):
    b, s, h, d = q.shape
    rows = s // GRID_W
    kh = min(NA_ROWS, rows)
    n_ctx = k_ctx.shape[1]
    qg = q.reshape(b, rows, GRID_W, h, d)
    kg = k.reshape(b, rows, GRID_W, h, d)
    vg = v.reshape(b, rows, GRID_W, h, d)
    col = jnp.arange(GRID_W)
    cs = jnp.clip(col - NA_COLS // 2, 0, GRID_W - NA_COLS)
    col_mask = (col[None, :] >= cs[:, None]) & (col[None, :] < cs[:, None] + NA_COLS)
    col_idx = jnp.clip(col[None, :] - col[:, None], -(NA_COLS - 1), NA_COLS - 1) + NA_COLS - 1
    rpb_c = rpb[:, :, col_idx]

    def row_block(args):
        q_row, r = args
        rs = jnp.clip(r - kh // 2, 0, rows - kh)
        kb = lax.dynamic_slice_in_dim(kg, rs, kh, axis=1)
        vb = lax.dynamic_slice_in_dim(vg, rs, kh, axis=1)
        sc = jnp.einsum('bqhd,bawhd->bhqaw', q_row, kb).astype(jnp.float32) * scale
        roff = rs + jnp.arange(kh) - r + NA_ROWS - 1
        bias = jnp.transpose(rpb_c[:, roff], (0, 2, 1, 3)).astype(jnp.float32)
        sc = jnp.where(col_mask[:, None, :], sc + bias[None], NEG_INF)
        sc = sc.reshape(b, h, GRID_W, kh * GRID_W)
        sx = jnp.einsum('bqhd,bkhd->bhqk', q_row, k_ctx).astype(jnp.float32) * scale
        p = jax.nn.softmax(jnp.concatenate([sx, sc], axis=-1), axis=-1).astype(v.dtype)
        px = p[..., :n_ctx]
        pb = p[..., n_ctx:].reshape(b, h, GRID_W, kh, GRID_W)
        return (jnp.einsum('bhqk,bkhd->bqhd', px, v_ctx)
                + jnp.einsum('bhqaw,bawhd->bqhd', pb, vb))

    out = lax.map(row_block, (jnp.moveaxis(qg, 1, 0), jnp.arange(rows)))
    return jnp.moveaxis(out, 0, 1).reshape(b, s, h, d)


def mixer_window_diff(p_lat, p_ctx, cos, sin, sink, lam_q1, lam_k1, lam_q2, lam_k2, subln_w, lam_init, need_ctx):
    b, s, _ = p_lat.shape
    scale = HEAD_DIM ** -0.5
    ra = A_HEADS // A_KV_HEADS

    def heads(p):
        n = p.shape[1]
        qa, ka, va, qb, kb, vb = split_cols(p, EVEN_SPLITS)
        return (qa.reshape(b, n, A_KV_HEADS, ra, HEAD_DIM),
                ka.reshape(b, n, A_KV_HEADS, HEAD_DIM),
                va.reshape(b, n, A_KV_HEADS, HEAD_DIM),
                qb.reshape(b, n, B_HEADS, 2, HEAD_DIM),
                kb.reshape(b, n, B_HEADS, 2, HEAD_DIM),
                vb.reshape(b, n, B_HEADS, 2 * HEAD_DIM))

    qa, ka, va, qb, kb, vb = heads(p_lat)
    qa_x, ka_x, va_x, qb_x, kb_x, vb_x = heads(p_ctx)
    qa, ka, qb, kb = (apply_rope(t, cos, sin) for t in (qa, ka, qb, kb))
    sink_gr = sink.reshape(A_KV_HEADS, ra)
    o_a = window_attention(qa, ka, va, ka_x, va_x, sink_gr, scale)
    lam = (jnp.exp(jnp.sum(lam_q1.astype(jnp.float32) * lam_k1.astype(jnp.float32)))
           - jnp.exp(jnp.sum(lam_q2.astype(jnp.float32) * lam_k2.astype(jnp.float32))) + lam_init)
    k1_all = jnp.concatenate([kb_x[..., 0, :], kb[..., 0, :]], axis=1)
    k2_all = jnp.concatenate([kb_x[..., 1, :], kb[..., 1, :]], axis=1)
    v_all = jnp.concatenate([vb_x, vb], axis=1)
    o_b = sweep_query_blocks(lambda q1, q2, i: diff_attend(q1, q2, k1_all, k2_all, v_all, lam, scale),
                             qb[..., 0, :], qb[..., 1, :])
    o_b = rmsnorm(o_b, subln_w) * (1.0 - lam_init)
    out_lat = jnp.concatenate([o_a.reshape(b, s, -1), o_b.reshape(b, s, -1)], axis=-1)
    if not need_ctx:
        return out_lat, None
    n = p_ctx.shape[1]
    oa_x = attend(qa_x, ka_x, va_x, scale, sink_gr)
    ob_x = diff_attend(qb_x[..., 0, :], qb_x[..., 1, :], kb_x[..., 0, :], kb_x[..., 1, :], vb_x, lam, scale)
    ob_x = rmsnorm(ob_x, subln_w) * (1.0 - lam_init)
    out_ctx = jnp.concatenate([oa_x.reshape(b, n, -1), ob_x.reshape(b, n, -1)], axis=-1)
    return out_lat, out_ctx


def mixer_axial_neighbourhood(p_lat, p_ctx, cos, sin, q_norm, k_norm, rpb, need_ctx):
    b, s, _ = p_lat.shape
    scale = HEAD_DIM ** -0.5
    rc = C_HEADS // C_KV_HEADS

    def heads(p):
        n = p.shape[1]
        cq, ck, cv, dq, dk, dv = split_cols(p, ODD_SPLITS)
        return (rmsnorm(cq.reshape(b, n, C_KV_HEADS, rc, HEAD_DIM), q_norm),
                rmsnorm(ck.reshape(b, n, C_KV_HEADS, HEAD_DIM), k_norm),
                cv.reshape(b, n, C_KV_HEADS, HEAD_DIM),
                dq.reshape(b, n, D_HEADS, HEAD_DIM),
                dk.reshape(b, n, D_HEADS, HEAD_DIM),
                dv.reshape(b, n, D_HEADS, HEAD_DIM))

    cq, ck, cv, dq, dk, dv = heads(p_lat)
    cq_x, ck_x, cv_x, dq_x, dk_x, dv_x = heads(p_ctx)
    cq = apply_rope(cq, cos, sin)
    ck = apply_rope(ck, cos, sin)
    k_all = jnp.concatenate([ck_x, ck], axis=1)
    v_all = jnp.concatenate([cv_x, cv], axis=1)
    o_c = sweep_query_blocks(lambda q, i: attend(q, k_all, v_all, scale), cq)
    o_d = neighbourhood_attention(dq, dk, dv, dk_x, dv_x, rpb, scale)
    out_lat = jnp.concatenate([o_c.reshape(b, s, -1), o_d.reshape(b, s, -1)], axis=-1)
    if not need_ctx:
        return out_lat, None
    n = p_ctx.shape[1]
    oc_x = attend(cq_x, ck_x, cv_x, scale)
    od_x = attend(dq_x[:, :, :, None, :], dk_x, dv_x, scale)
    out_ctx = jnp.concatenate([oc_x.reshape(b, n, -1), od_x.reshape(b, n, -1)], axis=-1)
    return out_lat, out_ctx


def expert_choice_ffn(h, w_router, w_gate, w_up, w_down):
    b, n, _ = h.shape
    cap = max(1, EC_CAPACITY_FACTOR * n // N_EXPERTS)
    aff = jax.nn.softmax((h @ w_router).astype(jnp.float32), axis=-1)
    gate, idx = lax.top_k(jnp.swapaxes(aff, 1, 2), cap)
    bidx = jnp.arange(b)[:, None, None]
    xe = h[bidx, idx]
    hid = (jax.nn.silu(jnp.einsum('becd,edf->becf', xe, w_gate))
           * jnp.einsum('becd,edf->becf', xe, w_up))
    ye = jnp.einsum('becf,efd->becd', hid, w_down) * gate[..., None].astype(h.dtype)
    return jnp.zeros_like(h).at[bidx, idx].add(ye)


def lambda_init_for(layer):
    return 0.8 - 0.6 * math.exp(-0.3 * layer)


def setup_inputs(seed: int = 0) -> dict:
    key = jax.random.key(seed)
    ks = jax.random.split(key, 24)
    n_even = (DEPTH + 1) // 2
    n_odd = DEPTH // 2
    d = D_MODEL

    def nrm(k, shape, scale):
        return jax.random.normal(k, shape, jnp.float32) * scale

    return {
        'x': nrm(ks[0], (BATCH, SEQ, d), 1.0),
        'c': nrm(ks[1], (BATCH, d), 1.0),
        'ctx': nrm(ks[2], (BATCH, CTX_LEN, d), 1.0),
        'c_ctx': nrm(ks[3], (d,), 1.0),
        'w_ada': nrm(ks[4], (DEPTH, d, 6 * d), 0.5 * d ** -0.5),
        'b_ada': nrm(ks[5], (DEPTH, 6 * d), 0.01),
        'g_mix': 1.0 + nrm(ks[6], (DEPTH, d), 0.05),
        'g_ffn': 1.0 + nrm(ks[7], (DEPTH, d), 0.05),
        'w_in': nrm(ks[8], (DEPTH, d, IN_WIDTH), d ** -0.5),
        'w_out': nrm(ks[9], (DEPTH, MIX_WIDTH, d), MIX_WIDTH ** -0.5),
        'a_sink': nrm(ks[10], (n_even, A_HEADS), 0.5),
        'b_lam_q1': nrm(ks[11], (n_even, HEAD_DIM), 0.1),
        'b_lam_k1': nrm(ks[12], (n_even, HEAD_DIM), 0.1),
        'b_lam_q2': nrm(ks[13], (n_even, HEAD_DIM), 0.1),
        'b_lam_k2': nrm(ks[14], (n_even, HEAD_DIM), 0.1),
        'b_subln': 1.0 + nrm(ks[15], (n_even, 2 * HEAD_DIM), 0.05),
        'c_q_norm': 1.0 + nrm(ks[16], (n_odd, HEAD_DIM), 0.05),
        'c_k_norm': 1.0 + nrm(ks[17], (n_odd, HEAD_DIM), 0.05),
        'd_rpb': nrm(ks[18], (n_odd, D_HEADS, 2 * NA_ROWS - 1, 2 * NA_COLS - 1), 0.1),
        'w_router': nrm(ks[19], (DEPTH, d, N_EXPERTS), d ** -0.5),
        'w_gate': nrm(ks[20], (DEPTH, N_EXPERTS, d, D_EXPERT), d ** -0.5),
        'w_up': nrm(ks[21], (DEPTH, N_EXPERTS, d, D_EXPERT), d ** -0.5),
        'w_down': nrm(ks[22], (DEPTH, N_EXPERTS, D_EXPERT, d), D_EXPERT ** -0.5),
        'g_final': 1.0 + nrm(ks[23], (d,), 0.05),
    }


def reference(x, c, ctx, c_ctx, w_ada, b_ada, g_mix, g_ffn, w_in, w_out, a_sink,
              b_lam_q1, b_lam_k1, b_lam_q2, b_lam_k2, b_subln, c_q_norm, c_k_norm, d_rpb,
              w_router, w_gate, w_up, w_down, g_final):
    s = x.shape[1]
    cos, sin = axial_rope_tables(s, x.dtype)
    cond = jax.nn.silu(c)
    cond_x = jax.nn.silu(c_ctx)
    xc = ctx
    for l in range(DEPTH):
        need_ctx = l < DEPTH - 1
        e = l // 2
        mod = (cond @ w_ada[l] + b_ada[l])[:, None, :]
        mod_x = cond_x @ w_ada[l] + b_ada[l]
        sh1, sc1, gt1, sh2, sc2, gt2 = jnp.split(mod, 6, axis=-1)
        sh1x, sc1x, gt1x, sh2x, sc2x, gt2x = jnp.split(mod_x, 6, axis=-1)
        p = (rmsnorm(x, g_mix[l]) * (1 + sc1) + sh1) @ w_in[l]
        px = (rmsnorm(xc, g_mix[l]) * (1 + sc1x) + sh1x) @ w_in[l]
        if l % 2 == 0:
            o, ox = mixer_window_diff(p, px, cos, sin, a_sink[e], b_lam_q1[e], b_lam_k1[e],
                                      b_lam_q2[e], b_lam_k2[e], b_subln[e], lambda_init_for(l), need_ctx)
        else:
            o, ox = mixer_axial_neighbourhood(p, px, cos, sin, c_q_norm[e], c_k_norm[e], d_rpb[e], need_ctx)
        x = x + gt1 * (o @ w_out[l])
        h = rmsnorm(x, g_ffn[l]) * (1 + sc2) + sh2
        x = x + gt2 * expert_choice_ffn(h, w_router[l], w_gate[l], w_up[l], w_down[l])
        if need_ctx:
            xc = xc + gt1x * (ox @ w_out[l])
            hx = rmsnorm(xc, g_ffn[l]) * (1 + sc2x) + sh2x
            xc = xc + gt2x * expert_choice_ffn(hx, w_router[l], w_gate[l], w_up[l], w_down[l])
    return rmsnorm(x, g_final)
```

```python
import functools
import math

import jax
import jax.numpy as jnp
from jax import lax
from jax.experimental import pallas as pl
from jax.experimental.pallas import tpu as pltpu

F32 = jnp.float32
BF16 = jnp.bfloat16

HEAD_DIM = 128
GRID_W = 64
ROPE_THETA = 10000.0
EPS = 1e-6
NEG_INF = -1e30
A_HEADS, A_KV_HEADS, A_WINDOW = 8, 2, 128
B_HEADS = 4
C_HEADS, C_KV_HEADS = 8, 2
D_HEADS, NA_ROWS, NA_COLS = 8, 8, 16
N_EXPERTS = 16
EC_CAPACITY_FACTOR = 2
IN_WIDTH = 4608
MIX_WIDTH = 2048
ROUTER_LANES = 128

ROPE_BIT, QNORM_BIT, KNORM_BIT, SCALE_BIT = 1, 2, 4, 8
IN_TN = 256

V7X_VMEM_LIMIT = 56 * 1024 * 1024


def _params(sem, vmem=V7X_VMEM_LIMIT):
    return pltpu.CompilerParams(dimension_semantics=sem, vmem_limit_bytes=vmem)


def _nt_dot(a, b):
    return lax.dot_general(a, b, (((1,), (1,)), ((), ())), preferred_element_type=F32)


def _ada_kernel(c_ref, w_ref, b_ref, o_ref):
    c = c_ref[...]
    cond = c * (1.0 / (1.0 + jnp.exp(-c)))
    o_ref[0] = jnp.dot(cond.astype(BF16), w_ref[0].astype(BF16), preferred_element_type=F32) + b_ref[0]


def ada_modulation(c_all, w_ada, b_ada):
    depth, d, n = w_ada.shape
    r = c_all.shape[0]
    tn = math.gcd(n, 1024)
    return pl.pallas_call(
        _ada_kernel,
        grid=(depth, n // tn),
        in_specs=[pl.BlockSpec((r, d), lambda l, j: (0, 0)),
                  pl.BlockSpec((1, d, tn), lambda l, j: (l, 0, j)),
                  pl.BlockSpec((1, 1, tn), lambda l, j: (l, 0, j))],
        out_specs=pl.BlockSpec((1, r, tn), lambda l, j: (l, 0, j)),
        out_shape=jax.ShapeDtypeStruct((depth, r, n), F32),
        compiler_params=_params(("arbitrary", "arbitrary")),
        name="ada_modulation",
    )(c_all, w_ada, b_ada.reshape(depth, 1, n))


def _rope_rotate(y, sin_signed):
    lane = lax.broadcasted_iota(jnp.int32, y.shape, 1)
    first = (lane & 63) < 32
    return jnp.where(first, pltpu.roll(y, 96, 1), pltpu.roll(y, 32, 1)) * sin_signed


def _in_proj_kernel(codes_ref, x_ref, g_ref, sc_ref, sh_ref, w_ref, cos_ref, sin_ref, qn_ref, kn_ref,
                    o_ref, xn_ref, y_ref, *, scale):
    j = pl.program_id(2)

    @pl.when(j == 0)
    def _():
        x = x_ref[0]
        ms = jnp.mean(x * x, axis=-1, keepdims=True)
        y = x * lax.rsqrt(ms + EPS) * g_ref[...]
        xn_ref[...] = (y * (1.0 + sc_ref[0]) + sh_ref[0]).astype(BF16)

    y_ref[...] = jnp.dot(xn_ref[...], w_ref[...], preferred_element_type=F32)
    code = codes_ref[j]
    n_heads = y_ref.shape[1] // HEAD_DIM

    def head_norm(gn_ref):
        for h in range(n_heads):
            sl = slice(h * HEAD_DIM, (h + 1) * HEAD_DIM)
            y = y_ref[:, sl]
            ms = jnp.mean(y * y, axis=-1, keepdims=True)
            y_ref[:, sl] = y * lax.rsqrt(ms + EPS) * gn_ref[...]

    @pl.when((code & QNORM_BIT) != 0)
    def _():
        head_norm(qn_ref)

    @pl.when((code & KNORM_BIT) != 0)
    def _():
        head_norm(kn_ref)

    @pl.when((code & ROPE_BIT) != 0)
    def _():
        for h in range(n_heads):
            sl = slice(h * HEAD_DIM, (h + 1) * HEAD_DIM)
            y = y_ref[:, sl]
            y_ref[:, sl] = y * cos_ref[...] + _rope_rotate(y, sin_ref[...])

    @pl.when((code & SCALE_BIT) != 0)
    def _():
        y_ref[...] = y_ref[...] * scale

    o_ref[0] = y_ref[...].astype(BF16)


def in_projection(x, g, scale1, shift1, w_bf16, codes, cos, sin_signed, qn, kn, tm):
    b, n, d = x.shape
    width = w_bf16.shape[1]
    per_batch = scale1.shape[0] > 1
    mod_map = (lambda bi, i, j, c: (bi, 0, 0)) if per_batch else (lambda bi, i, j, c: (0, 0, 0))
    grid_spec = pltpu.PrefetchScalarGridSpec(
        num_scalar_prefetch=1,
        grid=(b, n // tm, width // IN_TN),
        in_specs=[pl.BlockSpec((1, tm, d), lambda bi, i, j, c: (bi, i, 0)),
                  pl.BlockSpec((1, d), lambda bi, i, j, c: (0, 0)),
                  pl.BlockSpec((1, 1, d), mod_map),
                  pl.BlockSpec((1, 1, d), mod_map),
                  pl.BlockSpec((d, IN_TN), lambda bi, i, j, c: (0, j)),
                  pl.BlockSpec((tm, HEAD_DIM), lambda bi, i, j, c: (i, 0)),
                  pl.BlockSpec((tm, HEAD_DIM), lambda bi, i, j, c: (i, 0)),
                  pl.BlockSpec((1, HEAD_DIM), lambda bi, i, j, c: (0, 0)),
                  pl.BlockSpec((1, HEAD_DIM), lambda bi, i, j, c: (0, 0))],
        out_specs=pl.BlockSpec((1, tm, IN_TN), lambda bi, i, j, c: (bi, i, j)),
        scratch_shapes=[pltpu.VMEM((tm, d), BF16), pltpu.VMEM((tm, IN_TN), F32)])
    return pl.pallas_call(
        functools.partial(_in_proj_kernel, scale=HEAD_DIM ** -0.5),
        grid_spec=grid_spec,
        out_shape=jax.ShapeDtypeStruct((b, n, width), BF16),
        compiler_params=_params(("arbitrary", "arbitrary", "arbitrary")),
        name="in_projection",
    )(codes, x, g.reshape(1, d), scale1, shift1, w_bf16, cos, sin_signed, qn, kn)


def _gqa_kernel(*refs, rep, tq, mode, seq, use_sink):
    if use_sink:
        sink_ref, refs = refs[0], refs[1:]
    if mode == 'none':
        q_ref, kx_ref, vx_ref, o_ref = refs
    else:
        q_ref, kx_ref, vx_ref, kl_ref, vl_ref, o_ref = refs
    g = pl.program_id(1)
    qi = pl.program_id(2)
    q = q_ref[0]
    qs = jnp.concatenate([q[:, r * HEAD_DIM:(r + 1) * HEAD_DIM] for r in range(rep)], axis=0)
    rows = rep * tq
    sx = _nt_dot(qs, kx_ref[0])
    m = jnp.max(sx, axis=-1, keepdims=True)
    if mode == 'band':
        band = tq + 2 * A_WINDOW
        start = pl.multiple_of(jnp.clip(qi * tq - A_WINDOW, 0, seq - band), HEAD_DIM)
        kl = kl_ref[0, pl.ds(start, band), :]
        vl = vl_ref[0, pl.ds(start, band), :]
        sl = _nt_dot(qs, kl)
        qpos = qi * tq + (lax.broadcasted_iota(jnp.int32, (rows, band), 0) & (tq - 1))
        kpos = start + lax.broadcasted_iota(jnp.int32, (rows, band), 1)
        sl = jnp.where(jnp.abs(qpos - kpos) <= A_WINDOW, sl, NEG_INF)
    elif mode == 'full':
        kl = kl_ref[0]
        vl = vl_ref[0]
        sl = _nt_dot(qs, kl)
    if mode != 'none':
        m = jnp.maximum(m, jnp.max(sl, axis=-1, keepdims=True))
    if use_sink:
        sk = jnp.concatenate([jnp.full((tq, 1), sink_ref[g * rep + r], F32) for r in range(rep)], axis=0)
        m = jnp.maximum(m, sk)
    px = jnp.exp(sx - m)
    den = jnp.sum(px, axis=-1, keepdims=True)
    acc = jnp.dot(px.astype(BF16), vx_ref[0], preferred_element_type=F32)
    if mode != 'none':
        pll = jnp.exp(sl - m)
        den = den + jnp.sum(pll, axis=-1, keepdims=True)
        acc = acc + jnp.dot(pll.astype(BF16), vl, preferred_element_type=F32)
    if use_sink:
        den = den + jnp.exp(sk - m)
    out = acc / den
    for r in range(rep):
        o_ref[0, :, r * HEAD_DIM:(r + 1) * HEAD_DIM] = out[r * tq:(r + 1) * tq].astype(BF16)


def gqa_attention(p_q, p_ctx, p_lat, *, kv_heads, rep, q_col, k_col, v_col, tq, mode, sink=None):
    b, nq, _ = p_q.shape
    c = p_ctx.shape[1]
    qw = rep * HEAD_DIM
    use_sink = sink is not None
    assert tq & (tq - 1) == 0
    in_specs = [pl.BlockSpec((1, tq, qw), lambda bi, g, qi, *_: (bi, qi, q_col // qw + g)),
                pl.BlockSpec((1, c, HEAD_DIM), lambda bi, g, qi, *_: (bi, 0, k_col // HEAD_DIM + g)),
                pl.BlockSpec((1, c, HEAD_DIM), lambda bi, g, qi, *_: (bi, 0, v_col // HEAD_DIM + g))]
    args = [p_q, p_ctx, p_ctx]
    seq = 0
    if mode != 'none':
        seq = p_lat.shape[1]
        in_specs += [pl.BlockSpec((1, seq, HEAD_DIM), lambda bi, g, qi, *_: (bi, 0, k_col // HEAD_DIM + g)),
                     pl.BlockSpec((1, seq, HEAD_DIM), lambda bi, g, qi, *_: (bi, 0, v_col // HEAD_DIM + g))]
        args += [p_lat, p_lat]
    if use_sink:
        args = [sink] + args
    grid_spec = pltpu.PrefetchScalarGridSpec(
        num_scalar_prefetch=1 if use_sink else 0,
        grid=(b, kv_heads, nq // tq),
        in_specs=in_specs,
        out_specs=pl.BlockSpec((1, tq, qw), lambda bi, g, qi, *_: (bi, qi, g)))
    return pl.pallas_call(
        functools.partial(_gqa_kernel, rep=rep, tq=tq, mode=mode, seq=seq, use_sink=use_sink),
        grid_spec=grid_spec,
        out_shape=jax.ShapeDtypeStruct((b, nq, kv_heads * qw), BF16),
        compiler_params=_params(("arbitrary", "arbitrary", "arbitrary")),
        name=f"gqa_attention_{mode}",
    )(*args)


def _diff_kernel(*refs, has_lat, lam_init):
    if has_lat:
        (lq1, lk1, lq2, lk2, sub_ref, q1_ref, q2_ref, k1x_ref, k2x_ref, vx_ref,
         k1l_ref, k2l_ref, vl_ref, o_ref) = refs
    else:
        lq1, lk1, lq2, lk2, sub_ref, q1_ref, q2_ref, k1x_ref, k2x_ref, vx_ref, o_ref = refs
    lam = (jnp.exp(jnp.sum(lq1[...] * lk1[...], axis=-1, keepdims=True))
           - jnp.exp(jnp.sum(lq2[...] * lk2[...], axis=-1, keepdims=True)) + lam_init)

    def probs(q_ref, kx_ref, kl_ref):
        q = q_ref[0]
        sx = _nt_dot(q, kx_ref[0])
        m = jnp.max(sx, axis=-1, keepdims=True)
        if has_lat:
            sl = _nt_dot(q, kl_ref[0])
            m = jnp.maximum(m, jnp.max(sl, axis=-1, keepdims=True))
        ex = jnp.exp(sx - m)
        den = jnp.sum(ex, axis=-1, keepdims=True)
        el = None
        if has_lat:
            el = jnp.exp(sl - m)
            den = den + jnp.sum(el, axis=-1, keepdims=True)
        inv = 1.0 / den
        return ex * inv, (el * inv if has_lat else None)

    p1x, p1l = probs(q1_ref, k1x_ref, k1l_ref if has_lat else None)
    p2x, p2l = probs(q2_ref, k2x_ref, k2l_ref if has_lat else None)
    o = jnp.dot((p1x - lam * p2x).astype(BF16), vx_ref[0], preferred_element_type=F32)
    if has_lat:
        o = o + jnp.dot((p1l - lam * p2l).astype(BF16), vl_ref[0], preferred_element_type=F32)
    ms = jnp.mean(o * o, axis=-1, keepdims=True)
    o_ref[0] = (o * lax.rsqrt(ms + EPS) * sub_ref[...] * (1.0 - lam_init)).astype(BF16)


def diff_attention(p_q, p_ctx, p_lat, lam_vecs, subln, *, lam_init, tq, has_lat):
    b, nq, _ = p_q.shape
    c = p_ctx.shape[1]
    qb = (A_HEADS + 2 * A_KV_HEADS) * HEAD_DIM
    kb = qb + B_HEADS * 2 * HEAD_DIM
    vb = kb + B_HEADS * 2 * HEAD_DIM
    dv = 2 * HEAD_DIM
    vec = pl.BlockSpec((1, HEAD_DIM), lambda bi, h, qi: (0, 0))

    def head(n, base, m):
        return pl.BlockSpec((1, n, HEAD_DIM), lambda bi, h, qi: (bi, 0, base // HEAD_DIM + 2 * h + m))

    in_specs = [vec, vec, vec, vec, pl.BlockSpec((1, dv), lambda bi, h, qi: (0, 0)),
                pl.BlockSpec((1, tq, HEAD_DIM), lambda bi, h, qi: (bi, qi, qb // HEAD_DIM + 2 * h)),
                pl.BlockSpec((1, tq, HEAD_DIM), lambda bi, h, qi: (bi, qi, qb // HEAD_DIM + 2 * h + 1)),
                head(c, kb, 0), head(c, kb, 1),
                pl.BlockSpec((1, c, dv), lambda bi, h, qi: (bi, 0, vb // dv + h))]
    args = [v.reshape(1, HEAD_DIM) for v in lam_vecs] + [subln.reshape(1, dv), p_q, p_q, p_ctx, p_ctx, p_ctx]
    if has_lat:
        seq = p_lat.shape[1]
        in_specs += [head(seq, kb, 0), head(seq, kb, 1),
                     pl.BlockSpec((1, seq, dv), lambda bi, h, qi: (bi, 0, vb // dv + h))]
        args += [p_lat, p_lat, p_lat]
    return pl.pallas_call(
        functools.partial(_diff_kernel, has_lat=has_lat, lam_init=lam_init),
        grid=(b, B_HEADS, nq // tq),
        in_specs=in_specs,
        out_specs=pl.BlockSpec((1, tq, dv), lambda bi, h, qi: (bi, qi, h)),
        out_shape=jax.ShapeDtypeStruct((b, nq, B_HEADS * dv), BF16),
        compiler_params=_params(("arbitrary", "arbitrary", "arbitrary")),
        name="diff_attention",
    )(*args)


def _natten_kernel(q_ref, kx_ref, vx_ref, kl_ref, vl_ref, tb_ref, o_ref, *, rows_per_step, n_rows):
    blk = pl.program_id(2)
    kx = kx_ref[0]
    vx = vx_ref[0]
    win = NA_ROWS * GRID_W
    for i in range(rows_per_step):
        r = blk * rows_per_step + i
        rs = jnp.clip(r - NA_ROWS // 2, 0, n_rows - NA_ROWS)
        start = pl.multiple_of(rs * GRID_W, GRID_W)
        q = q_ref[0, i * GRID_W:(i + 1) * GRID_W, :]
        kl = kl_ref[0, pl.ds(start, win), :]
        vl = vl_ref[0, pl.ds(start, win), :]
        bias = tb_ref[0, r - rs]
        sl = _nt_dot(q, kl)
        sl = jnp.where(bias > 0.5 * NEG_INF, sl + bias, NEG_INF)
        sx = _nt_dot(q, kx)
        m = jnp.maximum(jnp.max(sx, axis=-1, keepdims=True), jnp.max(sl, axis=-1, keepdims=True))
        px = jnp.exp(sx - m)
        pll = jnp.exp(sl - m)
        den = jnp.sum(px, axis=-1, keepdims=True) + jnp.sum(pll, axis=-1, keepdims=True)
        acc = (jnp.dot(px.astype(BF16), vx, preferred_element_type=F32)
               + jnp.dot(pll.astype(BF16), vl, preferred_element_type=F32))
        o_ref[0, i * GRID_W:(i + 1) * GRID_W, :] = (acc / den).astype(BF16)


def natten_bias_tables(rpb):
    col = jnp.arange(GRID_W)
    cs = jnp.clip(col - NA_COLS // 2, 0, GRID_W - NA_COLS)
    col_mask = (col[None, :] >= cs[:, None]) & (col[None, :] < cs[:, None] + NA_COLS)
    col_idx = jnp.clip(col[None, :] - col[:, None], -(NA_COLS - 1), NA_COLS - 1) + NA_COLS - 1
    rpb_c = rpb[:, :, col_idx]
    d = jnp.arange(NA_ROWS)[:, None]
    a = jnp.arange(NA_ROWS)[None, :]
    roff = a - d + NA_ROWS - 1
    t = rpb_c[:, roff]
    t = jnp.where(col_mask[None, None, None], t, NEG_INF)
    t = jnp.transpose(t, (0, 1, 3, 2, 4))
    return t.reshape(rpb.shape[0], NA_ROWS, GRID_W, NA_ROWS * GRID_W).astype(F32)


def natten_attention(p_lat, p_ctx, tables, *, rows_per_step):
    b, seq, _ = p_lat.shape
    c = p_ctx.shape[1]
    n_rows = seq // GRID_W
    assert n_rows >= NA_ROWS and n_rows % rows_per_step == 0
    qd = (C_HEADS + 2 * C_KV_HEADS) * HEAD_DIM
    kd = qd + D_HEADS * HEAD_DIM
    vd = kd + D_HEADS * HEAD_DIM
    tq = rows_per_step * GRID_W
    win = NA_ROWS * GRID_W

    def col(n, base):
        return pl.BlockSpec((1, n, HEAD_DIM), lambda bi, h, i: (bi, 0, base // HEAD_DIM + h))

    in_specs = [pl.BlockSpec((1, tq, HEAD_DIM), lambda bi, h, i: (bi, i, qd // HEAD_DIM + h)),
                col(c, kd), col(c, vd), col(seq, kd), col(seq, vd),
                pl.BlockSpec((1, NA_ROWS, GRID_W, win), lambda bi, h, i: (h, 0, 0, 0))]
    return pl.pallas_call(
        functools.partial(_natten_kernel, rows_per_step=rows_per_step, n_rows=n_rows),
        grid=(b, D_HEADS, n_rows // rows_per_step),
        in_specs=in_specs,
        out_specs=pl.BlockSpec((1, tq, HEAD_DIM), lambda bi, h, i: (bi, i, h)),
        out_shape=jax.ShapeDtypeStruct((b, seq, D_HEADS * HEAD_DIM), BF16),
        compiler_params=_params(("arbitrary", "arbitrary", "arbitrary")),
        name="natten_attention",
    )(p_lat, p_ctx, p_ctx, p_lat, p_lat, tables)


def _out_proj_kernel(o1_ref, o2_ref, w_ref, x_ref, gt_ref, g_ref, sc_ref, sh_ref, wr_ref, xo_ref, h_ref, aff_ref,
                     *, n_experts):
    y = (jnp.dot(o1_ref[0], w_ref[0], preferred_element_type=F32)
         + jnp.dot(o2_ref[0], w_ref[1], preferred_element_type=F32))
    x = x_ref[0] + gt_ref[0] * y
    xo_ref[0] = x
    ms = jnp.mean(x * x, axis=-1, keepdims=True)
    h = x * lax.rsqrt(ms + EPS) * g_ref[...]
    h = h * (1.0 + sc_ref[0]) + sh_ref[0]
    h_ref[0] = h
    logits = jnp.dot(h.astype(BF16), wr_ref[...], preferred_element_type=F32)
    lane = lax.broadcasted_iota(jnp.int32, logits.shape, 1)
    logits = jnp.where(lane < n_experts, logits, NEG_INF)
    e = jnp.exp(logits - jnp.max(logits, axis=-1, keepdims=True))
    aff_ref[0] = e / jnp.sum(e, axis=-1, keepdims=True)


def out_projection(o1, o2, w_bf16, x, gate1, g_ffn, scale2, shift2, w_router_pad, n_experts, tm):
    b, n, d = x.shape
    k = o1.shape[2]
    assert o2.shape[2] == k and w_bf16.shape == (2, k, d)
    per_batch = gate1.shape[0] > 1
    mod_map = (lambda bi, i: (bi, 0, 0)) if per_batch else (lambda bi, i: (0, 0, 0))
    row = lambda bi, i: (bi, i, 0)
    return pl.pallas_call(
        functools.partial(_out_proj_kernel, n_experts=n_experts),
        grid=(b, n // tm),
        in_specs=[pl.BlockSpec((1, tm, k), row),
                  pl.BlockSpec((1, tm, k), row),
                  pl.BlockSpec((2, k, d), lambda bi, i: (0, 0, 0)),
                  pl.BlockSpec((1, tm, d), row),
                  pl.BlockSpec((1, 1, d), mod_map),
                  pl.BlockSpec((1, d), lambda bi, i: (0, 0)),
                  pl.BlockSpec((1, 1, d), mod_map),
                  pl.BlockSpec((1, 1, d), mod_map),
                  pl.BlockSpec((d, ROUTER_LANES), lambda bi, i: (0, 0))],
        out_specs=[pl.BlockSpec((1, tm, d), row), pl.BlockSpec((1, tm, d), row),
                   pl.BlockSpec((1, tm, ROUTER_LANES), row)],
        out_shape=[jax.ShapeDtypeStruct((b, n, d), F32), jax.ShapeDtypeStruct((b, n, d), F32),
                   jax.ShapeDtypeStruct((b, n, ROUTER_LANES), F32)],
        compiler_params=_params(("arbitrary", "arbitrary")),
        name="out_projection",
    )(o1, o2, w_bf16, x, gate1, g_ffn.reshape(1, d), scale2, shift2, w_router_pad)


def _gather_kernel(idx_ref, h_ref, o_ref, stage_ref, *, cap):
    def body(c, carry):
        t = idx_ref[0, 0, 0, c]
        stage_ref[pl.ds(c, 1), :] = h_ref[0, pl.ds(t, 1), :]
        return carry
    lax.fori_loop(0, cap, body, 0, unroll=8)
    o_ref[0, 0] = stage_ref[...].astype(BF16)


def moe_gather(h, idx, dc):
    b, n, d = h.shape
    _, e, cap = idx.shape
    return pl.pallas_call(
        functools.partial(_gather_kernel, cap=cap),
        grid=(b, d // dc, e),
        in_specs=[pl.BlockSpec((1, 1, 1, cap), lambda bi, j, ei: (bi, ei, 0, 0), memory_space=pltpu.SMEM),
                  pl.BlockSpec((1, n, dc), lambda bi, j, ei: (bi, 0, j))],
        out_specs=pl.BlockSpec((1, 1, cap, dc), lambda bi, j, ei: (bi, ei, 0, j)),
        out_shape=jax.ShapeDtypeStruct((b, e, cap, d), BF16),
        scratch_shapes=[pltpu.VMEM((cap, dc), F32)],
        compiler_params=_params(("arbitrary", "arbitrary", "arbitrary")),
        name="moe_gather",
    )(idx.reshape(b, e, 1, cap), h)


def _expert_kernel(xe_ref, wg_ref, wu_ref, wd_ref, gate_ref, o_ref):
    xe = xe_ref[0, 0]
    a = jnp.dot(xe, wg_ref[0], preferred_element_type=F32)
    u = jnp.dot(xe, wu_ref[0], preferred_element_type=F32)
    hid = (a * (1.0 / (1.0 + jnp.exp(-a))) * u).astype(BF16)
    o_ref[0, 0] = jnp.dot(hid, wd_ref[0], preferred_element_type=F32) * gate_ref[0, 0]


def moe_experts(xe, wg, wu, wd, gate):
    b, e, cap, d = xe.shape
    f = wg.shape[2]
    single = pl.Buffered(1)
    return pl.pallas_call(
        _expert_kernel,
        grid=(e, b),
        in_specs=[pl.BlockSpec((1, 1, cap, d), lambda ei, bi: (bi, ei, 0, 0)),
                  pl.BlockSpec((1, d, f), lambda ei, bi: (ei, 0, 0), pipeline_mode=single),
                  pl.BlockSpec((1, d, f), lambda ei, bi: (ei, 0, 0), pipeline_mode=single),
                  pl.BlockSpec((1, f, d), lambda ei, bi: (ei, 0, 0), pipeline_mode=single),
                  pl.BlockSpec((1, 1, cap, 1), lambda ei, bi: (bi, ei, 0, 0))],
        out_specs=pl.BlockSpec((1, 1, cap, d), lambda ei, bi: (bi, ei, 0, 0)),
        out_shape=jax.ShapeDtypeStruct((b, e, cap, d), F32),
        compiler_params=_params(("arbitrary", "arbitrary")),
        name="moe_experts",
    )(xe, wg, wu, wd, gate)


def _combine_kernel(idx_ref, ye_ref, x_ref, gt_ref, o_ref, *, cap, n_experts):
    ei = pl.program_id(2)

    @pl.when(ei == 0)
    def _():
        o_ref[...] = jnp.zeros_like(o_ref)

    def body(c, carry):
        t = idx_ref[0, 0, 0, c]
        o_ref[0, pl.ds(t, 1), :] = o_ref[0, pl.ds(t, 1), :] + ye_ref[0, 0, pl.ds(c, 1), :]
        return carry
    lax.fori_loop(0, cap, body, 0, unroll=8)

    @pl.when(ei == n_experts - 1)
    def _():
        o_ref[0] = x_ref[0] + gt_ref[0] * o_ref[0]


def moe_combine(ye, idx, x, gate2, dc):
    b, n, d = x.shape
    _, e, cap = idx.shape
    per_batch = gate2.shape[0] > 1
    mod_map = (lambda bi, j, ei: (bi, 0, j)) if per_batch else (lambda bi, j, ei: (0, 0, j))
    return pl.pallas_call(
        functools.partial(_combine_kernel, cap=cap, n_experts=e),
        grid=(b, d // dc, e),
        in_specs=[pl.BlockSpec((1, 1, 1, cap), lambda bi, j, ei: (bi, ei, 0, 0), memory_space=pltpu.SMEM),
                  pl.BlockSpec((1, 1, cap, dc), lambda bi, j, ei: (bi, ei, 0, j)),
                  pl.BlockSpec((1, n, dc), lambda bi, j, ei: (bi, 0, j)),
                  pl.BlockSpec((1, 1, dc), mod_map)],
        out_specs=pl.BlockSpec((1, n, dc), lambda bi, j, ei: (bi, 0, j)),
        out_shape=jax.ShapeDtypeStruct((b, n, d), F32),
        compiler_params=_params(("arbitrary", "arbitrary", "arbitrary")),
        name="moe_combine",
    )(idx.reshape(b, e, 1, cap), ye, x, gate2)


def expert_choice_ffn(h, aff, x, gate2, wg, wu, wd, n_experts):
    b, n, d = h.shape
    cap = max(1, EC_CAPACITY_FACTOR * n // n_experts)
    gate, idx = lax.top_k(jnp.swapaxes(aff[..., :n_experts], 1, 2), cap)
    dc = min(d, 512)
    xe = moe_gather(h, idx, dc)
    ye = moe_experts(xe, wg, wu, wd, gate[..., None])
    return moe_combine(ye, idx, x, gate2, dc)


def _final_norm_kernel(x_ref, g_ref, o_ref):
    x = x_ref[0]
    ms = jnp.mean(x * x, axis=-1, keepdims=True)
    o_ref[0] = x * lax.rsqrt(ms + EPS) * g_ref[...]


def final_norm(x, g, tm):
    b, n, d = x.shape
    return pl.pallas_call(
        _final_norm_kernel,
        grid=(b, n // tm),
        in_specs=[pl.BlockSpec((1, tm, d), lambda bi, i: (bi, i, 0)), pl.BlockSpec((1, d), lambda bi, i: (0, 0))],
        out_specs=pl.BlockSpec((1, tm, d), lambda bi, i: (bi, i, 0)),
        out_shape=jax.ShapeDtypeStruct((b, n, d), F32),
        compiler_params=_params(("arbitrary", "arbitrary")),
        name="final_norm",
    )(x, g.reshape(1, d))


def _rope_tables(n):
    t = jnp.arange(n, dtype=jnp.int32)
    row = (t // GRID_W).astype(F32)
    col = (t % GRID_W).astype(F32)
    n_freq = HEAD_DIM // 4
    inv = ROPE_THETA ** (-jnp.arange(n_freq, dtype=F32) / n_freq)
    ar = row[:, None] * inv
    ac = col[:, None] * inv
    ang = jnp.concatenate([ar, ar, ac, ac], axis=-1)
    sign = jnp.where((jnp.arange(HEAD_DIM) % (2 * n_freq)) < n_freq, -1.0, 1.0).astype(F32)
    return jnp.cos(ang), jnp.sin(ang) * sign


def _tile_codes(layer, rope):
    if layer % 2 == 0:
        groups = [(A_HEADS * HEAD_DIM, ROPE_BIT | SCALE_BIT), (A_KV_HEADS * HEAD_DIM, ROPE_BIT),
                  (A_KV_HEADS * HEAD_DIM, 0), (B_HEADS * 2 * HEAD_DIM, ROPE_BIT | SCALE_BIT),
                  (B_HEADS * 2 * HEAD_DIM, ROPE_BIT), (B_HEADS * 2 * HEAD_DIM, 0)]
    else:
        groups = [(C_HEADS * HEAD_DIM, QNORM_BIT | ROPE_BIT | SCALE_BIT), (C_KV_HEADS * HEAD_DIM, KNORM_BIT | ROPE_BIT),
                  (C_KV_HEADS * HEAD_DIM, 0), (D_HEADS * HEAD_DIM, SCALE_BIT), (D_HEADS * HEAD_DIM, 0),
                  (D_HEADS * HEAD_DIM, 0)]
    codes = []
    for width, code in groups:
        assert width % IN_TN == 0
        codes += [code if rope else code & ~ROPE_BIT] * (width // IN_TN)
    return jnp.asarray(codes, dtype=jnp.int32)


def _lambda_init(layer):
    return 0.8 - 0.6 * math.exp(-0.3 * layer)


def _row_tile(n, target):
    t = min(n, target)
    assert n % t == 0
    return t


def kernel(x, c, ctx, c_ctx, w_ada, b_ada, g_mix, g_ffn, w_in, w_out, a_sink, b_lam_q1, b_lam_k1, b_lam_q2,
           b_lam_k2, b_subln, c_q_norm, c_k_norm, d_rpb, w_router, w_gate, w_up, w_down, g_final):
    bsz, seq, d = x.shape
    n_ctx = ctx.shape[1]
    depth = w_ada.shape[0]
    n_experts = w_router.shape[2]
    cos, sin_signed = _rope_tables(seq)
    ones_head = jnp.ones((1, HEAD_DIM), F32)

    n_rows = -(-(bsz + 1) // 8) * 8
    c_all = jnp.zeros((n_rows, d), F32).at[:bsz].set(c).at[bsz].set(c_ctx)
    mod = ada_modulation(c_all, w_ada, b_ada)

    xc = ctx
    for l in range(depth):
        need_ctx = l < depth - 1
        e = l // 2
        m_lat = mod[l, :bsz].reshape(bsz, 1, 6, d)
        m_ctx = mod[l, bsz:bsz + 1].reshape(1, 1, 6, d)
        sh1, sc1, gt1, sh2, sc2, gt2 = (m_lat[:, :, i] for i in range(6))
        sh1x, sc1x, gt1x, sh2x, sc2x, gt2x = (m_ctx[:, :, i] for i in range(6))
        w_in_l = w_in[l].astype(BF16)
        w_out_l = w_out[l].astype(BF16).reshape(2, MIX_WIDTH // 2, d)
        if l % 2 == 0:
            qn = kn = ones_head
        else:
            qn, kn = c_q_norm[e].reshape(1, HEAD_DIM), c_k_norm[e].reshape(1, HEAD_DIM)
        tm_lat = _row_tile(seq, 1024)
        tm_ctx = _row_tile(n_ctx, 1024)
        p = in_projection(x, g_mix[l], sc1, sh1, w_in_l, _tile_codes(l, True), cos, sin_signed, qn, kn, tm_lat)
        px = in_projection(xc, g_mix[l], sc1x, sh1x, w_in_l, _tile_codes(l, False), cos[:tm_ctx],
                           sin_signed[:tm_ctx], qn, kn, tm_ctx)
        ox1 = ox2 = None
        if l % 2 == 0:
            lam_vecs = (b_lam_q1[e], b_lam_k1[e], b_lam_q2[e], b_lam_k2[e])
            ka = A_HEADS * HEAD_DIM
            va = ka + A_KV_HEADS * HEAD_DIM
            a_cfg = dict(kv_heads=A_KV_HEADS, rep=A_HEADS // A_KV_HEADS, q_col=0, k_col=ka, v_col=va,
                         sink=a_sink[e])
            o1 = gqa_attention(p, px, p, tq=_row_tile(seq, 256), mode='band', **a_cfg)
            o2 = diff_attention(p, px, p, lam_vecs, b_subln[e], lam_init=_lambda_init(l),
                                tq=_row_tile(seq, 256), has_lat=True)
            if need_ctx:
                ox1 = gqa_attention(px, px, None, tq=_row_tile(n_ctx, 256), mode='none', **a_cfg)
                ox2 = diff_attention(px, px, None, lam_vecs, b_subln[e], lam_init=_lambda_init(l),
                                     tq=_row_tile(n_ctx, 256), has_lat=False)
        else:
            kc = C_HEADS * HEAD_DIM
            vc = kc + C_KV_HEADS * HEAD_DIM
            c_cfg = dict(kv_heads=C_KV_HEADS, rep=C_HEADS // C_KV_HEADS, q_col=0, k_col=kc, v_col=vc)
            o1 = gqa_attention(p, px, p, tq=_row_tile(seq, 128), mode='full', **c_cfg)
            o2 = natten_attention(p, px, natten_bias_tables(d_rpb[e]), rows_per_step=4)
            if need_ctx:
                qd = vc + C_KV_HEADS * HEAD_DIM
                kd = qd + D_HEADS * HEAD_DIM
                vd = kd + D_HEADS * HEAD_DIM
                ox1 = gqa_attention(px, px, None, tq=_row_tile(n_ctx, 256), mode='none', **c_cfg)
                ox2 = gqa_attention(px, px, None, kv_heads=D_HEADS, rep=1, q_col=qd, k_col=kd, v_col=vd,
                                    tq=_row_tile(n_ctx, 256), mode='none')
        w_router_pad = jnp.zeros((d, ROUTER_LANES), BF16).at[:, :n_experts].set(w_router[l].astype(BF16))
        wg, wu, wd = w_gate[l].astype(BF16), w_up[l].astype(BF16), w_down[l].astype(BF16)
        x1, h, aff = out_projection(o1, o2, w_out_l, x, gt1, g_ffn[l], sc2, sh2, w_router_pad, n_experts,
                                    _row_tile(seq, 256))
        x = expert_choice_ffn(h, aff, x1, gt2, wg, wu, wd, n_experts)
        if need_ctx:
            xc1, hx, affx = out_projection(ox1, ox2, w_out_l, xc, gt1x, g_ffn[l], sc2x, sh2x, w_router_pad,
                                           n_experts, _row_tile(n_ctx, 256))
            xc = expert_choice_ffn(hx, affx, xc1, gt2x, wg, wu, wd, n_experts)
    return final_norm(x, g_final, _row_tile(seq, 512))
```

```python
import functools
import math

import jax
import jax.numpy as jnp
from jax import lax
from jax.experimental import pallas as pl
from jax.experimental.pallas import tpu as pltpu

F32 = jnp.float32
BF16 = jnp.bfloat16

HEAD_DIM = 128
GRID_W = 64
ROPE_THETA = 10000.0
EPS = 1e-6
NEG_INF = -1e30
A_HEADS, A_KV_HEADS, A_WINDOW = 8, 2, 128
B_HEADS = 4
C_HEADS, C_KV_HEADS = 8, 2
D_HEADS, NA_ROWS, NA_COLS = 8, 8, 16
N_EXPERTS = 16
EC_CAPACITY_FACTOR = 2
IN_WIDTH = 4608
MIX_WIDTH = 2048
ROUTER_LANES = 128

ROPE_BIT, QNORM_BIT, KNORM_BIT, SCALE_BIT = 1, 2, 4, 8
IN_TN = 256

V7X_VMEM_LIMIT = 56 * 1024 * 1024
V7X_VMEM_LIMIT_LARGE = 60 * 1024 * 1024


def _params(sem, vmem=V7X_VMEM_LIMIT):
    return pltpu.CompilerParams(dimension_semantics=sem, vmem_limit_bytes=vmem)


def _nt_dot(a, b):
    return lax.dot_general(a, b, (((1,), (1,)), ((), ())), preferred_element_type=F32)


def _ada_kernel(c_ref, w_ref, b_ref, o_ref):
    c = c_ref[...]
    cond = c * (1.0 / (1.0 + jnp.exp(-c)))
    o_ref[0] = jnp.dot(cond.astype(BF16), w_ref[0].astype(BF16), preferred_element_type=F32) + b_ref[0]


def ada_modulation(c_all, w_ada, b_ada):
    depth, d, n = w_ada.shape
    r = c_all.shape[0]
    tn = math.gcd(n, 1024)
    return pl.pallas_call(
        _ada_kernel,
        grid=(depth, n // tn),
        in_specs=[pl.BlockSpec((r, d), lambda l, j: (0, 0)),
                  pl.BlockSpec((1, d, tn), lambda l, j: (l, 0, j)),
                  pl.BlockSpec((1, 1, tn), lambda l, j: (l, 0, j))],
        out_specs=pl.BlockSpec((1, r, tn), lambda l, j: (l, 0, j)),
        out_shape=jax.ShapeDtypeStruct((depth, r, n), F32),
        compiler_params=_params(("arbitrary", "arbitrary")),
        name="ada_modulation",
    )(c_all, w_ada, b_ada.reshape(depth, 1, n))


def _rope_rotate(y, sin_signed):
    lane = lax.broadcasted_iota(jnp.int32, y.shape, 1)
    first = (lane & 63) < 32
    return jnp.where(first, pltpu.roll(y, 96, 1), pltpu.roll(y, 32, 1)) * sin_signed


def _in_proj_kernel(codes_ref, x_ref, g_ref, sc_ref, sh_ref, w_ref, cos_ref, sin_ref, qn_ref, kn_ref,
                    o_ref, xn_ref, y_ref, *, scale):
    j = pl.program_id(2)

    @pl.when(j == 0)
    def _():
        x = x_ref[0]
        ms = jnp.mean(x * x, axis=-1, keepdims=True)
        y = x * lax.rsqrt(ms + EPS) * g_ref[...]
        xn_ref[...] = (y * (1.0 + sc_ref[0]) + sh_ref[0]).astype(BF16)

    y_ref[...] = jnp.dot(xn_ref[...], w_ref[...], preferred_element_type=F32)
    code = codes_ref[j]
    n_heads = y_ref.shape[1] // HEAD_DIM

    def head_norm(gn_ref):
        for h in range(n_heads):
            sl = slice(h * HEAD_DIM, (h + 1) * HEAD_DIM)
            y = y_ref[:, sl]
            ms = jnp.mean(y * y, axis=-1, keepdims=True)
            y_ref[:, sl] = y * lax.rsqrt(ms + EPS) * gn_ref[...]

    @pl.when((code & QNORM_BIT) != 0)
    def _():
        head_norm(qn_ref)

    @pl.when((code & KNORM_BIT) != 0)
    def _():
        head_norm(kn_ref)

    @pl.when((code & ROPE_BIT) != 0)
    def _():
        for h in range(n_heads):
            sl = slice(h * HEAD_DIM, (h + 1) * HEAD_DIM)
            y = y_ref[:, sl]
            y_ref[:, sl] = y * cos_ref[...] + _rope_rotate(y, sin_ref[...])

    @pl.when((code & SCALE_BIT) != 0)
    def _():
        y_ref[...] = y_ref[...] * scale

    o_ref[0] = y_ref[...].astype(BF16)


def in_projection(x, g, scale1, shift1, w_bf16, codes, cos, sin_signed, qn, kn, tm):
    b, n, d = x.shape
    width = w_bf16.shape[1]
    per_batch = scale1.shape[0] > 1
    mod_map = (lambda bi, i, j, c: (bi, 0, 0)) if per_batch else (lambda bi, i, j, c: (0, 0, 0))
    grid_spec = pltpu.PrefetchScalarGridSpec(
        num_scalar_prefetch=1,
        grid=(b, n // tm, width // IN_TN),
        in_specs=[pl.BlockSpec((1, tm, d), lambda bi, i, j, c: (bi, i, 0)),
                  pl.BlockSpec((1, d), lambda bi, i, j, c: (0, 0)),
                  pl.BlockSpec((1, 1, d), mod_map),
                  pl.BlockSpec((1, 1, d), mod_map),
                  pl.BlockSpec((d, IN_TN), lambda bi, i, j, c: (0, j)),
                  pl.BlockSpec((tm, HEAD_DIM), lambda bi, i, j, c: (i, 0)),
                  pl.BlockSpec((tm, HEAD_DIM), lambda bi, i, j, c: (i, 0)),
                  pl.BlockSpec((1, HEAD_DIM), lambda bi, i, j, c: (0, 0)),
                  pl.BlockSpec((1, HEAD_DIM), lambda bi, i, j, c: (0, 0))],
        out_specs=pl.BlockSpec((1, tm, IN_TN), lambda bi, i, j, c: (bi, i, j)),
        scratch_shapes=[pltpu.VMEM((tm, d), BF16), pltpu.VMEM((tm, IN_TN), F32)])
    return pl.pallas_call(
        functools.partial(_in_proj_kernel, scale=HEAD_DIM ** -0.5),
        grid_spec=grid_spec,
        out_shape=jax.ShapeDtypeStruct((b, n, width), BF16),
        compiler_params=_params(("arbitrary", "arbitrary", "arbitrary")),
        name="in_projection",
    )(codes, x, g.reshape(1, d), scale1, shift1, w_bf16, cos, sin_signed, qn, kn)


KEY_CHUNK = 256


class _SoftmaxChain:
    def __init__(self, q, chunks, s_ref, dv, sink=None):
        self.q, self.chunks, self.s_ref, self.sink = q, chunks, s_ref, sink
        m = q.shape[0]
        self.mx = jnp.full((m, HEAD_DIM), NEG_INF, F32)
        self.ls = jnp.zeros((m, HEAD_DIM), F32)
        self.acc = jnp.zeros((m, dv), F32)

    def score_steps(self):
        return [functools.partial(self._score, j) for j in range(len(self.chunks))]

    def value_steps(self):
        return [functools.partial(self._value, j) for j in range(len(self.chunks))]

    def _score(self, j):
        k_fn, _, bias_fn = self.chunks[j]
        s = _nt_dot(self.q, k_fn())
        if bias_fn is not None:
            s = s + bias_fn()
        self.s_ref[:, j * KEY_CHUNK:(j + 1) * KEY_CHUNK] = s
        for t in range(KEY_CHUNK // HEAD_DIM):
            self.mx = jnp.maximum(self.mx, s[:, t * HEAD_DIM:(t + 1) * HEAD_DIM])

    def finish_scores(self):
        m = jnp.max(self.mx, axis=-1, keepdims=True)
        if self.sink is not None:
            m = jnp.maximum(m, self.sink)
        self.m = m
        self.m_wide = jnp.broadcast_to(m, self.mx.shape)

    def _value(self, j):
        _, v_fn, _ = self.chunks[j]
        parts = []
        for t in range(KEY_CHUNK // HEAD_DIM):
            lo = j * KEY_CHUNK + t * HEAD_DIM
            e = jnp.exp(self.s_ref[:, lo:lo + HEAD_DIM] - self.m_wide)
            self.ls = self.ls + e
            parts.append(e.astype(BF16))
        p = jnp.concatenate(parts, axis=1)
        self.acc = self.acc + jnp.dot(p, v_fn(), preferred_element_type=F32)

    def finish(self):
        den = jnp.sum(self.ls, axis=-1, keepdims=True)
        if self.sink is not None:
            den = den + jnp.exp(self.sink - self.m)
        return self.acc, den


def _run_chains(chains):
    prev = None
    for chain in chains:
        scores = chain.score_steps()
        values = prev.value_steps() if prev is not None else []
        for i in range(max(len(scores), len(values))):
            if i < len(values):
                values[i]()
            if i < len(scores):
                scores[i]()
        chain.finish_scores()
        prev = chain
    for step in prev.value_steps():
        step()


def _ref_chunks(k_ref, v_ref, start, length, bias=None, k_cols=slice(None), v_cols=slice(None)):
    assert length % KEY_CHUNK == 0
    out = []
    for j in range(length // KEY_CHUNK):
        if isinstance(start, int):
            rows = slice(start + j * KEY_CHUNK, start + (j + 1) * KEY_CHUNK)
        else:
            rows = pl.ds(start + j * KEY_CHUNK, KEY_CHUNK)
        out.append((lambda rows=rows: k_ref[0, rows, k_cols], lambda rows=rows: v_ref[0, rows, v_cols],
                    (lambda j=j: bias(j)) if bias is not None else None))
    return out


def _gqa_kernel(*refs, rep, tq, mode, seq, use_sink):
    if use_sink:
        sink_ref, refs = refs[0], refs[1:]
    s_ref = refs[-1]
    refs = refs[:-1]
    if mode == 'none':
        q_ref, kx_ref, vx_ref, o_ref = refs
    else:
        q_ref, kx_ref, vx_ref, kl_ref, vl_ref, o_ref = refs
    g = pl.program_id(1)
    qi = pl.program_id(2)
    chunks = _ref_chunks(kx_ref, vx_ref, 0, kx_ref.shape[1])
    if mode == 'full':
        chunks += _ref_chunks(kl_ref, vl_ref, 0, seq)
    elif mode == 'band':
        band = tq + 2 * A_WINDOW
        start = pl.multiple_of(jnp.clip(qi * tq - A_WINDOW, 0, seq - band), HEAD_DIM)
        rel = (lax.broadcasted_iota(jnp.int32, (tq, KEY_CHUNK), 0)
               - lax.broadcasted_iota(jnp.int32, (tq, KEY_CHUNK), 1))

        def band_bias(j):
            dist = rel + (qi * tq - start - j * KEY_CHUNK)
            return jnp.where(jnp.abs(dist) <= A_WINDOW, 0.0, NEG_INF)

        chunks += _ref_chunks(kl_ref, vl_ref, start, band, band_bias)
    q = q_ref[0]
    chains = [_SoftmaxChain(q[:, r * HEAD_DIM:(r + 1) * HEAD_DIM], chunks, s_ref.at[r], HEAD_DIM,
                            sink=sink_ref[g * rep + r] if use_sink else None) for r in range(rep)]
    _run_chains(chains)
    for r, chain in enumerate(chains):
        acc, den = chain.finish()
        o_ref[0, :, r * HEAD_DIM:(r + 1) * HEAD_DIM] = (acc / den).astype(BF16)


def gqa_attention(p_q, p_ctx, p_lat, *, kv_heads, rep, q_col, k_col, v_col, tq, mode, sink=None):
    b, nq, _ = p_q.shape
    c = p_ctx.shape[1]
    qw = rep * HEAD_DIM
    use_sink = sink is not None
    assert tq & (tq - 1) == 0
    in_specs = [pl.BlockSpec((1, tq, qw), lambda bi, g, qi, *_: (bi, qi, q_col // qw + g)),
                pl.BlockSpec((1, c, HEAD_DIM), lambda bi, g, qi, *_: (bi, 0, k_col // HEAD_DIM + g)),
                pl.BlockSpec((1, c, HEAD_DIM), lambda bi, g, qi, *_: (bi, 0, v_col // HEAD_DIM + g))]
    args = [p_q, p_ctx, p_ctx]
    seq = 0
    if mode != 'none':
        seq = p_lat.shape[1]
        in_specs += [pl.BlockSpec((1, seq, HEAD_DIM), lambda bi, g, qi, *_: (bi, 0, k_col // HEAD_DIM + g)),
                     pl.BlockSpec((1, seq, HEAD_DIM), lambda bi, g, qi, *_: (bi, 0, v_col // HEAD_DIM + g))]
        args += [p_lat, p_lat]
    if use_sink:
        args = [sink] + args
    n_keys = c + {'none': 0, 'full': seq, 'band': tq + 2 * A_WINDOW}[mode]
    grid_spec = pltpu.PrefetchScalarGridSpec(
        num_scalar_prefetch=1 if use_sink else 0,
        grid=(b, kv_heads, nq // tq),
        in_specs=in_specs,
        out_specs=pl.BlockSpec((1, tq, qw), lambda bi, g, qi, *_: (bi, qi, g)),
        scratch_shapes=[pltpu.VMEM((rep, tq, n_keys), F32)])
    return pl.pallas_call(
        functools.partial(_gqa_kernel, rep=rep, tq=tq, mode=mode, seq=seq, use_sink=use_sink),
        grid_spec=grid_spec,
        out_shape=jax.ShapeDtypeStruct((b, nq, kv_heads * qw), BF16),
        compiler_params=_params(("arbitrary", "arbitrary", "arbitrary")),
        name=f"gqa_attention_{mode}",
    )(*args)


def _diff_kernel(*refs, has_lat, lam_init):
    if has_lat:
        lq1, lk1, lq2, lk2, sub_ref, q_ref, kx_ref, vx_ref, kl_ref, vl_ref, o_ref, s_ref = refs
    else:
        lq1, lk1, lq2, lk2, sub_ref, q_ref, kx_ref, vx_ref, o_ref, s_ref = refs
    lam = (jnp.exp(jnp.sum(lq1[...] * lk1[...], axis=-1, keepdims=True))
           - jnp.exp(jnp.sum(lq2[...] * lk2[...], axis=-1, keepdims=True)) + lam_init)
    n_ctx = vx_ref.shape[1]
    dv = 2 * HEAD_DIM
    chains = []
    for h in range(DIFF_HEADS_PER_STEP):
        v_cols = slice(h * dv, (h + 1) * dv)
        for m in range(2):
            k_cols = slice((2 * h + m) * HEAD_DIM, (2 * h + m + 1) * HEAD_DIM)
            chunks = _ref_chunks(kx_ref, vx_ref, 0, n_ctx, k_cols=k_cols, v_cols=v_cols)
            if has_lat:
                chunks += _ref_chunks(kl_ref, vl_ref, 0, vl_ref.shape[1], k_cols=k_cols, v_cols=v_cols)
            chains.append(_SoftmaxChain(q_ref[0, :, k_cols], chunks, s_ref.at[2 * h + m], dv))
    _run_chains(chains)
    for h in range(DIFF_HEADS_PER_STEP):
        acc1, den1 = chains[2 * h].finish()
        acc2, den2 = chains[2 * h + 1].finish()
        o = acc1 / den1 - lam * (acc2 / den2)
        ms = jnp.mean(o * o, axis=-1, keepdims=True)
        o_ref[0, :, h * dv:(h + 1) * dv] = (o * lax.rsqrt(ms + EPS) * sub_ref[...] * (1.0 - lam_init)).astype(BF16)


DIFF_HEADS_PER_STEP = 2


def diff_attention(p_q, p_ctx, p_lat, lam_vecs, subln, *, lam_init, tq, has_lat):
    b, nq, _ = p_q.shape
    c = p_ctx.shape[1]
    qb = (A_HEADS + 2 * A_KV_HEADS) * HEAD_DIM
    kb = qb + B_HEADS * 2 * HEAD_DIM
    vb = kb + B_HEADS * 2 * HEAD_DIM
    dv = 2 * HEAD_DIM
    wide = DIFF_HEADS_PER_STEP * dv
    assert qb % wide == 0 and kb % wide == 0 and vb % wide == 0 and B_HEADS % DIFF_HEADS_PER_STEP == 0
    vec = pl.BlockSpec((1, HEAD_DIM), lambda bi, h, qi: (0, 0))

    def cols(n, base):
        return pl.BlockSpec((1, n, wide), lambda bi, h, qi: (bi, 0, base // wide + h))

    in_specs = [vec, vec, vec, vec, pl.BlockSpec((1, dv), lambda bi, h, qi: (0, 0)),
                pl.BlockSpec((1, tq, wide), lambda bi, h, qi: (bi, qi, qb // wide + h)),
                cols(c, kb), cols(c, vb)]
    args = [v.reshape(1, HEAD_DIM) for v in lam_vecs] + [subln.reshape(1, dv), p_q, p_ctx, p_ctx]
    seq = 0
    if has_lat:
        seq = p_lat.shape[1]
        in_specs += [cols(seq, kb), cols(seq, vb)]
        args += [p_lat, p_lat]
    return pl.pallas_call(
        functools.partial(_diff_kernel, has_lat=has_lat, lam_init=lam_init),
        grid=(b, B_HEADS // DIFF_HEADS_PER_STEP, nq // tq),
        in_specs=in_specs,
        out_specs=pl.BlockSpec((1, tq, wide), lambda bi, h, qi: (bi, qi, h)),
        out_shape=jax.ShapeDtypeStruct((b, nq, B_HEADS * dv), BF16),
        scratch_shapes=[pltpu.VMEM((2 * DIFF_HEADS_PER_STEP, tq, c + seq), F32)],
        compiler_params=_params(("arbitrary", "arbitrary", "arbitrary")),
        name="diff_attention",
    )(*args)


NA_QROWS = 4
NA_WROWS = NA_QROWS + NA_ROWS
NA_BLOCKS_PER_STEP = 2


def _natten_kernel(q_ref, kx_ref, vx_ref, kl_ref, vl_ref, tb_ref, o_ref, s_ref, *, n_rows):
    n_blocks = n_rows // NA_QROWS
    tq = NA_QROWS * GRID_W
    n_ctx = kx_ref.shape[1]
    chains = []
    for i in range(NA_BLOCKS_PER_STEP):
        blk = pl.program_id(2) * NA_BLOCKS_PER_STEP + i
        first_row = jnp.clip(blk * NA_QROWS - NA_ROWS // 2, 0, n_rows - NA_WROWS)
        start = pl.multiple_of(first_row * GRID_W, GRID_W)
        variant = jnp.where(blk == 0, 0, jnp.where(blk == n_blocks - 1, 2, 1))

        def bias(j, variant=variant):
            return tb_ref[0, variant, :, j * KEY_CHUNK:(j + 1) * KEY_CHUNK]

        chunks = (_ref_chunks(kx_ref, vx_ref, 0, n_ctx)
                  + _ref_chunks(kl_ref, vl_ref, start, NA_WROWS * GRID_W, bias))
        chains.append(_SoftmaxChain(q_ref[0, i * tq:(i + 1) * tq, :], chunks, s_ref.at[i], HEAD_DIM))
    _run_chains(chains)
    for i, chain in enumerate(chains):
        acc, den = chain.finish()
        o_ref[0, i * tq:(i + 1) * tq, :] = (acc / den).astype(BF16)


def natten_bias_slabs(rpb):
    col = jnp.arange(GRID_W)
    cs = jnp.clip(col - NA_COLS // 2, 0, GRID_W - NA_COLS)
    col_mask = (col[None, :] >= cs[:, None]) & (col[None, :] < cs[:, None] + NA_COLS)
    col_idx = jnp.clip(col[None, :] - col[:, None], -(NA_COLS - 1), NA_COLS - 1) + NA_COLS - 1
    rpb_c = jnp.where(col_mask[None, None], rpb[:, :, col_idx], NEG_INF)
    i = jnp.arange(NA_QROWS)[:, None]
    a = jnp.arange(NA_WROWS)[None, :]
    half = NA_ROWS // 2
    kinds = [(a < NA_ROWS, a - i + NA_ROWS - 1),
             ((a >= i) & (a < i + NA_ROWS), a - i - half + NA_ROWS - 1),
             (a >= NA_WROWS - NA_ROWS, a - i - NA_ROWS + NA_ROWS - 1)]
    slabs = []
    for valid, roff in kinds:
        t = rpb_c[:, jnp.clip(roff, 0, 2 * NA_ROWS - 2)]
        t = jnp.where(valid[None, :, :, None, None], t, NEG_INF)
        slabs.append(jnp.transpose(t, (0, 1, 3, 2, 4)).reshape(rpb.shape[0], NA_QROWS * GRID_W, NA_WROWS * GRID_W))
    return jnp.stack(slabs, axis=1).astype(F32)


def natten_attention(p_lat, p_ctx, slabs):
    b, seq, _ = p_lat.shape
    c = p_ctx.shape[1]
    n_rows = seq // GRID_W
    rows_per_step = NA_QROWS * NA_BLOCKS_PER_STEP
    assert n_rows >= NA_WROWS and n_rows % rows_per_step == 0
    qd = (C_HEADS + 2 * C_KV_HEADS) * HEAD_DIM
    kd = qd + D_HEADS * HEAD_DIM
    vd = kd + D_HEADS * HEAD_DIM
    tq = rows_per_step * GRID_W

    def col(n, base):
        return pl.BlockSpec((1, n, HEAD_DIM), lambda bi, h, i: (bi, 0, base // HEAD_DIM + h))

    in_specs = [pl.BlockSpec((1, tq, HEAD_DIM), lambda bi, h, i: (bi, i, qd // HEAD_DIM + h)),
                col(c, kd), col(c, vd), col(seq, kd), col(seq, vd),
                pl.BlockSpec((1, 3, NA_QROWS * GRID_W, NA_WROWS * GRID_W), lambda bi, h, i: (h, 0, 0, 0))]
    return pl.pallas_call(
        functools.partial(_natten_kernel, n_rows=n_rows),
        grid=(b, D_HEADS, n_rows // rows_per_step),
        in_specs=in_specs,
        out_specs=pl.BlockSpec((1, tq, HEAD_DIM), lambda bi, h, i: (bi, i, h)),
        out_shape=jax.ShapeDtypeStruct((b, seq, D_HEADS * HEAD_DIM), BF16),
        scratch_shapes=[pltpu.VMEM((NA_BLOCKS_PER_STEP, NA_QROWS * GRID_W, c + NA_WROWS * GRID_W), F32)],
        compiler_params=_params(("arbitrary", "arbitrary", "arbitrary")),
        name="natten_attention",
    )(p_lat, p_ctx, p_ctx, p_lat, p_lat, slabs)


def _out_proj_kernel(o1_ref, o2_ref, w_ref, x_ref, gt_ref, g_ref, sc_ref, sh_ref, wr_ref, xo_ref, h_ref, aff_ref,
                     *, n_experts):
    y = (jnp.dot(o1_ref[0], w_ref[0], preferred_element_type=F32)
         + jnp.dot(o2_ref[0], w_ref[1], preferred_element_type=F32))
    x = x_ref[0] + gt_ref[0] * y
    xo_ref[0] = x
    ms = jnp.mean(x * x, axis=-1, keepdims=True)
    h = x * lax.rsqrt(ms + EPS) * g_ref[...]
    h = h * (1.0 + sc_ref[0]) + sh_ref[0]
    h_ref[0] = h
    logits = jnp.dot(h.astype(BF16), wr_ref[...], preferred_element_type=F32)
    lane = lax.broadcasted_iota(jnp.int32, logits.shape, 1)
    logits = jnp.where(lane < n_experts, logits, NEG_INF)
    e = jnp.exp(logits - jnp.max(logits, axis=-1, keepdims=True))
    aff_ref[0] = e / jnp.sum(e, axis=-1, keepdims=True)


def out_projection(o1, o2, w_bf16, x, gate1, g_ffn, scale2, shift2, w_router_pad, n_experts, tm):
    b, n, d = x.shape
    k = o1.shape[2]
    assert o2.shape[2] == k and w_bf16.shape == (2, k, d)
    per_batch = gate1.shape[0] > 1
    mod_map = (lambda bi, i: (bi, 0, 0)) if per_batch else (lambda bi, i: (0, 0, 0))
    row = lambda bi, i: (bi, i, 0)
    return pl.pallas_call(
        functools.partial(_out_proj_kernel, n_experts=n_experts),
        grid=(b, n // tm),
        in_specs=[pl.BlockSpec((1, tm, k), row),
                  pl.BlockSpec((1, tm, k), row),
                  pl.BlockSpec((2, k, d), lambda bi, i: (0, 0, 0)),
                  pl.BlockSpec((1, tm, d), row),
                  pl.BlockSpec((1, 1, d), mod_map),
                  pl.BlockSpec((1, d), lambda bi, i: (0, 0)),
                  pl.BlockSpec((1, 1, d), mod_map),
                  pl.BlockSpec((1, 1, d), mod_map),
                  pl.BlockSpec((d, ROUTER_LANES), lambda bi, i: (0, 0))],
        out_specs=[pl.BlockSpec((1, tm, d), row), pl.BlockSpec((1, tm, d), row),
                   pl.BlockSpec((1, tm, ROUTER_LANES), row)],
        out_shape=[jax.ShapeDtypeStruct((b, n, d), F32), jax.ShapeDtypeStruct((b, n, d), F32),
                   jax.ShapeDtypeStruct((b, n, ROUTER_LANES), F32)],
        compiler_params=_params(("arbitrary", "arbitrary")),
        name="out_projection",
    )(o1, o2, w_bf16, x, gate1, g_ffn.reshape(1, d), scale2, shift2, w_router_pad)


MOE_ROW_CHUNK = 1024


def _gather_kernel(idx_ref, h_ref, o_ref, stage_ref, *, cap):
    def body(c, carry):
        t = idx_ref[0, 0, 0, c]
        stage_ref[pl.ds(c, 1), :] = h_ref[0, pl.ds(t, 1), :]
        return carry
    lax.fori_loop(0, cap, body, 0, unroll=8)
    o_ref[0, 0] = stage_ref[...].astype(BF16)


def moe_gather(h, idx, dc):
    b, n, d = h.shape
    _, e, cap = idx.shape
    return pl.pallas_call(
        functools.partial(_gather_kernel, cap=cap),
        grid=(b, d // dc, e),
        in_specs=[pl.BlockSpec((1, 1, 1, cap), lambda bi, j, ei: (bi, ei, 0, 0), memory_space=pltpu.SMEM),
                  pl.BlockSpec((1, n, dc), lambda bi, j, ei: (bi, 0, j))],
        out_specs=pl.BlockSpec((1, 1, cap, dc), lambda bi, j, ei: (bi, ei, 0, j)),
        out_shape=jax.ShapeDtypeStruct((b, e, cap, d), BF16),
        scratch_shapes=[pltpu.VMEM((cap, dc), F32)],
        compiler_params=_params(("arbitrary", "arbitrary", "arbitrary")),
        name="moe_gather",
    )(idx.reshape(b, e, 1, cap), h)


def _expert_kernel(xe_ref, wg_ref, wu_ref, wd_ref, gate_ref, o_ref):
    bs, _, cap, d = xe_ref.shape
    xe = xe_ref[...].reshape(bs * cap, d)
    a = jnp.dot(xe, wg_ref[0], preferred_element_type=F32)
    u = jnp.dot(xe, wu_ref[0], preferred_element_type=F32)
    hid = (a * (1.0 / (1.0 + jnp.exp(-a))) * u).astype(BF16)
    ye = jnp.dot(hid, wd_ref[0], preferred_element_type=F32) * gate_ref[...].reshape(bs * cap, 1)
    o_ref[...] = ye.reshape(bs, 1, cap, d)


def moe_experts(xe, wg, wu, wd, gate):
    b, e, cap, d = xe.shape
    f = wg.shape[2]
    single = pl.Buffered(1)
    bs = max(1, min(b, 512 // cap))
    assert b % bs == 0
    return pl.pallas_call(
        _expert_kernel,
        grid=(e, b // bs),
        in_specs=[pl.BlockSpec((bs, 1, cap, d), lambda ei, bi: (bi, ei, 0, 0)),
                  pl.BlockSpec((1, d, f), lambda ei, bi: (ei, 0, 0), pipeline_mode=single),
                  pl.BlockSpec((1, d, f), lambda ei, bi: (ei, 0, 0), pipeline_mode=single),
                  pl.BlockSpec((1, f, d), lambda ei, bi: (ei, 0, 0), pipeline_mode=single),
                  pl.BlockSpec((bs, 1, cap, 1), lambda ei, bi: (bi, ei, 0, 0))],
        out_specs=pl.BlockSpec((bs, 1, cap, d), lambda ei, bi: (bi, ei, 0, 0)),
        out_shape=jax.ShapeDtypeStruct((b, e, cap, d), F32),
        compiler_params=_params(("arbitrary", "arbitrary")),
        name="moe_experts",
    )(xe, wg, wu, wd, gate)


def _combine_kernel(idx_ref, ye_ref, x_ref, gt_ref, o_ref, *, cap, n_experts):
    ei = pl.program_id(2)

    @pl.when(ei == 0)
    def _():
        o_ref[...] = jnp.zeros_like(o_ref)

    def body(c, carry):
        t = idx_ref[0, 0, 0, c]
        o_ref[0, pl.ds(t, 1), :] = o_ref[0, pl.ds(t, 1), :] + ye_ref[0, 0, pl.ds(c, 1), :]
        return carry
    lax.fori_loop(0, cap, body, 0, unroll=8)

    @pl.when(ei == n_experts - 1)
    def _():
        o_ref[0] = x_ref[0] + gt_ref[0] * o_ref[0]


def moe_combine(ye, idx, x, gate2, dc):
    b, n, d = x.shape
    _, e, cap = idx.shape
    per_batch = gate2.shape[0] > 1
    mod_map = (lambda bi, j, ei: (bi, 0, j)) if per_batch else (lambda bi, j, ei: (0, 0, j))
    return pl.pallas_call(
        functools.partial(_combine_kernel, cap=cap, n_experts=e),
        grid=(b, d // dc, e),
        in_specs=[pl.BlockSpec((1, 1, 1, cap), lambda bi, j, ei: (bi, ei, 0, 0), memory_space=pltpu.SMEM),
                  pl.BlockSpec((1, 1, cap, dc), lambda bi, j, ei: (bi, ei, 0, j)),
                  pl.BlockSpec((1, n, dc), lambda bi, j, ei: (bi, 0, j), pipeline_mode=pl.Buffered(1)),
                  pl.BlockSpec((1, 1, dc), mod_map)],
        out_specs=pl.BlockSpec((1, n, dc), lambda bi, j, ei: (bi, 0, j)),
        out_shape=jax.ShapeDtypeStruct((b, n, d), F32),
        compiler_params=_params(("arbitrary", "arbitrary", "arbitrary"), V7X_VMEM_LIMIT_LARGE),
        name="moe_combine",
    )(idx.reshape(b, e, 1, cap), ye, x, gate2)


def expert_choice_ffn(h, aff, x, gate2, wg, wu, wd, n_experts):
    b, n, d = h.shape
    cap = max(1, EC_CAPACITY_FACTOR * n // n_experts)
    gate, idx = lax.top_k(jnp.swapaxes(aff[..., :n_experts], 1, 2), cap)
    dc = min(d, MOE_ROW_CHUNK)
    xe = moe_gather(h, idx, dc)
    ye = moe_experts(xe, wg, wu, wd, gate[..., None])
    return moe_combine(ye, idx, x, gate2, dc)


def _final_norm_kernel(x_ref, g_ref, o_ref):
    x = x_ref[0]
    ms = jnp.mean(x * x, axis=-1, keepdims=True)
    o_ref[0] = x * lax.rsqrt(ms + EPS) * g_ref[...]


def final_norm(x, g, tm):
    b, n, d = x.shape
    return pl.pallas_call(
        _final_norm_kernel,
        grid=(b, n // tm),
        in_specs=[pl.BlockSpec((1, tm, d), lambda bi, i: (bi, i, 0)), pl.BlockSpec((1, d), lambda bi, i: (0, 0))],
        out_specs=pl.BlockSpec((1, tm, d), lambda bi, i: (bi, i, 0)),
        out_shape=jax.ShapeDtypeStruct((b, n, d), F32),
        compiler_params=_params(("arbitrary", "arbitrary")),
        name="final_norm",
    )(x, g.reshape(1, d))


def _rope_tables(n):
    t = jnp.arange(n, dtype=jnp.int32)
    row = (t // GRID_W).astype(F32)
    col = (t % GRID_W).astype(F32)
    n_freq = HEAD_DIM // 4
    inv = ROPE_THETA ** (-jnp.arange(n_freq, dtype=F32) / n_freq)
    ar = row[:, None] * inv
    ac = col[:, None] * inv
    ang = jnp.concatenate([ar, ar, ac, ac], axis=-1)
    sign = jnp.where((jnp.arange(HEAD_DIM) % (2 * n_freq)) < n_freq, -1.0, 1.0).astype(F32)
    return jnp.cos(ang), jnp.sin(ang) * sign


def _tile_codes(layer, rope):
    if layer % 2 == 0:
        groups = [(A_HEADS * HEAD_DIM, ROPE_BIT | SCALE_BIT), (A_KV_HEADS * HEAD_DIM, ROPE_BIT),
                  (A_KV_HEADS * HEAD_DIM, 0), (B_HEADS * 2 * HEAD_DIM, ROPE_BIT | SCALE_BIT),
                  (B_HEADS * 2 * HEAD_DIM, ROPE_BIT), (B_HEADS * 2 * HEAD_DIM, 0)]
    else:
        groups = [(C_HEADS * HEAD_DIM, QNORM_BIT | ROPE_BIT | SCALE_BIT), (C_KV_HEADS * HEAD_DIM, KNORM_BIT | ROPE_BIT),
                  (C_KV_HEADS * HEAD_DIM, 0), (D_HEADS * HEAD_DIM, SCALE_BIT), (D_HEADS * HEAD_DIM, 0),
                  (D_HEADS * HEAD_DIM, 0)]
    codes = []
    for width, code in groups:
        assert width % IN_TN == 0
        codes += [code if rope else code & ~ROPE_BIT] * (width // IN_TN)
    return jnp.asarray(codes, dtype=jnp.int32)


def _lambda_init(layer):
    return 0.8 - 0.6 * math.exp(-0.3 * layer)


def _row_tile(n, target):
    t = min(n, target)
    assert n % t == 0
    return t


def kernel(x, c, ctx, c_ctx, w_ada, b_ada, g_mix, g_ffn, w_in, w_out, a_sink, b_lam_q1, b_lam_k1, b_lam_q2,
           b_lam_k2, b_subln, c_q_norm, c_k_norm, d_rpb, w_router, w_gate, w_up, w_down, g_final):
    bsz, seq, d = x.shape
    n_ctx = ctx.shape[1]
    depth = w_ada.shape[0]
    n_experts = w_router.shape[2]
    cos, sin_signed = _rope_tables(seq)
    ones_head = jnp.ones((1, HEAD_DIM), F32)

    n_rows = -(-(bsz + 1) // 8) * 8
    c_all = jnp.zeros((n_rows, d), F32).at[:bsz].set(c).at[bsz].set(c_ctx)
    mod = ada_modulation(c_all, w_ada, b_ada)

    xc = ctx
    for l in range(depth):
        need_ctx = l < depth - 1
        e = l // 2
        m_lat = mod[l, :bsz].reshape(bsz, 1, 6, d)
        m_ctx = mod[l, bsz:bsz + 1].reshape(1, 1, 6, d)
        sh1, sc1, gt1, sh2, sc2, gt2 = (m_lat[:, :, i] for i in range(6))
        sh1x, sc1x, gt1x, sh2x, sc2x, gt2x = (m_ctx[:, :, i] for i in range(6))
        w_in_l = w_in[l].astype(BF16)
        w_out_l = w_out[l].astype(BF16).reshape(2, MIX_WIDTH // 2, d)
        if l % 2 == 0:
            qn = kn = ones_head
        else:
            qn, kn = c_q_norm[e].reshape(1, HEAD_DIM), c_k_norm[e].reshape(1, HEAD_DIM)
        tm_lat = _row_tile(seq, 1024)
        tm_ctx = _row_tile(n_ctx, 1024)
        p = in_projection(x, g_mix[l], sc1, sh1, w_in_l, _tile_codes(l, True), cos, sin_signed, qn, kn, tm_lat)
        px = in_projection(xc, g_mix[l], sc1x, sh1x, w_in_l, _tile_codes(l, False), cos[:tm_ctx],
                           sin_signed[:tm_ctx], qn, kn, tm_ctx)
        ox1 = ox2 = None
        if l % 2 == 0:
            lam_vecs = (b_lam_q1[e], b_lam_k1[e], b_lam_q2[e], b_lam_k2[e])
            ka = A_HEADS * HEAD_DIM
            va = ka + A_KV_HEADS * HEAD_DIM
            a_cfg = dict(kv_heads=A_KV_HEADS, rep=A_HEADS // A_KV_HEADS, q_col=0, k_col=ka, v_col=va,
                         sink=a_sink[e])
            o1 = gqa_attention(p, px, p, tq=_row_tile(seq, 256), mode='band', **a_cfg)
            o2 = diff_attention(p, px, p, lam_vecs, b_subln[e], lam_init=_lambda_init(l),
                                tq=_row_tile(seq, 256), has_lat=True)
            if need_ctx:
                ox1 = gqa_attention(px, px, None, tq=_row_tile(n_ctx, 256), mode='none', **a_cfg)
                ox2 = diff_attention(px, px, None, lam_vecs, b_subln[e], lam_init=_lambda_init(l),
                                     tq=_row_tile(n_ctx, 256), has_lat=False)
        else:
            kc = C_HEADS * HEAD_DIM
            vc = kc + C_KV_HEADS * HEAD_DIM
            c_cfg = dict(kv_heads=C_KV_HEADS, rep=C_HEADS // C_KV_HEADS, q_col=0, k_col=kc, v_col=vc)
            o1 = gqa_attention(p, px, p, tq=_row_tile(seq, 256), mode='full', **c_cfg)
            o2 = natten_attention(p, px, natten_bias_slabs(d_rpb[e]))
            if need_ctx:
                qd = vc + C_KV_HEADS * HEAD_DIM
                kd = qd + D_HEADS * HEAD_DIM
                vd = kd + D_HEADS * HEAD_DIM
                ox1 = gqa_attention(px, px, None, tq=_row_tile(n_ctx, 256), mode='none', **c_cfg)
                ox2 = gqa_attention(px, px, None, kv_heads=D_HEADS, rep=1, q_col=qd, k_col=kd, v_col=vd,
                                    tq=_row_tile(n_ctx, 256), mode='none')
        w_router_pad = jnp.zeros((d, ROUTER_LANES), BF16).at[:, :n_experts].set(w_router[l].astype(BF16))
        wg, wu, wd = w_gate[l].astype(BF16), w_up[l].astype(BF16), w_down[l].astype(BF16)
        x1, h, aff = out_projection(o1, o2, w_out_l, x, gt1, g_ffn[l], sc2, sh2, w_router_pad, n_experts,
                                    _row_tile(seq, 256))
        x = expert_choice_ffn(h, aff, x1, gt2, wg, wu, wd, n_experts)
        if need_ctx:
            xc1, hx, affx = out_projection(ox1, ox2, w_out_l, xc, gt1x, g_ffn[l], sc2x, sh2x, w_router_pad,
                                           n_experts, _row_tile(n_ctx, 256))
            xc = expert_choice_ffn(hx, affx, xc1, gt2x, wg, wu, wd, n_experts)
    return final_norm(x, g_final, _row_tile(seq, 512))
```

```python
import functools
import math

import jax
import jax.numpy as jnp
from jax import lax
from jax.experimental import pallas as pl
from jax.experimental.pallas import tpu as pltpu

F32 = jnp.float32
BF16 = jnp.bfloat16

HEAD_DIM = 128
GRID_W = 64
ROPE_THETA = 10000.0
EPS = 1e-6
NEG_INF = -1e30
A_HEADS, A_KV_HEADS, A_WINDOW = 8, 2, 128
B_HEADS = 4
C_HEADS, C_KV_HEADS = 8, 2
D_HEADS, NA_ROWS, NA_COLS = 8, 8, 16
N_EXPERTS = 16
EC_CAPACITY_FACTOR = 2
IN_WIDTH = 4608
MIX_WIDTH = 2048
ROUTER_LANES = 128

ROPE_BIT, QNORM_BIT, KNORM_BIT, SCALE_BIT = 1, 2, 4, 8
IN_TN = 256

V7X_VMEM_LIMIT = 56 * 1024 * 1024
V7X_VMEM_LIMIT_LARGE = 60 * 1024 * 1024


def _params(sem, vmem=V7X_VMEM_LIMIT):
    return pltpu.CompilerParams(dimension_semantics=sem, vmem_limit_bytes=vmem)


def _nt_dot(a, b):
    return lax.dot_general(a, b, (((1,), (1,)), ((), ())), preferred_element_type=F32)


def _ada_kernel(c_ref, w_ref, b_ref, o_ref):
    c = c_ref[...]
    cond = c * (1.0 / (1.0 + jnp.exp(-c)))
    o_ref[0] = jnp.dot(cond.astype(BF16), w_ref[0].astype(BF16), preferred_element_type=F32) + b_ref[0]


def ada_modulation(c_all, w_ada, b_ada):
    depth, d, n = w_ada.shape
    r = c_all.shape[0]
    tn = math.gcd(n, 1024)
    return pl.pallas_call(
        _ada_kernel,
        grid=(depth, n // tn),
        in_specs=[pl.BlockSpec((r, d), lambda l, j: (0, 0)),
                  pl.BlockSpec((1, d, tn), lambda l, j: (l, 0, j)),
                  pl.BlockSpec((1, 1, tn), lambda l, j: (l, 0, j))],
        out_specs=pl.BlockSpec((1, r, tn), lambda l, j: (l, 0, j)),
        out_shape=jax.ShapeDtypeStruct((depth, r, n), F32),
        compiler_params=_params(("arbitrary", "arbitrary")),
        name="ada_modulation",
    )(c_all, w_ada, b_ada.reshape(depth, 1, n))


def _rope_rotate(y, sin_signed):
    lane = lax.broadcasted_iota(jnp.int32, y.shape, 1)
    first = (lane & 63) < 32
    return jnp.where(first, pltpu.roll(y, 96, 1), pltpu.roll(y, 32, 1)) * sin_signed


def _in_proj_kernel(x_ref, g_ref, sc_ref, sh_ref, w_ref, cos_ref, sin_ref, qn_ref, kn_ref, o_ref, xn_ref,
                    *, codes, scale):
    x = x_ref[0]
    ms = jnp.mean(x * x, axis=-1, keepdims=True)
    y = x * lax.rsqrt(ms + EPS) * g_ref[...]
    xn_ref[...] = (y * (1.0 + sc_ref[0]) + sh_ref[0]).astype(BF16)
    for j, code in enumerate(codes):
        acc = jnp.dot(xn_ref[...], w_ref[:, j * IN_TN:(j + 1) * IN_TN], preferred_element_type=F32)
        for h in range(IN_TN // HEAD_DIM):
            y = acc[:, h * HEAD_DIM:(h + 1) * HEAD_DIM]
            if code & (QNORM_BIT | KNORM_BIT):
                gn_ref = qn_ref if code & QNORM_BIT else kn_ref
                ms = jnp.mean(y * y, axis=-1, keepdims=True)
                y = y * lax.rsqrt(ms + EPS) * gn_ref[...]
            if code & ROPE_BIT:
                y = y * cos_ref[...] + _rope_rotate(y, sin_ref[...])
            if code & SCALE_BIT:
                y = y * scale
            lo = j * IN_TN + h * HEAD_DIM
            o_ref[0, :, lo:lo + HEAD_DIM] = y.astype(BF16)


def in_projection(x, g, scale1, shift1, w_bf16, codes, cos, sin_signed, qn, kn, tm):
    b, n, d = x.shape
    width = w_bf16.shape[1]
    assert len(codes) * IN_TN == width
    per_batch = scale1.shape[0] > 1
    mod_map = (lambda bi, i: (bi, 0, 0)) if per_batch else (lambda bi, i: (0, 0, 0))
    const = lambda bi, i: (0, 0)
    return pl.pallas_call(
        functools.partial(_in_proj_kernel, codes=tuple(codes), scale=HEAD_DIM ** -0.5),
        grid=(b, n // tm),
        in_specs=[pl.BlockSpec((1, tm, d), lambda bi, i: (bi, i, 0)),
                  pl.BlockSpec((1, d), const),
                  pl.BlockSpec((1, 1, d), mod_map),
                  pl.BlockSpec((1, 1, d), mod_map),
                  pl.BlockSpec((d, width), const, pipeline_mode=pl.Buffered(1)),
                  pl.BlockSpec((tm, HEAD_DIM), lambda bi, i: (i, 0)),
                  pl.BlockSpec((tm, HEAD_DIM), lambda bi, i: (i, 0)),
                  pl.BlockSpec((1, HEAD_DIM), const),
                  pl.BlockSpec((1, HEAD_DIM), const)],
        out_specs=pl.BlockSpec((1, tm, width), lambda bi, i: (bi, i, 0)),
        out_shape=jax.ShapeDtypeStruct((b, n, width), BF16),
        scratch_shapes=[pltpu.VMEM((tm, d), BF16)],
        compiler_params=_params(("arbitrary", "arbitrary")),
        name="in_projection",
    )(x, g.reshape(1, d), scale1, shift1, w_bf16, cos, sin_signed, qn, kn)


KEY_CHUNK = 256


class _SoftmaxChain:
    def __init__(self, q, chunks, s_ref, dv, sink=None):
        self.q, self.chunks, self.s_ref, self.sink = q, chunks, s_ref, sink
        m = q.shape[0]
        self.mx = jnp.full((m, HEAD_DIM), NEG_INF, F32)
        self.ls = jnp.zeros((m, HEAD_DIM), F32)
        self.acc = jnp.zeros((m, dv), F32)

    def score_steps(self):
        return [functools.partial(self._score, j) for j in range(len(self.chunks))]

    def value_steps(self):
        return [functools.partial(self._value, j) for j in range(len(self.chunks))]

    def _score(self, j):
        k_fn, _, bias_fn = self.chunks[j]
        s = _nt_dot(self.q, k_fn())
        if bias_fn is not None:
            s = s + bias_fn()
        self.s_ref[:, j * KEY_CHUNK:(j + 1) * KEY_CHUNK] = s
        for t in range(KEY_CHUNK // HEAD_DIM):
            self.mx = jnp.maximum(self.mx, s[:, t * HEAD_DIM:(t + 1) * HEAD_DIM])

    def finish_scores(self):
        m = jnp.max(self.mx, axis=-1, keepdims=True)
        if self.sink is not None:
            m = jnp.maximum(m, self.sink)
        self.m = m
        self.m_wide = jnp.broadcast_to(m, self.mx.shape)

    def _value(self, j):
        _, v_fn, _ = self.chunks[j]
        parts = []
        for t in range(KEY_CHUNK // HEAD_DIM):
            lo = j * KEY_CHUNK + t * HEAD_DIM
            e = jnp.exp(self.s_ref[:, lo:lo + HEAD_DIM] - self.m_wide)
            self.ls = self.ls + e
            parts.append(e.astype(BF16))
        p = jnp.concatenate(parts, axis=1)
        self.acc = self.acc + jnp.dot(p, v_fn(), preferred_element_type=F32)

    def finish(self):
        den = jnp.sum(self.ls, axis=-1, keepdims=True)
        if self.sink is not None:
            den = den + jnp.exp(self.sink - self.m)
        return self.acc, den


def _run_chains(chains):
    prev = None
    for chain in chains:
        scores = chain.score_steps()
        values = prev.value_steps() if prev is not None else []
        for i in range(max(len(scores), len(values))):
            if i < len(values):
                values[i]()
            if i < len(scores):
                scores[i]()
        chain.finish_scores()
        prev = chain
    for step in prev.value_steps():
        step()


def _ref_chunks(k_ref, v_ref, start, length, bias=None, k_cols=slice(None), v_cols=slice(None)):
    assert length % KEY_CHUNK == 0
    out = []
    for j in range(length // KEY_CHUNK):
        if isinstance(start, int):
            rows = slice(start + j * KEY_CHUNK, start + (j + 1) * KEY_CHUNK)
        else:
            rows = pl.ds(start + j * KEY_CHUNK, KEY_CHUNK)
        out.append((lambda rows=rows: k_ref[0, rows, k_cols], lambda rows=rows: v_ref[0, rows, v_cols],
                    (lambda j=j: bias(j)) if bias is not None else None))
    return out


def _gqa_kernel(*refs, rep, tq, mode, seq, use_sink):
    if use_sink:
        sink_ref, refs = refs[0], refs[1:]
    s_ref = refs[-1]
    refs = refs[:-1]
    if mode == 'none':
        q_ref, kx_ref, vx_ref, o_ref = refs
    else:
        q_ref, kx_ref, vx_ref, kl_ref, vl_ref, o_ref = refs
    g = pl.program_id(1)
    qi = pl.program_id(2)
    chunks = _ref_chunks(kx_ref, vx_ref, 0, kx_ref.shape[1])
    if mode == 'full':
        chunks += _ref_chunks(kl_ref, vl_ref, 0, seq)
    elif mode == 'band':
        band = tq + 2 * A_WINDOW
        start = pl.multiple_of(jnp.clip(qi * tq - A_WINDOW, 0, seq - band), HEAD_DIM)
        rel = (lax.broadcasted_iota(jnp.int32, (tq, KEY_CHUNK), 0)
               - lax.broadcasted_iota(jnp.int32, (tq, KEY_CHUNK), 1))

        def band_bias(j):
            dist = rel + (qi * tq - start - j * KEY_CHUNK)
            return jnp.where(jnp.abs(dist) <= A_WINDOW, 0.0, NEG_INF)

        chunks += _ref_chunks(kl_ref, vl_ref, start, band, band_bias)
    q = q_ref[0]
    chains = [_SoftmaxChain(q[:, r * HEAD_DIM:(r + 1) * HEAD_DIM], chunks, s_ref.at[r], HEAD_DIM,
                            sink=sink_ref[g * rep + r] if use_sink else None) for r in range(rep)]
    _run_chains(chains)
    for r, chain in enumerate(chains):
        acc, den = chain.finish()
        o_ref[0, :, r * HEAD_DIM:(r + 1) * HEAD_DIM] = (acc / den).astype(BF16)


def gqa_attention(p_q, p_ctx, p_lat, *, kv_heads, rep, q_col, k_col, v_col, tq, mode, sink=None):
    b, nq, _ = p_q.shape
    c = p_ctx.shape[1]
    qw = rep * HEAD_DIM
    use_sink = sink is not None
    assert tq & (tq - 1) == 0
    in_specs = [pl.BlockSpec((1, tq, qw), lambda bi, g, qi, *_: (bi, qi, q_col // qw + g)),
                pl.BlockSpec((1, c, HEAD_DIM), lambda bi, g, qi, *_: (bi, 0, k_col // HEAD_DIM + g)),
                pl.BlockSpec((1, c, HEAD_DIM), lambda bi, g, qi, *_: (bi, 0, v_col // HEAD_DIM + g))]
    args = [p_q, p_ctx, p_ctx]
    seq = 0
    if mode != 'none':
        seq = p_lat.shape[1]
        in_specs += [pl.BlockSpec((1, seq, HEAD_DIM), lambda bi, g, qi, *_: (bi, 0, k_col // HEAD_DIM + g)),
                     pl.BlockSpec((1, seq, HEAD_DIM), lambda bi, g, qi, *_: (bi, 0, v_col // HEAD_DIM + g))]
        args += [p_lat, p_lat]
    if use_sink:
        args = [sink] + args
    n_keys = c + {'none': 0, 'full': seq, 'band': tq + 2 * A_WINDOW}[mode]
    grid_spec = pltpu.PrefetchScalarGridSpec(
        num_scalar_prefetch=1 if use_sink else 0,
        grid=(b, kv_heads, nq // tq),
        in_specs=in_specs,
        out_specs=pl.BlockSpec((1, tq, qw), lambda bi, g, qi, *_: (bi, qi, g)),
        scratch_shapes=[pltpu.VMEM((rep, tq, n_keys), F32)])
    return pl.pallas_call(
        functools.partial(_gqa_kernel, rep=rep, tq=tq, mode=mode, seq=seq, use_sink=use_sink),
        grid_spec=grid_spec,
        out_shape=jax.ShapeDtypeStruct((b, nq, kv_heads * qw), BF16),
        compiler_params=_params(("arbitrary", "arbitrary", "arbitrary")),
        name=f"gqa_attention_{mode}",
    )(*args)


def _diff_kernel(*refs, has_lat, lam_init):
    if has_lat:
        lq1, lk1, lq2, lk2, sub_ref, q_ref, kx_ref, vx_ref, kl_ref, vl_ref, o_ref, s_ref = refs
    else:
        lq1, lk1, lq2, lk2, sub_ref, q_ref, kx_ref, vx_ref, o_ref, s_ref = refs
    lam = (jnp.exp(jnp.sum(lq1[...] * lk1[...], axis=-1, keepdims=True))
           - jnp.exp(jnp.sum(lq2[...] * lk2[...], axis=-1, keepdims=True)) + lam_init)
    n_ctx = vx_ref.shape[1]
    dv = 2 * HEAD_DIM
    chains = []
    for h in range(DIFF_HEADS_PER_STEP):
        v_cols = slice(h * dv, (h + 1) * dv)
        for m in range(2):
            k_cols = slice((2 * h + m) * HEAD_DIM, (2 * h + m + 1) * HEAD_DIM)
            chunks = _ref_chunks(kx_ref, vx_ref, 0, n_ctx, k_cols=k_cols, v_cols=v_cols)
            if has_lat:
                chunks += _ref_chunks(kl_ref, vl_ref, 0, vl_ref.shape[1], k_cols=k_cols, v_cols=v_cols)
            chains.append(_SoftmaxChain(q_ref[0, :, k_cols], chunks, s_ref.at[2 * h + m], dv))
    _run_chains(chains)
    for h in range(DIFF_HEADS_PER_STEP):
        acc1, den1 = chains[2 * h].finish()
        acc2, den2 = chains[2 * h + 1].finish()
        o = acc1 / den1 - lam * (acc2 / den2)
        ms = jnp.mean(o * o, axis=-1, keepdims=True)
        o_ref[0, :, h * dv:(h + 1) * dv] = (o * lax.rsqrt(ms + EPS) * sub_ref[...] * (1.0 - lam_init)).astype(BF16)


DIFF_HEADS_PER_STEP = 2


def diff_attention(p_q, p_ctx, p_lat, lam_vecs, subln, *, lam_init, tq, has_lat):
    b, nq, _ = p_q.shape
    c = p_ctx.shape[1]
    qb = (A_HEADS + 2 * A_KV_HEADS) * HEAD_DIM
    kb = qb + B_HEADS * 2 * HEAD_DIM
    vb = kb + B_HEADS * 2 * HEAD_DIM
    dv = 2 * HEAD_DIM
    wide = DIFF_HEADS_PER_STEP * dv
    assert qb % wide == 0 and kb % wide == 0 and vb % wide == 0 and B_HEADS % DIFF_HEADS_PER_STEP == 0
    vec = pl.BlockSpec((1, HEAD_DIM), lambda bi, h, qi: (0, 0))

    def cols(n, base):
        return pl.BlockSpec((1, n, wide), lambda bi, h, qi: (bi, 0, base // wide + h))

    in_specs = [vec, vec, vec, vec, pl.BlockSpec((1, dv), lambda bi, h, qi: (0, 0)),
                pl.BlockSpec((1, tq, wide), lambda bi, h, qi: (bi, qi, qb // wide + h)),
                cols(c, kb), cols(c, vb)]
    args = [v.reshape(1, HEAD_DIM) for v in lam_vecs] + [subln.reshape(1, dv), p_q, p_ctx, p_ctx]
    seq = 0
    if has_lat:
        seq = p_lat.shape[1]
        in_specs += [cols(seq, kb), cols(seq, vb)]
        args += [p_lat, p_lat]
    return pl.pallas_call(
        functools.partial(_diff_kernel, has_lat=has_lat, lam_init=lam_init),
        grid=(b, B_HEADS // DIFF_HEADS_PER_STEP, nq // tq),
        in_specs=in_specs,
        out_specs=pl.BlockSpec((1, tq, wide), lambda bi, h, qi: (bi, qi, h)),
        out_shape=jax.ShapeDtypeStruct((b, nq, B_HEADS * dv), BF16),
        scratch_shapes=[pltpu.VMEM((2 * DIFF_HEADS_PER_STEP, tq, c + seq), F32)],
        compiler_params=_params(("arbitrary", "arbitrary", "arbitrary")),
        name="diff_attention",
    )(*args)


NA_QROWS = 4
NA_WROWS = NA_QROWS + NA_ROWS
NA_BLOCKS_PER_STEP = 2


def _natten_kernel(q_ref, kx_ref, vx_ref, kl_ref, vl_ref, tb_ref, o_ref, s_ref, *, n_rows):
    n_blocks = n_rows // NA_QROWS
    tq = NA_QROWS * GRID_W
    n_ctx = kx_ref.shape[1]
    chains = []
    for i in range(NA_BLOCKS_PER_STEP):
        blk = pl.program_id(2) * NA_BLOCKS_PER_STEP + i
        first_row = jnp.clip(blk * NA_QROWS - NA_ROWS // 2, 0, n_rows - NA_WROWS)
        start = pl.multiple_of(first_row * GRID_W, GRID_W)
        variant = jnp.where(blk == 0, 0, jnp.where(blk == n_blocks - 1, 2, 1))

        def bias(j, variant=variant):
            return tb_ref[0, variant, :, j * KEY_CHUNK:(j + 1) * KEY_CHUNK]

        chunks = (_ref_chunks(kx_ref, vx_ref, 0, n_ctx)
                  + _ref_chunks(kl_ref, vl_ref, start, NA_WROWS * GRID_W, bias))
        chains.append(_SoftmaxChain(q_ref[0, i * tq:(i + 1) * tq, :], chunks, s_ref.at[i], HEAD_DIM))
    _run_chains(chains)
    for i, chain in enumerate(chains):
        acc, den = chain.finish()
        o_ref[0, i * tq:(i + 1) * tq, :] = (acc / den).astype(BF16)


def natten_bias_slabs(rpb):
    col = jnp.arange(GRID_W)
    cs = jnp.clip(col - NA_COLS // 2, 0, GRID_W - NA_COLS)
    col_mask = (col[None, :] >= cs[:, None]) & (col[None, :] < cs[:, None] + NA_COLS)
    col_idx = jnp.clip(col[None, :] - col[:, None], -(NA_COLS - 1), NA_COLS - 1) + NA_COLS - 1
    rpb_c = jnp.where(col_mask[None, None], rpb[:, :, col_idx], NEG_INF)
    i = jnp.arange(NA_QROWS)[:, None]
    a = jnp.arange(NA_WROWS)[None, :]
    half = NA_ROWS // 2
    kinds = [(a < NA_ROWS, a - i + NA_ROWS - 1),
             ((a >= i) & (a < i + NA_ROWS), a - i - half + NA_ROWS - 1),
             (a >= NA_WROWS - NA_ROWS, a - i - NA_ROWS + NA_ROWS - 1)]
    slabs = []
    for valid, roff in kinds:
        t = rpb_c[:, jnp.clip(roff, 0, 2 * NA_ROWS - 2)]
        t = jnp.where(valid[None, :, :, None, None], t, NEG_INF)
        slabs.append(jnp.transpose(t, (0, 1, 3, 2, 4)).reshape(rpb.shape[0], NA_QROWS * GRID_W, NA_WROWS * GRID_W))
    return jnp.stack(slabs, axis=1).astype(F32)


def natten_attention(p_lat, p_ctx, slabs):
    b, seq, _ = p_lat.shape
    c = p_ctx.shape[1]
    n_rows = seq // GRID_W
    rows_per_step = NA_QROWS * NA_BLOCKS_PER_STEP
    assert n_rows >= NA_WROWS and n_rows % rows_per_step == 0
    qd = (C_HEADS + 2 * C_KV_HEADS) * HEAD_DIM
    kd = qd + D_HEADS * HEAD_DIM
    vd = kd + D_HEADS * HEAD_DIM
    tq = rows_per_step * GRID_W

    def col(n, base):
        return pl.BlockSpec((1, n, HEAD_DIM), lambda bi, h, i: (bi, 0, base // HEAD_DIM + h))

    in_specs = [pl.BlockSpec((1, tq, HEAD_DIM), lambda bi, h, i: (bi, i, qd // HEAD_DIM + h)),
                col(c, kd), col(c, vd), col(seq, kd), col(seq, vd),
                pl.BlockSpec((1, 3, NA_QROWS * GRID_W, NA_WROWS * GRID_W), lambda bi, h, i: (h, 0, 0, 0))]
    return pl.pallas_call(
        functools.partial(_natten_kernel, n_rows=n_rows),
        grid=(b, D_HEADS, n_rows // rows_per_step),
        in_specs=in_specs,
        out_specs=pl.BlockSpec((1, tq, HEAD_DIM), lambda bi, h, i: (bi, i, h)),
        out_shape=jax.ShapeDtypeStruct((b, seq, D_HEADS * HEAD_DIM), BF16),
        scratch_shapes=[pltpu.VMEM((NA_BLOCKS_PER_STEP, NA_QROWS * GRID_W, c + NA_WROWS * GRID_W), F32)],
        compiler_params=_params(("arbitrary", "arbitrary", "arbitrary")),
        name="natten_attention",
    )(p_lat, p_ctx, p_ctx, p_lat, p_lat, slabs)


def _out_proj_kernel(o1_ref, o2_ref, w_ref, x_ref, gt_ref, g_ref, sc_ref, sh_ref, wr_ref, xo_ref, h_ref, aff_ref,
                     *, n_experts):
    y = (jnp.dot(o1_ref[0], w_ref[0], preferred_element_type=F32)
         + jnp.dot(o2_ref[0], w_ref[1], preferred_element_type=F32))
    x = x_ref[0] + gt_ref[0] * y
    xo_ref[0] = x
    ms = jnp.mean(x * x, axis=-1, keepdims=True)
    h = x * lax.rsqrt(ms + EPS) * g_ref[...]
    h = h * (1.0 + sc_ref[0]) + sh_ref[0]
    h_ref[0] = h
    logits = jnp.dot(h.astype(BF16), wr_ref[...], preferred_element_type=F32)
    lane = lax.broadcasted_iota(jnp.int32, logits.shape, 1)
    logits = jnp.where(lane < n_experts, logits, NEG_INF)
    e = jnp.exp(logits - jnp.max(logits, axis=-1, keepdims=True))
    aff_ref[0] = e / jnp.sum(e, axis=-1, keepdims=True)


def out_projection(o1, o2, w_bf16, x, gate1, g_ffn, scale2, shift2, w_router_pad, n_experts, tm):
    b, n, d = x.shape
    k = o1.shape[2]
    assert o2.shape[2] == k and w_bf16.shape == (2, k, d)
    per_batch = gate1.shape[0] > 1
    mod_map = (lambda bi, i: (bi, 0, 0)) if per_batch else (lambda bi, i: (0, 0, 0))
    row = lambda bi, i: (bi, i, 0)
    return pl.pallas_call(
        functools.partial(_out_proj_kernel, n_experts=n_experts),
        grid=(b, n // tm),
        in_specs=[pl.BlockSpec((1, tm, k), row),
                  pl.BlockSpec((1, tm, k), row),
                  pl.BlockSpec((2, k, d), lambda bi, i: (0, 0, 0)),
                  pl.BlockSpec((1, tm, d), row),
                  pl.BlockSpec((1, 1, d), mod_map),
                  pl.BlockSpec((1, d), lambda bi, i: (0, 0)),
                  pl.BlockSpec((1, 1, d), mod_map),
                  pl.BlockSpec((1, 1, d), mod_map),
                  pl.BlockSpec((d, ROUTER_LANES), lambda bi, i: (0, 0))],
        out_specs=[pl.BlockSpec((1, tm, d), row), pl.BlockSpec((1, tm, d), row),
                   pl.BlockSpec((1, tm, ROUTER_LANES), row)],
        out_shape=[jax.ShapeDtypeStruct((b, n, d), F32), jax.ShapeDtypeStruct((b, n, d), F32),
                   jax.ShapeDtypeStruct((b, n, ROUTER_LANES), F32)],
        compiler_params=_params(("arbitrary", "arbitrary")),
        name="out_projection",
    )(o1, o2, w_bf16, x, gate1, g_ffn.reshape(1, d), scale2, shift2, w_router_pad)


MOE_ROW_CHUNK = 1024


def _gather_kernel(idx_ref, *refs, cap, dc):
    h_refs, o_ref, stage_ref = refs[:-2], refs[-2], refs[-1]

    def body(c, carry):
        t = idx_ref[0, 0, 0, c]
        for k, h_ref in enumerate(h_refs):
            stage_ref[pl.ds(c, 1), k * dc:(k + 1) * dc] = h_ref[0, pl.ds(t, 1), :]
        return carry
    lax.fori_loop(0, cap, body, 0, unroll=8)
    o_ref[0, 0] = stage_ref[...].astype(BF16)


def moe_gather(h, idx, dc):
    b, n, d = h.shape
    _, e, cap = idx.shape
    n_chunks = d // dc
    h_specs = [pl.BlockSpec((1, n, dc), lambda bi, ei, k=k: (bi, 0, k), pipeline_mode=pl.Buffered(1))
               for k in range(n_chunks)]
    return pl.pallas_call(
        functools.partial(_gather_kernel, cap=cap, dc=dc),
        grid=(b, e),
        in_specs=[pl.BlockSpec((1, 1, 1, cap), lambda bi, ei: (bi, ei, 0, 0), memory_space=pltpu.SMEM)] + h_specs,
        out_specs=pl.BlockSpec((1, 1, cap, d), lambda bi, ei: (bi, ei, 0, 0)),
        out_shape=jax.ShapeDtypeStruct((b, e, cap, d), BF16),
        scratch_shapes=[pltpu.VMEM((cap, d), F32)],
        compiler_params=_params(("arbitrary", "arbitrary")),
        name="moe_gather",
    )(idx.reshape(b, e, 1, cap), *([h] * n_chunks))


def _expert_kernel(xe_ref, wg_ref, wu_ref, wd_ref, gate_ref, o_ref):
    bs, _, cap, d = xe_ref.shape
    xe = xe_ref[...].reshape(bs * cap, d)
    a = jnp.dot(xe, wg_ref[0], preferred_element_type=F32)
    u = jnp.dot(xe, wu_ref[0], preferred_element_type=F32)
    hid = (a * (1.0 / (1.0 + jnp.exp(-a))) * u).astype(BF16)
    ye = jnp.dot(hid, wd_ref[0], preferred_element_type=F32) * gate_ref[...].reshape(bs * cap, 1)
    o_ref[...] = ye.reshape(bs, 1, cap, d)


def moe_experts(xe, wg, wu, wd, gate):
    b, e, cap, d = xe.shape
    f = wg.shape[2]
    single = pl.Buffered(1)
    bs = max(1, min(b, 512 // cap))
    assert b % bs == 0
    return pl.pallas_call(
        _expert_kernel,
        grid=(e, b // bs),
        in_specs=[pl.BlockSpec((bs, 1, cap, d), lambda ei, bi: (bi, ei, 0, 0)),
                  pl.BlockSpec((1, d, f), lambda ei, bi: (ei, 0, 0), pipeline_mode=single),
                  pl.BlockSpec((1, d, f), lambda ei, bi: (ei, 0, 0), pipeline_mode=single),
                  pl.BlockSpec((1, f, d), lambda ei, bi: (ei, 0, 0), pipeline_mode=single),
                  pl.BlockSpec((bs, 1, cap, 1), lambda ei, bi: (bi, ei, 0, 0))],
        out_specs=pl.BlockSpec((bs, 1, cap, d), lambda ei, bi: (bi, ei, 0, 0)),
        out_shape=jax.ShapeDtypeStruct((b, e, cap, d), F32),
        compiler_params=_params(("arbitrary", "arbitrary")),
        name="moe_experts",
    )(xe, wg, wu, wd, gate)


def _combine_kernel(idx_ref, ye_ref, x_ref, gt_ref, o_ref, acc_ref, *, cap, n_experts, rows_out):
    step = pl.program_id(1)

    @pl.when(step == 0)
    def _():
        acc_ref[...] = jnp.zeros_like(acc_ref)

    @pl.when(step < n_experts)
    def _():
        def body(c, carry):
            t = idx_ref[0, 0, 0, c]
            acc_ref[pl.ds(t, 1), :] = acc_ref[pl.ds(t, 1), :] + ye_ref[0, 0, pl.ds(c, 1), :]
            return carry
        lax.fori_loop(0, cap, body, 0, unroll=8)

    @pl.when(step >= n_experts)
    def _():
        r0 = pl.multiple_of((step - n_experts) * rows_out, rows_out)
        o_ref[0] = x_ref[0] + gt_ref[0] * acc_ref[pl.ds(r0, rows_out), :]


def moe_combine(ye, idx, x, gate2):
    b, n, d = x.shape
    _, e, cap = idx.shape
    rows_out = min(n, 256)
    per_batch = gate2.shape[0] > 1
    mod_map = (lambda bi, s: (bi, 0, 0)) if per_batch else (lambda bi, s: (0, 0, 0))
    expert = lambda bi, s: (bi, jnp.minimum(s, e - 1), 0, 0)
    tile = lambda bi, s: (bi, jnp.maximum(s - e, 0), 0)
    return pl.pallas_call(
        functools.partial(_combine_kernel, cap=cap, n_experts=e, rows_out=rows_out),
        grid=(b, e + n // rows_out),
        in_specs=[pl.BlockSpec((1, 1, 1, cap), expert, memory_space=pltpu.SMEM),
                  pl.BlockSpec((1, 1, cap, d), expert),
                  pl.BlockSpec((1, rows_out, d), tile),
                  pl.BlockSpec((1, 1, d), mod_map)],
        out_specs=pl.BlockSpec((1, rows_out, d), tile),
        out_shape=jax.ShapeDtypeStruct((b, n, d), F32),
        scratch_shapes=[pltpu.VMEM((n, d), F32)],
        compiler_params=_params(("arbitrary", "arbitrary")),
        name="moe_combine",
    )(idx.reshape(b, e, 1, cap), ye, x, gate2)


def expert_choice_ffn(h, aff, x, gate2, wg, wu, wd, n_experts):
    b, n, d = h.shape
    cap = max(1, EC_CAPACITY_FACTOR * n // n_experts)
    gate, idx = lax.top_k(jnp.swapaxes(aff[..., :n_experts], 1, 2), cap)
    dc = min(d, MOE_ROW_CHUNK)
    xe = moe_gather(h, idx, dc)
    ye = moe_experts(xe, wg, wu, wd, gate[..., None])
    return moe_combine(ye, idx, x, gate2)


def _final_norm_kernel(x_ref, g_ref, o_ref):
    x = x_ref[0]
    ms = jnp.mean(x * x, axis=-1, keepdims=True)
    o_ref[0] = x * lax.rsqrt(ms + EPS) * g_ref[...]


def final_norm(x, g, tm):
    b, n, d = x.shape
    return pl.pallas_call(
        _final_norm_kernel,
        grid=(b, n // tm),
        in_specs=[pl.BlockSpec((1, tm, d), lambda bi, i: (bi, i, 0)), pl.BlockSpec((1, d), lambda bi, i: (0, 0))],
        out_specs=pl.BlockSpec((1, tm, d), lambda bi, i: (bi, i, 0)),
        out_shape=jax.ShapeDtypeStruct((b, n, d), F32),
        compiler_params=_params(("arbitrary", "arbitrary")),
        name="final_norm",
    )(x, g.reshape(1, d))


def _rope_tables(n):
    t = jnp.arange(n, dtype=jnp.int32)
    row = (t // GRID_W).astype(F32)
    col = (t % GRID_W).astype(F32)
    n_freq = HEAD_DIM // 4
    inv = ROPE_THETA ** (-jnp.arange(n_freq, dtype=F32) / n_freq)
    ar = row[:, None] * inv
    ac = col[:, None] * inv
    ang = jnp.concatenate([ar, ar, ac, ac], axis=-1)
    sign = jnp.where((jnp.arange(HEAD_DIM) % (2 * n_freq)) < n_freq, -1.0, 1.0).astype(F32)
    return jnp.cos(ang), jnp.sin(ang) * sign


def _tile_codes(layer, rope):
    if layer % 2 == 0:
        groups = [(A_HEADS * HEAD_DIM, ROPE_BIT | SCALE_BIT), (A_KV_HEADS * HEAD_DIM, ROPE_BIT),
                  (A_KV_HEADS * HEAD_DIM, 0), (B_HEADS * 2 * HEAD_DIM, ROPE_BIT | SCALE_BIT),
                  (B_HEADS * 2 * HEAD_DIM, ROPE_BIT), (B_HEADS * 2 * HEAD_DIM, 0)]
    else:
        groups = [(C_HEADS * HEAD_DIM, QNORM_BIT | ROPE_BIT | SCALE_BIT), (C_KV_HEADS * HEAD_DIM, KNORM_BIT | ROPE_BIT),
                  (C_KV_HEADS * HEAD_DIM, 0), (D_HEADS * HEAD_DIM, SCALE_BIT), (D_HEADS * HEAD_DIM, 0),
                  (D_HEADS * HEAD_DIM, 0)]
    codes = []
    for width, code in groups:
        assert width % IN_TN == 0
        codes += [code if rope else code & ~ROPE_BIT] * (width // IN_TN)
    return codes


def _lambda_init(layer):
    return 0.8 - 0.6 * math.exp(-0.3 * layer)


def _row_tile(n, target):
    t = min(n, target)
    assert n % t == 0
    return t


def kernel(x, c, ctx, c_ctx, w_ada, b_ada, g_mix, g_ffn, w_in, w_out, a_sink, b_lam_q1, b_lam_k1, b_lam_q2,
           b_lam_k2, b_subln, c_q_norm, c_k_norm, d_rpb, w_router, w_gate, w_up, w_down, g_final):
    bsz, seq, d = x.shape
    n_ctx = ctx.shape[1]
    depth = w_ada.shape[0]
    n_experts = w_router.shape[2]
    cos, sin_signed = _rope_tables(seq)
    ones_head = jnp.ones((1, HEAD_DIM), F32)

    n_rows = -(-(bsz + 1) // 8) * 8
    c_all = jnp.zeros((n_rows, d), F32).at[:bsz].set(c).at[bsz].set(c_ctx)
    mod = ada_modulation(c_all, w_ada, b_ada)

    xc = ctx
    for l in range(depth):
        need_ctx = l < depth - 1
        e = l // 2
        m_lat = mod[l, :bsz].reshape(bsz, 1, 6, d)
        m_ctx = mod[l, bsz:bsz + 1].reshape(1, 1, 6, d)
        sh1, sc1, gt1, sh2, sc2, gt2 = (m_lat[:, :, i] for i in range(6))
        sh1x, sc1x, gt1x, sh2x, sc2x, gt2x = (m_ctx[:, :, i] for i in range(6))
        w_in_l = w_in[l].astype(BF16)
        w_out_l = w_out[l].astype(BF16).reshape(2, MIX_WIDTH // 2, d)
        if l % 2 == 0:
            qn = kn = ones_head
        else:
            qn, kn = c_q_norm[e].reshape(1, HEAD_DIM), c_k_norm[e].reshape(1, HEAD_DIM)
        tm_lat = _row_tile(seq, 512)
        tm_ctx = _row_tile(n_ctx, 512)
        p = in_projection(x, g_mix[l], sc1, sh1, w_in_l, _tile_codes(l, True), cos, sin_signed, qn, kn, tm_lat)
        px = in_projection(xc, g_mix[l], sc1x, sh1x, w_in_l, _tile_codes(l, False), cos[:tm_ctx],
                           sin_signed[:tm_ctx], qn, kn, tm_ctx)
        ox1 = ox2 = None
        if l % 2 == 0:
            lam_vecs = (b_lam_q1[e], b_lam_k1[e], b_lam_q2[e], b_lam_k2[e])
            ka = A_HEADS * HEAD_DIM
            va = ka + A_KV_HEADS * HEAD_DIM
            a_cfg = dict(kv_heads=A_KV_HEADS, rep=A_HEADS // A_KV_HEADS, q_col=0, k_col=ka, v_col=va,
                         sink=a_sink[e])
            o1 = gqa_attention(p, px, p, tq=_row_tile(seq, 256), mode='band', **a_cfg)
            o2 = diff_attention(p, px, p, lam_vecs, b_subln[e], lam_init=_lambda_init(l),
                                tq=_row_tile(seq, 256), has_lat=True)
            if need_ctx:
                ox1 = gqa_attention(px, px, None, tq=_row_tile(n_ctx, 256), mode='none', **a_cfg)
                ox2 = diff_attention(px, px, None, lam_vecs, b_subln[e], lam_init=_lambda_init(l),
                                     tq=_row_tile(n_ctx, 256), has_lat=False)
        else:
            kc = C_HEADS * HEAD_DIM
            vc = kc + C_KV_HEADS * HEAD_DIM
            c_cfg = dict(kv_heads=C_KV_HEADS, rep=C_HEADS // C_KV_HEADS, q_col=0, k_col=kc, v_col=vc)
            o1 = gqa_attention(p, px, p, tq=_row_tile(seq, 256), mode='full', **c_cfg)
            o2 = natten_attention(p, px, natten_bias_slabs(d_rpb[e]))
            if need_ctx:
                qd = vc + C_KV_HEADS * HEAD_DIM
                kd = qd + D_HEADS * HEAD_DIM
                vd = kd + D_HEADS * HEAD_DIM
                ox1 = gqa_attention(px, px, None, tq=_row_tile(n_ctx, 256), mode='none', **c_cfg)
                ox2 = gqa_attention(px, px, None, kv_heads=D_HEADS, rep=1, q_col=qd, k_col=kd, v_col=vd,
                                    tq=_row_tile(n_ctx, 256), mode='none')
        w_router_pad = jnp.zeros((d, ROUTER_LANES), BF16).at[:, :n_experts].set(w_router[l].astype(BF16))
        wg, wu, wd = w_gate[l].astype(BF16), w_up[l].astype(BF16), w_down[l].astype(BF16)
        x1, h, aff = out_projection(o1, o2, w_out_l, x, gt1, g_ffn[l], sc2, sh2, w_router_pad, n_experts,
                                    _row_tile(seq, 256))
        x = expert_choice_ffn(h, aff, x1, gt2, wg, wu, wd, n_experts)
        if need_ctx:
            xc1, hx, affx = out_projection(ox1, ox2, w_out_l, xc, gt1x, g_ffn[l], sc2x, sh2x, w_router_pad,
                                           n_experts, _row_tile(n_ctx, 256))
            xc = expert_choice_ffn(hx, affx, xc1, gt2x, wg, wu, wd, n_experts)
    return final_norm(x, g_final, _row_tile(seq, 512))
```

```python
import functools
import math

import jax
import jax.numpy as jnp
from jax import lax
from jax.experimental import pallas as pl
from jax.experimental.pallas import tpu as pltpu

F32 = jnp.float32
BF16 = jnp.bfloat16

HEAD_DIM = 128
GRID_W = 64
ROPE_THETA = 10000.0
EPS = 1e-6
NEG_INF = -1e30
LOG2E = math.log2(math.e)
A_HEADS, A_KV_HEADS, A_WINDOW = 8, 2, 128
B_HEADS = 4
C_HEADS, C_KV_HEADS = 8, 2
D_HEADS, NA_ROWS, NA_COLS = 8, 8, 16
N_EXPERTS = 16
EC_CAPACITY_FACTOR = 2
IN_WIDTH = 4608
MIX_WIDTH = 2048
ROUTER_LANES = 128

ROPE_BIT, QNORM_BIT, KNORM_BIT, SCALE_BIT = 1, 2, 4, 8
IN_TN = 256

V7X_VMEM_LIMIT = 56 * 1024 * 1024
V7X_VMEM_LIMIT_LARGE = 60 * 1024 * 1024


def _params(sem, vmem=V7X_VMEM_LIMIT):
    return pltpu.CompilerParams(dimension_semantics=sem, vmem_limit_bytes=vmem)


def _nt_dot(a, b):
    return lax.dot_general(a, b, (((1,), (1,)), ((), ())), preferred_element_type=F32)


def _ada_kernel(c_ref, w_ref, b_ref, o_ref):
    c = c_ref[...]
    cond = c * (1.0 / (1.0 + jnp.exp(-c)))
    o_ref[0] = jnp.dot(cond.astype(BF16), w_ref[0].astype(BF16), preferred_element_type=F32) + b_ref[0]


def ada_modulation(c_all, w_ada, b_ada):
    depth, d, n = w_ada.shape
    r = c_all.shape[0]
    tn = math.gcd(n, 1024)
    return pl.pallas_call(
        _ada_kernel,
        grid=(depth, n // tn),
        in_specs=[pl.BlockSpec((r, d), lambda l, j: (0, 0)),
                  pl.BlockSpec((1, d, tn), lambda l, j: (l, 0, j)),
                  pl.BlockSpec((1, 1, tn), lambda l, j: (l, 0, j))],
        out_specs=pl.BlockSpec((1, r, tn), lambda l, j: (l, 0, j)),
        out_shape=jax.ShapeDtypeStruct((depth, r, n), F32),
        compiler_params=_params(("arbitrary", "arbitrary")),
        name="ada_modulation",
    )(c_all, w_ada, b_ada.reshape(depth, 1, n))


def _rope_rotate(y, sin_signed):
    lane = lax.broadcasted_iota(jnp.int32, y.shape, 1)
    first = (lane & 63) < 32
    return jnp.where(first, pltpu.roll(y, 96, 1), pltpu.roll(y, 32, 1)) * sin_signed


def _in_proj_kernel(x_ref, g_ref, sc_ref, sh_ref, w_ref, cos_ref, sin_ref, qn_ref, kn_ref, o_ref, xn_ref,
                    *, codes, scale):
    x = x_ref[0]
    ms = jnp.mean(x * x, axis=-1, keepdims=True)
    y = x * lax.rsqrt(ms + EPS) * g_ref[...]
    xn_ref[...] = (y * (1.0 + sc_ref[0]) + sh_ref[0]).astype(BF16)
    for j, code in enumerate(codes):
        acc = jnp.dot(xn_ref[...], w_ref[:, j * IN_TN:(j + 1) * IN_TN], preferred_element_type=F32)
        for h in range(IN_TN // HEAD_DIM):
            y = acc[:, h * HEAD_DIM:(h + 1) * HEAD_DIM]
            if code & (QNORM_BIT | KNORM_BIT):
                gn_ref = qn_ref if code & QNORM_BIT else kn_ref
                ms = jnp.mean(y * y, axis=-1, keepdims=True)
                y = y * lax.rsqrt(ms + EPS) * gn_ref[...]
            if code & ROPE_BIT:
                y = y * cos_ref[...] + _rope_rotate(y, sin_ref[...])
            if code & SCALE_BIT:
                y = y * scale
            lo = j * IN_TN + h * HEAD_DIM
            o_ref[0, :, lo:lo + HEAD_DIM] = y.astype(BF16)


def in_projection(x, g, scale1, shift1, w_bf16, codes, cos, sin_signed, qn, kn, tm):
    b, n, d = x.shape
    width = w_bf16.shape[1]
    assert len(codes) * IN_TN == width
    per_batch = scale1.shape[0] > 1
    mod_map = (lambda bi, i: (bi, 0, 0)) if per_batch else (lambda bi, i: (0, 0, 0))
    const = lambda bi, i: (0, 0)
    return pl.pallas_call(
        functools.partial(_in_proj_kernel, codes=tuple(codes), scale=HEAD_DIM ** -0.5 * LOG2E),
        grid=(b, n // tm),
        in_specs=[pl.BlockSpec((1, tm, d), lambda bi, i: (bi, i, 0)),
                  pl.BlockSpec((1, d), const),
                  pl.BlockSpec((1, 1, d), mod_map),
                  pl.BlockSpec((1, 1, d), mod_map),
                  pl.BlockSpec((d, width), const, pipeline_mode=pl.Buffered(1)),
                  pl.BlockSpec((tm, HEAD_DIM), lambda bi, i: (i, 0)),
                  pl.BlockSpec((tm, HEAD_DIM), lambda bi, i: (i, 0)),
                  pl.BlockSpec((1, HEAD_DIM), const),
                  pl.BlockSpec((1, HEAD_DIM), const)],
        out_specs=pl.BlockSpec((1, tm, width), lambda bi, i: (bi, i, 0)),
        out_shape=jax.ShapeDtypeStruct((b, n, width), BF16),
        scratch_shapes=[pltpu.VMEM((tm, d), BF16)],
        compiler_params=_params(("arbitrary", "arbitrary")),
        name="in_projection",
    )(x, g.reshape(1, d), scale1, shift1, w_bf16, cos, sin_signed, qn, kn)


KEY_CHUNK = 256


class _SoftmaxChain:
    def __init__(self, q, chunks, s_ref, dv, sink=None):
        self.q, self.chunks, self.s_ref, self.sink = q, chunks, s_ref, sink
        m = q.shape[0]
        self.ones_col = dv == HEAD_DIM
        self.dv = dv
        self.mx = jnp.full((m, HEAD_DIM), NEG_INF, F32)
        self.ls = jnp.zeros((m, HEAD_DIM), F32)
        self.acc = jnp.zeros((m, 2 * HEAD_DIM if self.ones_col else dv), F32)

    def score_steps(self):
        return [functools.partial(self._score, j) for j in range(len(self.chunks))]

    def value_steps(self):
        return [functools.partial(self._value, j) for j in range(len(self.chunks))]

    def _score(self, j):
        k_fn, _, bias_fn = self.chunks[j]
        s = _nt_dot(self.q, k_fn())
        if bias_fn is not None:
            s = s + bias_fn()
        self.s_ref[:, j * KEY_CHUNK:(j + 1) * KEY_CHUNK] = s
        for t in range(KEY_CHUNK // HEAD_DIM):
            self.mx = jnp.maximum(self.mx, s[:, t * HEAD_DIM:(t + 1) * HEAD_DIM])

    def finish_scores(self):
        m = jnp.max(self.mx, axis=-1, keepdims=True)
        if self.sink is not None:
            m = jnp.maximum(m, self.sink)
        self.m = m
        self.m_wide = jnp.broadcast_to(m, self.mx.shape)

    def _value(self, j):
        _, v_fn, _ = self.chunks[j]
        parts = []
        for t in range(KEY_CHUNK // HEAD_DIM):
            lo = j * KEY_CHUNK + t * HEAD_DIM
            e = jnp.exp2(self.s_ref[:, lo:lo + HEAD_DIM] - self.m_wide)
            if not self.ones_col:
                self.ls = self.ls + e
            parts.append(e.astype(BF16))
        p = jnp.concatenate(parts, axis=1)
        v = v_fn()
        if self.ones_col:
            v = jnp.concatenate([v, jnp.ones(v.shape, BF16)], axis=1)
        self.acc = self.acc + jnp.dot(p, v, preferred_element_type=F32)

    def finish(self):
        if self.ones_col:
            num, den = self.acc[:, :self.dv], self.acc[:, self.dv:]
        else:
            num, den = self.acc, jnp.sum(self.ls, axis=-1, keepdims=True)
        if self.sink is not None:
            den = den + jnp.exp2(self.sink - self.m)
        return num, den


def _run_chains(chains):
    prev = None
    for chain in chains:
        scores = chain.score_steps()
        values = prev.value_steps() if prev is not None else []
        for i in range(max(len(scores), len(values))):
            if i < len(values):
                values[i]()
            if i < len(scores):
                scores[i]()
        chain.finish_scores()
        prev = chain
    for step in prev.value_steps():
        step()


def _ref_chunks(k_ref, v_ref, start, length, bias=None, k_cols=slice(None), v_cols=slice(None)):
    assert length % KEY_CHUNK == 0
    out = []
    for j in range(length // KEY_CHUNK):
        if isinstance(start, int):
            rows = slice(start + j * KEY_CHUNK, start + (j + 1) * KEY_CHUNK)
        else:
            rows = pl.ds(start + j * KEY_CHUNK, KEY_CHUNK)
        out.append((lambda rows=rows: k_ref[0, rows, k_cols], lambda rows=rows: v_ref[0, rows, v_cols],
                    (lambda j=j: bias(j)) if bias is not None else None))
    return out


def _gqa_kernel(*refs, rep, tq, mode, seq, use_sink):
    if use_sink:
        sink_ref, refs = refs[0], refs[1:]
    s_ref = refs[-1]
    refs = refs[:-1]
    if mode == 'none':
        q_ref, kx_ref, vx_ref, o_ref = refs
    else:
        q_ref, kx_ref, vx_ref, kl_ref, vl_ref, o_ref = refs
    g = pl.program_id(1)
    qi = pl.program_id(2)
    chunks = _ref_chunks(kx_ref, vx_ref, 0, kx_ref.shape[1])
    if mode == 'full':
        chunks += _ref_chunks(kl_ref, vl_ref, 0, seq)
    elif mode == 'band':
        band = tq + 2 * A_WINDOW
        start = pl.multiple_of(jnp.clip(qi * tq - A_WINDOW, 0, seq - band), HEAD_DIM)
        rel = (lax.broadcasted_iota(jnp.int32, (tq, KEY_CHUNK), 0)
               - lax.broadcasted_iota(jnp.int32, (tq, KEY_CHUNK), 1))

        def band_bias(j):
            dist = rel + (qi * tq - start - j * KEY_CHUNK)
            return jnp.where(jnp.abs(dist) <= A_WINDOW, 0.0, NEG_INF)

        chunks += _ref_chunks(kl_ref, vl_ref, start, band, band_bias)
    q = q_ref[0]
    chains = [_SoftmaxChain(q[:, r * HEAD_DIM:(r + 1) * HEAD_DIM], chunks, s_ref.at[r], HEAD_DIM,
                            sink=sink_ref[g * rep + r] * LOG2E if use_sink else None) for r in range(rep)]
    _run_chains(chains)
    for r, chain in enumerate(chains):
        acc, den = chain.finish()
        o_ref[0, :, r * HEAD_DIM:(r + 1) * HEAD_DIM] = (acc / den).astype(BF16)


def gqa_attention(p_q, p_ctx, p_lat, *, kv_heads, rep, q_col, k_col, v_col, tq, mode, sink=None):
    b, nq, _ = p_q.shape
    c = p_ctx.shape[1]
    qw = rep * HEAD_DIM
    use_sink = sink is not None
    assert tq & (tq - 1) == 0
    in_specs = [pl.BlockSpec((1, tq, qw), lambda bi, g, qi, *_: (bi, qi, q_col // qw + g)),
                pl.BlockSpec((1, c, HEAD_DIM), lambda bi, g, qi, *_: (bi, 0, k_col // HEAD_DIM + g)),
                pl.BlockSpec((1, c, HEAD_DIM), lambda bi, g, qi, *_: (bi, 0, v_col // HEAD_DIM + g))]
    args = [p_q, p_ctx, p_ctx]
    seq = 0
    if mode != 'none':
        seq = p_lat.shape[1]
        in_specs += [pl.BlockSpec((1, seq, HEAD_DIM), lambda bi, g, qi, *_: (bi, 0, k_col // HEAD_DIM + g)),
                     pl.BlockSpec((1, seq, HEAD_DIM), lambda bi, g, qi, *_: (bi, 0, v_col // HEAD_DIM + g))]
        args += [p_lat, p_lat]
    if use_sink:
        args = [sink] + args
    n_keys = c + {'none': 0, 'full': seq, 'band': tq + 2 * A_WINDOW}[mode]
    grid_spec = pltpu.PrefetchScalarGridSpec(
        num_scalar_prefetch=1 if use_sink else 0,
        grid=(b, kv_heads, nq // tq),
        in_specs=in_specs,
        out_specs=pl.BlockSpec((1, tq, qw), lambda bi, g, qi, *_: (bi, qi, g)),
        scratch_shapes=[pltpu.VMEM((rep, tq, n_keys), F32)])
    return pl.pallas_call(
        functools.partial(_gqa_kernel, rep=rep, tq=tq, mode=mode, seq=seq, use_sink=use_sink),
        grid_spec=grid_spec,
        out_shape=jax.ShapeDtypeStruct((b, nq, kv_heads * qw), BF16),
        compiler_params=_params(("arbitrary", "arbitrary", "arbitrary")),
        name=f"gqa_attention_{mode}",
    )(*args)


def _diff_kernel(*refs, has_lat, lam_init):
    if has_lat:
        lq1, lk1, lq2, lk2, sub_ref, q_ref, kx_ref, vx_ref, kl_ref, vl_ref, o_ref, s_ref = refs
    else:
        lq1, lk1, lq2, lk2, sub_ref, q_ref, kx_ref, vx_ref, o_ref, s_ref = refs
    lam = (jnp.exp(jnp.sum(lq1[...] * lk1[...], axis=-1, keepdims=True))
           - jnp.exp(jnp.sum(lq2[...] * lk2[...], axis=-1, keepdims=True)) + lam_init)
    n_ctx = vx_ref.shape[1]
    dv = 2 * HEAD_DIM
    chains = []
    for h in range(DIFF_HEADS_PER_STEP):
        v_cols = slice(h * dv, (h + 1) * dv)
        for m in range(2):
            k_cols = slice((2 * h + m) * HEAD_DIM, (2 * h + m + 1) * HEAD_DIM)
            chunks = _ref_chunks(kx_ref, vx_ref, 0, n_ctx, k_cols=k_cols, v_cols=v_cols)
            if has_lat:
                chunks += _ref_chunks(kl_ref, vl_ref, 0, vl_ref.shape[1], k_cols=k_cols, v_cols=v_cols)
            chains.append(_SoftmaxChain(q_ref[0, :, k_cols], chunks, s_ref.at[2 * h + m], dv))
    _run_chains(chains)
    for h in range(DIFF_HEADS_PER_STEP):
        acc1, den1 = chains[2 * h].finish()
        acc2, den2 = chains[2 * h + 1].finish()
        o = acc1 / den1 - lam * (acc2 / den2)
        ms = jnp.mean(o * o, axis=-1, keepdims=True)
        o_ref[0, :, h * dv:(h + 1) * dv] = (o * lax.rsqrt(ms + EPS) * sub_ref[...] * (1.0 - lam_init)).astype(BF16)


DIFF_HEADS_PER_STEP = 2


def diff_attention(p_q, p_ctx, p_lat, lam_vecs, subln, *, lam_init, tq, has_lat):
    b, nq, _ = p_q.shape
    c = p_ctx.shape[1]
    qb = (A_HEADS + 2 * A_KV_HEADS) * HEAD_DIM
    kb = qb + B_HEADS * 2 * HEAD_DIM
    vb = kb + B_HEADS * 2 * HEAD_DIM
    dv = 2 * HEAD_DIM
    wide = DIFF_HEADS_PER_STEP * dv
    assert qb % wide == 0 and kb % wide == 0 and vb % wide == 0 and B_HEADS % DIFF_HEADS_PER_STEP == 0
    vec = pl.BlockSpec((1, HEAD_DIM), lambda bi, h, qi: (0, 0))

    def cols(n, base):
        return pl.BlockSpec((1, n, wide), lambda bi, h, qi: (bi, 0, base // wide + h))

    in_specs = [vec, vec, vec, vec, pl.BlockSpec((1, dv), lambda bi, h, qi: (0, 0)),
                pl.BlockSpec((1, tq, wide), lambda bi, h, qi: (bi, qi, qb // wide + h)),
                cols(c, kb), cols(c, vb)]
    args = [v.reshape(1, HEAD_DIM) for v in lam_vecs] + [subln.reshape(1, dv), p_q, p_ctx, p_ctx]
    seq = 0
    if has_lat:
        seq = p_lat.shape[1]
        in_specs += [cols(seq, kb), cols(seq, vb)]
        args += [p_lat, p_lat]
    return pl.pallas_call(
        functools.partial(_diff_kernel, has_lat=has_lat, lam_init=lam_init),
        grid=(b, B_HEADS // DIFF_HEADS_PER_STEP, nq // tq),
        in_specs=in_specs,
        out_specs=pl.BlockSpec((1, tq, wide), lambda bi, h, qi: (bi, qi, h)),
        out_shape=jax.ShapeDtypeStruct((b, nq, B_HEADS * dv), BF16),
        scratch_shapes=[pltpu.VMEM((2 * DIFF_HEADS_PER_STEP, tq, c + seq), F32)],
        compiler_params=_params(("arbitrary", "arbitrary", "arbitrary")),
        name="diff_attention",
    )(*args)


NA_QROWS = 4
NA_WROWS = NA_QROWS + NA_ROWS
NA_BLOCKS_PER_STEP = 2


def _natten_kernel(q_ref, kx_ref, vx_ref, kl_ref, vl_ref, tb_ref, o_ref, s_ref, *, n_rows):
    n_blocks = n_rows // NA_QROWS
    tq = NA_QROWS * GRID_W
    n_ctx = kx_ref.shape[1]
    chains = []
    for i in range(NA_BLOCKS_PER_STEP):
        blk = pl.program_id(2) * NA_BLOCKS_PER_STEP + i
        first_row = jnp.clip(blk * NA_QROWS - NA_ROWS // 2, 0, n_rows - NA_WROWS)
        start = pl.multiple_of(first_row * GRID_W, GRID_W)
        variant = jnp.where(blk == 0, 0, jnp.where(blk == n_blocks - 1, 2, 1))

        def bias(j, variant=variant):
            return tb_ref[0, variant, :, j * KEY_CHUNK:(j + 1) * KEY_CHUNK]

        chunks = (_ref_chunks(kx_ref, vx_ref, 0, n_ctx)
                  + _ref_chunks(kl_ref, vl_ref, start, NA_WROWS * GRID_W, bias))
        chains.append(_SoftmaxChain(q_ref[0, i * tq:(i + 1) * tq, :], chunks, s_ref.at[i], HEAD_DIM))
    _run_chains(chains)
    for i, chain in enumerate(chains):
        acc, den = chain.finish()
        o_ref[0, i * tq:(i + 1) * tq, :] = (acc / den).astype(BF16)


def natten_bias_slabs(rpb):
    col = jnp.arange(GRID_W)
    cs = jnp.clip(col - NA_COLS // 2, 0, GRID_W - NA_COLS)
    col_mask = (col[None, :] >= cs[:, None]) & (col[None, :] < cs[:, None] + NA_COLS)
    col_idx = jnp.clip(col[None, :] - col[:, None], -(NA_COLS - 1), NA_COLS - 1) + NA_COLS - 1
    rpb_c = jnp.where(col_mask[None, None], rpb[:, :, col_idx], NEG_INF)
    i = jnp.arange(NA_QROWS)[:, None]
    a = jnp.arange(NA_WROWS)[None, :]
    half = NA_ROWS // 2
    kinds = [(a < NA_ROWS, a - i + NA_ROWS - 1),
             ((a >= i) & (a < i + NA_ROWS), a - i - half + NA_ROWS - 1),
             (a >= NA_WROWS - NA_ROWS, a - i - NA_ROWS + NA_ROWS - 1)]
    slabs = []
    for valid, roff in kinds:
        t = rpb_c[:, jnp.clip(roff, 0, 2 * NA_ROWS - 2)] * LOG2E
        t = jnp.where(valid[None, :, :, None, None], t, NEG_INF)
        slabs.append(jnp.transpose(t, (0, 1, 3, 2, 4)).reshape(rpb.shape[0], NA_QROWS * GRID_W, NA_WROWS * GRID_W))
    return jnp.stack(slabs, axis=1).astype(F32)


def natten_attention(p_lat, p_ctx, slabs):
    b, seq, _ = p_lat.shape
    c = p_ctx.shape[1]
    n_rows = seq // GRID_W
    rows_per_step = NA_QROWS * NA_BLOCKS_PER_STEP
    assert n_rows >= NA_WROWS and n_rows % rows_per_step == 0
    qd = (C_HEADS + 2 * C_KV_HEADS) * HEAD_DIM
    kd = qd + D_HEADS * HEAD_DIM
    vd = kd + D_HEADS * HEAD_DIM
    tq = rows_per_step * GRID_W

    def col(n, base):
        return pl.BlockSpec((1, n, HEAD_DIM), lambda bi, h, i: (bi, 0, base // HEAD_DIM + h))

    in_specs = [pl.BlockSpec((1, tq, HEAD_DIM), lambda bi, h, i: (bi, i, qd // HEAD_DIM + h)),
                col(c, kd), col(c, vd), col(seq, kd), col(seq, vd),
                pl.BlockSpec((1, 3, NA_QROWS * GRID_W, NA_WROWS * GRID_W), lambda bi, h, i: (h, 0, 0, 0))]
    return pl.pallas_call(
        functools.partial(_natten_kernel, n_rows=n_rows),
        grid=(b, D_HEADS, n_rows // rows_per_step),
        in_specs=in_specs,
        out_specs=pl.BlockSpec((1, tq, HEAD_DIM), lambda bi, h, i: (bi, i, h)),
        out_shape=jax.ShapeDtypeStruct((b, seq, D_HEADS * HEAD_DIM), BF16),
        scratch_shapes=[pltpu.VMEM((NA_BLOCKS_PER_STEP, NA_QROWS * GRID_W, c + NA_WROWS * GRID_W), F32)],
        compiler_params=_params(("arbitrary", "arbitrary", "arbitrary")),
        name="natten_attention",
    )(p_lat, p_ctx, p_ctx, p_lat, p_lat, slabs)


def _out_proj_kernel(o1_ref, o2_ref, w_ref, x_ref, gt_ref, g_ref, sc_ref, sh_ref, wr_ref, xo_ref, h_ref, aff_ref,
                     *, n_experts):
    y = (jnp.dot(o1_ref[0], w_ref[0], preferred_element_type=F32)
         + jnp.dot(o2_ref[0], w_ref[1], preferred_element_type=F32))
    x = x_ref[0] + gt_ref[0] * y
    xo_ref[0] = x
    ms = jnp.mean(x * x, axis=-1, keepdims=True)
    h = x * lax.rsqrt(ms + EPS) * g_ref[...]
    h = h * (1.0 + sc_ref[0]) + sh_ref[0]
    h_ref[0] = h
    logits = jnp.dot(h.astype(BF16), wr_ref[...], preferred_element_type=F32)
    lane = lax.broadcasted_iota(jnp.int32, logits.shape, 1)
    logits = jnp.where(lane < n_experts, logits, NEG_INF)
    e = jnp.exp(logits - jnp.max(logits, axis=-1, keepdims=True))
    aff_ref[0] = e / jnp.sum(e, axis=-1, keepdims=True)


def out_projection(o1, o2, w_bf16, x, gate1, g_ffn, scale2, shift2, w_router_pad, n_experts, tm):
    b, n, d = x.shape
    k = o1.shape[2]
    assert o2.shape[2] == k and w_bf16.shape == (2, k, d)
    per_batch = gate1.shape[0] > 1
    mod_map = (lambda bi, i: (bi, 0, 0)) if per_batch else (lambda bi, i: (0, 0, 0))
    row = lambda bi, i: (bi, i, 0)
    return pl.pallas_call(
        functools.partial(_out_proj_kernel, n_experts=n_experts),
        grid=(b, n // tm),
        in_specs=[pl.BlockSpec((1, tm, k), row),
                  pl.BlockSpec((1, tm, k), row),
                  pl.BlockSpec((2, k, d), lambda bi, i: (0, 0, 0)),
                  pl.BlockSpec((1, tm, d), row),
                  pl.BlockSpec((1, 1, d), mod_map),
                  pl.BlockSpec((1, d), lambda bi, i: (0, 0)),
                  pl.BlockSpec((1, 1, d), mod_map),
                  pl.BlockSpec((1, 1, d), mod_map),
                  pl.BlockSpec((d, ROUTER_LANES), lambda bi, i: (0, 0))],
        out_specs=[pl.BlockSpec((1, tm, d), row), pl.BlockSpec((1, tm, d), row),
                   pl.BlockSpec((1, tm, ROUTER_LANES), row)],
        out_shape=[jax.ShapeDtypeStruct((b, n, d), F32), jax.ShapeDtypeStruct((b, n, d), F32),
                   jax.ShapeDtypeStruct((b, n, ROUTER_LANES), F32)],
        compiler_params=_params(("arbitrary", "arbitrary")),
        name="out_projection",
    )(o1, o2, w_bf16, x, gate1, g_ffn.reshape(1, d), scale2, shift2, w_router_pad)


MOE_ROW_CHUNK = 1024


def _gather_kernel(idx_ref, *refs, cap, dc):
    h_refs, o_ref, stage_ref = refs[:-2], refs[-2], refs[-1]

    def body(c, carry):
        t = idx_ref[0, 0, 0, c]
        for k, h_ref in enumerate(h_refs):
            stage_ref[pl.ds(c, 1), k * dc:(k + 1) * dc] = h_ref[0, pl.ds(t, 1), :]
        return carry
    lax.fori_loop(0, cap, body, 0, unroll=8)
    o_ref[0, 0] = stage_ref[...].astype(BF16)


def moe_gather(h, idx, dc):
    b, n, d = h.shape
    _, e, cap = idx.shape
    n_chunks = d // dc
    h_specs = [pl.BlockSpec((1, n, dc), lambda bi, ei, k=k: (bi, 0, k), pipeline_mode=pl.Buffered(1))
               for k in range(n_chunks)]
    return pl.pallas_call(
        functools.partial(_gather_kernel, cap=cap, dc=dc),
        grid=(b, e),
        in_specs=[pl.BlockSpec((1, 1, 1, cap), lambda bi, ei: (bi, ei, 0, 0), memory_space=pltpu.SMEM)] + h_specs,
        out_specs=pl.BlockSpec((1, 1, cap, d), lambda bi, ei: (bi, ei, 0, 0)),
        out_shape=jax.ShapeDtypeStruct((b, e, cap, d), BF16),
        scratch_shapes=[pltpu.VMEM((cap, d), F32)],
        compiler_params=_params(("arbitrary", "arbitrary")),
        name="moe_gather",
    )(idx.reshape(b, e, 1, cap), *([h] * n_chunks))


EXPERT_F_CHUNK = 256


def _expert_kernel(*refs, n_chunks, step_ranges):
    n_groups = len(step_ranges)
    in_refs = refs[:2 * n_groups]
    wg_ref, wu_ref, wd_ref = refs[2 * n_groups:2 * n_groups + 3]
    out_refs = refs[2 * n_groups + 3:3 * n_groups + 3]
    wg_s, wu_s, wd_s = refs[-3:]
    step = pl.program_id(1)

    @pl.when(step < n_chunks)
    def _():
        wg_s[step] = wg_ref[0, 0].astype(BF16)
        wu_s[step] = wu_ref[0, 0].astype(BF16)
        wd_s[step] = wd_ref[0, 0].astype(BF16)

    for g, (lo, hi) in enumerate(step_ranges):
        xe_ref, gate_ref, o_ref = in_refs[2 * g], in_refs[2 * g + 1], out_refs[g]

        @pl.when((step >= lo) & (step < hi))
        def _(xe_ref=xe_ref, gate_ref=gate_ref, o_ref=o_ref):
            bs, _, cap, d = xe_ref.shape
            xe = xe_ref[...].reshape(bs * cap, d)
            ye = jnp.zeros((bs * cap, d), F32)
            for k in range(n_chunks):
                a = jnp.dot(xe, wg_s[k], preferred_element_type=F32)
                u = jnp.dot(xe, wu_s[k], preferred_element_type=F32)
                hid = (a * (1.0 / (1.0 + jnp.exp(-a))) * u).astype(BF16)
                ye = ye + jnp.dot(hid, wd_s[k], preferred_element_type=F32)
            o_ref[...] = (ye * gate_ref[...].reshape(bs * cap, 1)).reshape(bs, 1, cap, d)


def moe_experts(groups, wg, wu, wd, layer):
    _, e, d, f = wg.shape
    fc = math.gcd(f, EXPERT_F_CHUNK)
    n_chunks = f // fc
    in_specs, args, out_specs, out_shapes, step_ranges = [], [], [], [], []
    lo = n_chunks
    for xe, gate in groups:
        b, _, cap, _ = xe.shape
        bs = max(1, min(b, 512 // cap))
        assert b % bs == 0
        nb = b // bs
        tile = lambda ei, s, lo=lo, nb=nb: (jnp.clip(s - lo, 0, nb - 1), ei, 0, 0)
        in_specs += [pl.BlockSpec((bs, 1, cap, d), tile), pl.BlockSpec((bs, 1, cap, 1), tile)]
        args += [xe, gate]
        out_specs.append(pl.BlockSpec((bs, 1, cap, d), tile))
        out_shapes.append(jax.ShapeDtypeStruct((b, e, cap, d), F32))
        step_ranges.append((lo, lo + nb))
        lo += nb
    chunk = lambda ei, s: jnp.minimum(s, n_chunks - 1)
    in_specs += [pl.BlockSpec((1, 1, d, fc), lambda ei, s: (layer, ei, 0, chunk(ei, s))),
                 pl.BlockSpec((1, 1, d, fc), lambda ei, s: (layer, ei, 0, chunk(ei, s))),
                 pl.BlockSpec((1, 1, fc, d), lambda ei, s: (layer, ei, chunk(ei, s), 0))]
    return pl.pallas_call(
        functools.partial(_expert_kernel, n_chunks=n_chunks, step_ranges=tuple(step_ranges)),
        grid=(e, lo),
        in_specs=in_specs,
        out_specs=out_specs,
        out_shape=out_shapes,
        scratch_shapes=[pltpu.VMEM((n_chunks, d, fc), BF16), pltpu.VMEM((n_chunks, d, fc), BF16),
                        pltpu.VMEM((n_chunks, fc, d), BF16)],
        compiler_params=_params(("arbitrary", "arbitrary")),
        name="moe_experts",
    )(*args, wg, wu, wd)


def _combine_kernel(idx_ref, ye_ref, x_ref, gt_ref, *refs, cap, n_experts, rows_out, final_norm):
    gf_ref = refs[0] if final_norm else None
    o_ref, acc_ref = refs[-2:]
    step = pl.program_id(1)

    @pl.when(step == 0)
    def _():
        acc_ref[...] = jnp.zeros_like(acc_ref)

    @pl.when(step < n_experts)
    def _():
        def body(c, carry):
            t = idx_ref[0, 0, 0, c]
            acc_ref[pl.ds(t, 1), :] = acc_ref[pl.ds(t, 1), :] + ye_ref[0, 0, pl.ds(c, 1), :]
            return carry
        lax.fori_loop(0, cap, body, 0, unroll=8)

    @pl.when(step >= n_experts)
    def _():
        r0 = pl.multiple_of((step - n_experts) * rows_out, rows_out)
        x = x_ref[0] + gt_ref[0] * acc_ref[pl.ds(r0, rows_out), :]
        if final_norm:
            ms = jnp.mean(x * x, axis=-1, keepdims=True)
            x = x * lax.rsqrt(ms + EPS) * gf_ref[...]
        o_ref[0] = x


def moe_combine(ye, idx, x, gate2, final_gain=None):
    b, n, d = x.shape
    final_norm = final_gain is not None
    extra_specs = [pl.BlockSpec((1, d), lambda bi, s: (0, 0))] if final_norm else []
    extra_args = [final_gain.reshape(1, d)] if final_norm else []
    _, e, cap = idx.shape
    rows_out = min(n, 256)
    per_batch = gate2.shape[0] > 1
    mod_map = (lambda bi, s: (bi, 0, 0)) if per_batch else (lambda bi, s: (0, 0, 0))
    expert = lambda bi, s: (bi, jnp.minimum(s, e - 1), 0, 0)
    tile = lambda bi, s: (bi, jnp.maximum(s - e, 0), 0)
    return pl.pallas_call(
        functools.partial(_combine_kernel, cap=cap, n_experts=e, rows_out=rows_out, final_norm=final_norm),
        grid=(b, e + n // rows_out),
        in_specs=[pl.BlockSpec((1, 1, 1, cap), expert, memory_space=pltpu.SMEM),
                  pl.BlockSpec((1, 1, cap, d), expert),
                  pl.BlockSpec((1, rows_out, d), tile),
                  pl.BlockSpec((1, 1, d), mod_map)] + extra_specs,
        out_specs=pl.BlockSpec((1, rows_out, d), tile),
        out_shape=jax.ShapeDtypeStruct((b, n, d), F32),
        scratch_shapes=[pltpu.VMEM((n, d), F32)],
        compiler_params=_params(("arbitrary", "arbitrary")),
        name="moe_combine",
    )(idx.reshape(b, e, 1, cap), ye, x, gate2, *extra_args)


def moe_dispatch(h, aff, n_experts):
    b, n, d = h.shape
    cap = max(1, EC_CAPACITY_FACTOR * n // n_experts)
    gate, idx = lax.top_k(jnp.swapaxes(aff[..., :n_experts], 1, 2), cap)
    return idx, gate[..., None], moe_gather(h, idx, min(d, MOE_ROW_CHUNK))


def _rope_tables(n):
    t = jnp.arange(n, dtype=jnp.int32)
    row = (t // GRID_W).astype(F32)
    col = (t % GRID_W).astype(F32)
    n_freq = HEAD_DIM // 4
    inv = ROPE_THETA ** (-jnp.arange(n_freq, dtype=F32) / n_freq)
    ar = row[:, None] * inv
    ac = col[:, None] * inv
    ang = jnp.concatenate([ar, ar, ac, ac], axis=-1)
    sign = jnp.where((jnp.arange(HEAD_DIM) % (2 * n_freq)) < n_freq, -1.0, 1.0).astype(F32)
    return jnp.cos(ang), jnp.sin(ang) * sign


def _tile_codes(layer, rope):
    if layer % 2 == 0:
        groups = [(A_HEADS * HEAD_DIM, ROPE_BIT | SCALE_BIT), (A_KV_HEADS * HEAD_DIM, ROPE_BIT),
                  (A_KV_HEADS * HEAD_DIM, 0), (B_HEADS * 2 * HEAD_DIM, ROPE_BIT | SCALE_BIT),
                  (B_HEADS * 2 * HEAD_DIM, ROPE_BIT), (B_HEADS * 2 * HEAD_DIM, 0)]
    else:
        groups = [(C_HEADS * HEAD_DIM, QNORM_BIT | ROPE_BIT | SCALE_BIT), (C_KV_HEADS * HEAD_DIM, KNORM_BIT | ROPE_BIT),
                  (C_KV_HEADS * HEAD_DIM, 0), (D_HEADS * HEAD_DIM, SCALE_BIT), (D_HEADS * HEAD_DIM, 0),
                  (D_HEADS * HEAD_DIM, 0)]
    codes = []
    for width, code in groups:
        assert width % IN_TN == 0
        codes += [code if rope else code & ~ROPE_BIT] * (width // IN_TN)
    return codes


def _lambda_init(layer):
    return 0.8 - 0.6 * math.exp(-0.3 * layer)


def _row_tile(n, target):
    t = min(n, target)
    assert n % t == 0
    return t


def kernel(x, c, ctx, c_ctx, w_ada, b_ada, g_mix, g_ffn, w_in, w_out, a_sink, b_lam_q1, b_lam_k1, b_lam_q2,
           b_lam_k2, b_subln, c_q_norm, c_k_norm, d_rpb, w_router, w_gate, w_up, w_down, g_final):
    bsz, seq, d = x.shape
    n_ctx = ctx.shape[1]
    depth = w_ada.shape[0]
    n_experts = w_router.shape[2]
    cos, sin_signed = _rope_tables(seq)
    ones_head = jnp.ones((1, HEAD_DIM), F32)

    n_rows = -(-(bsz + 1) // 8) * 8
    c_all = jnp.zeros((n_rows, d), F32).at[:bsz].set(c).at[bsz].set(c_ctx)
    mod = ada_modulation(c_all, w_ada, b_ada)

    xc = ctx
    for l in range(depth):
        need_ctx = l < depth - 1
        e = l // 2
        m_lat = mod[l, :bsz].reshape(bsz, 1, 6, d)
        m_ctx = mod[l, bsz:bsz + 1].reshape(1, 1, 6, d)
        sh1, sc1, gt1, sh2, sc2, gt2 = (m_lat[:, :, i] for i in range(6))
        sh1x, sc1x, gt1x, sh2x, sc2x, gt2x = (m_ctx[:, :, i] for i in range(6))
        w_in_l = w_in[l].astype(BF16)
        w_out_l = w_out[l].astype(BF16).reshape(2, MIX_WIDTH // 2, d)
        if l % 2 == 0:
            qn = kn = ones_head
        else:
            qn, kn = c_q_norm[e].reshape(1, HEAD_DIM), c_k_norm[e].reshape(1, HEAD_DIM)
        tm_lat = _row_tile(seq, 512)
        tm_ctx = _row_tile(n_ctx, 512)
        p = in_projection(x, g_mix[l], sc1, sh1, w_in_l, _tile_codes(l, True), cos, sin_signed, qn, kn, tm_lat)
        px = in_projection(xc, g_mix[l], sc1x, sh1x, w_in_l, _tile_codes(l, False), cos[:tm_ctx],
                           sin_signed[:tm_ctx], qn, kn, tm_ctx)
        ox1 = ox2 = None
        if l % 2 == 0:
            lam_vecs = (b_lam_q1[e], b_lam_k1[e], b_lam_q2[e], b_lam_k2[e])
            ka = A_HEADS * HEAD_DIM
            va = ka + A_KV_HEADS * HEAD_DIM
            a_cfg = dict(kv_heads=A_KV_HEADS, rep=A_HEADS // A_KV_HEADS, q_col=0, k_col=ka, v_col=va,
                         sink=a_sink[e])
            o1 = gqa_attention(p, px, p, tq=_row_tile(seq, 256), mode='band', **a_cfg)
            o2 = diff_attention(p, px, p, lam_vecs, b_subln[e], lam_init=_lambda_init(l),
                                tq=_row_tile(seq, 256), has_lat=True)
            if need_ctx:
                ox1 = gqa_attention(px, px, None, tq=_row_tile(n_ctx, 256), mode='none', **a_cfg)
                ox2 = diff_attention(px, px, None, lam_vecs, b_subln[e], lam_init=_lambda_init(l),
                                     tq=_row_tile(n_ctx, 256), has_lat=False)
        else:
            kc = C_HEADS * HEAD_DIM
            vc = kc + C_KV_HEADS * HEAD_DIM
            c_cfg = dict(kv_heads=C_KV_HEADS, rep=C_HEADS // C_KV_HEADS, q_col=0, k_col=kc, v_col=vc)
            o1 = gqa_attention(p, px, p, tq=_row_tile(seq, 256), mode='full', **c_cfg)
            o2 = natten_attention(p, px, natten_bias_slabs(d_rpb[e]))
            if need_ctx:
                qd = vc + C_KV_HEADS * HEAD_DIM
                kd = qd + D_HEADS * HEAD_DIM
                vd = kd + D_HEADS * HEAD_DIM
                ox1 = gqa_attention(px, px, None, tq=_row_tile(n_ctx, 256), mode='none', **c_cfg)
                ox2 = gqa_attention(px, px, None, kv_heads=D_HEADS, rep=1, q_col=qd, k_col=kd, v_col=vd,
                                    tq=_row_tile(n_ctx, 256), mode='none')
        w_router_pad = jnp.zeros((d, ROUTER_LANES), BF16).at[:, :n_experts].set(w_router[l].astype(BF16))
        x1, h, aff = out_projection(o1, o2, w_out_l, x, gt1, g_ffn[l], sc2, sh2, w_router_pad, n_experts,
                                    _row_tile(seq, 256))
        idx, gate, xe = moe_dispatch(h, aff, n_experts)
        groups = [(xe, gate)]
        if need_ctx:
            xc1, hx, affx = out_projection(ox1, ox2, w_out_l, xc, gt1x, g_ffn[l], sc2x, sh2x, w_router_pad,
                                           n_experts, _row_tile(n_ctx, 256))
            idx_x, gate_x, xe_x = moe_dispatch(hx, affx, n_experts)
            groups.append((xe_x, gate_x))
        ye = moe_experts(groups, w_gate, w_up, w_down, l)
        x = moe_combine(ye[0], idx, x1, gt2, g_final if l == depth - 1 else None)
        if need_ctx:
            xc = moe_combine(ye[1], idx_x, xc1, gt2x)
    return x
```

```python
import functools
import math

import jax
import jax.numpy as jnp
from jax import lax
from jax.experimental import pallas as pl
from jax.experimental.pallas import tpu as pltpu

F32 = jnp.float32
BF16 = jnp.bfloat16

HEAD_DIM = 128
GRID_W = 64
ROPE_THETA = 10000.0
EPS = 1e-6
NEG_INF = -1e30
LOG2E = math.log2(math.e)
A_HEADS, A_KV_HEADS, A_WINDOW = 8, 2, 128
B_HEADS = 4
C_HEADS, C_KV_HEADS = 8, 2
D_HEADS, NA_ROWS, NA_COLS = 8, 8, 16
N_EXPERTS = 16
EC_CAPACITY_FACTOR = 2
IN_WIDTH = 4608
MIX_WIDTH = 2048
ROUTER_LANES = 128

ROPE_BIT, QNORM_BIT, KNORM_BIT, SCALE_BIT = 1, 2, 4, 8
IN_TN = 256

V7X_VMEM_LIMIT = 56 * 1024 * 1024
V7X_VMEM_LIMIT_LARGE = 60 * 1024 * 1024


def _params(sem, vmem=V7X_VMEM_LIMIT):
    return pltpu.CompilerParams(dimension_semantics=sem, vmem_limit_bytes=vmem)


def _nt_dot(a, b):
    return lax.dot_general(a, b, (((1,), (1,)), ((), ())), preferred_element_type=F32)


def _ada_kernel(c_ref, w_ref, b_ref, o_ref):
    c = c_ref[...]
    cond = c * (1.0 / (1.0 + jnp.exp(-c)))
    o_ref[0] = jnp.dot(cond.astype(BF16), w_ref[0].astype(BF16), preferred_element_type=F32) + b_ref[0]


def ada_modulation(c_all, w_ada, b_ada):
    depth, d, n = w_ada.shape
    r = c_all.shape[0]
    tn = math.gcd(n, 1024)
    return pl.pallas_call(
        _ada_kernel,
        grid=(depth, n // tn),
        in_specs=[pl.BlockSpec((r, d), lambda l, j: (0, 0)),
                  pl.BlockSpec((1, d, tn), lambda l, j: (l, 0, j)),
                  pl.BlockSpec((1, 1, tn), lambda l, j: (l, 0, j))],
        out_specs=pl.BlockSpec((1, r, tn), lambda l, j: (l, 0, j)),
        out_shape=jax.ShapeDtypeStruct((depth, r, n), F32),
        compiler_params=_params(("arbitrary", "arbitrary")),
        name="ada_modulation",
    )(c_all, w_ada, b_ada.reshape(depth, 1, n))


def _rope_rotate(y, sin_signed):
    lane = lax.broadcasted_iota(jnp.int32, y.shape, 1)
    first = (lane & 63) < 32
    return jnp.where(first, pltpu.roll(y, 96, 1), pltpu.roll(y, 32, 1)) * sin_signed


def _in_proj_kernel(x_ref, g_ref, sc_ref, sh_ref, w_ref, cos_ref, sin_ref, qn_ref, kn_ref, o_ref, xn_ref,
                    *, codes, scale):
    x = x_ref[0]
    ms = jnp.mean(x * x, axis=-1, keepdims=True)
    y = x * lax.rsqrt(ms + EPS) * g_ref[...]
    xn_ref[...] = (y * (1.0 + sc_ref[0]) + sh_ref[0]).astype(BF16)
    for j, code in enumerate(codes):
        acc = jnp.dot(xn_ref[...], w_ref[:, j * IN_TN:(j + 1) * IN_TN], preferred_element_type=F32)
        for h in range(IN_TN // HEAD_DIM):
            y = acc[:, h * HEAD_DIM:(h + 1) * HEAD_DIM]
            if code & (QNORM_BIT | KNORM_BIT):
                gn_ref = qn_ref if code & QNORM_BIT else kn_ref
                ms = jnp.mean(y * y, axis=-1, keepdims=True)
                y = y * lax.rsqrt(ms + EPS) * gn_ref[...]
            if code & ROPE_BIT:
                y = y * cos_ref[...] + _rope_rotate(y, sin_ref[...])
            if code & SCALE_BIT:
                y = y * scale
            lo = j * IN_TN + h * HEAD_DIM
            o_ref[0, :, lo:lo + HEAD_DIM] = y.astype(BF16)


def in_projection(x, g, scale1, shift1, w_bf16, codes, cos, sin_signed, qn, kn, tm):
    b, n, d = x.shape
    width = w_bf16.shape[1]
    assert len(codes) * IN_TN == width
    per_batch = scale1.shape[0] > 1
    mod_map = (lambda bi, i: (bi, 0, 0)) if per_batch else (lambda bi, i: (0, 0, 0))
    const = lambda bi, i: (0, 0)
    return pl.pallas_call(
        functools.partial(_in_proj_kernel, codes=tuple(codes), scale=HEAD_DIM ** -0.5 * LOG2E),
        grid=(b, n // tm),
        in_specs=[pl.BlockSpec((1, tm, d), lambda bi, i: (bi, i, 0)),
                  pl.BlockSpec((1, d), const),
                  pl.BlockSpec((1, 1, d), mod_map),
                  pl.BlockSpec((1, 1, d), mod_map),
                  pl.BlockSpec((d, width), const, pipeline_mode=pl.Buffered(1)),
                  pl.BlockSpec((tm, HEAD_DIM), lambda bi, i: (i, 0)),
                  pl.BlockSpec((tm, HEAD_DIM), lambda bi, i: (i, 0)),
                  pl.BlockSpec((1, HEAD_DIM), const),
                  pl.BlockSpec((1, HEAD_DIM), const)],
        out_specs=pl.BlockSpec((1, tm, width), lambda bi, i: (bi, i, 0)),
        out_shape=jax.ShapeDtypeStruct((b, n, width), BF16),
        scratch_shapes=[pltpu.VMEM((tm, d), BF16)],
        compiler_params=_params(("arbitrary", "arbitrary")),
        name="in_projection",
    )(x, g.reshape(1, d), scale1, shift1, w_bf16, cos, sin_signed, qn, kn)


KEY_CHUNK = 256


class _SoftmaxChain:
    def __init__(self, q, chunks, s_ref, dv, sink=None):
        self.q, self.chunks, self.s_ref, self.sink = q, chunks, s_ref, sink
        m = q.shape[0]
        self.ones_col = dv == HEAD_DIM
        self.dv = dv
        self.mx = jnp.full((m, HEAD_DIM), NEG_INF, F32)
        self.ls = jnp.zeros((m, HEAD_DIM), F32)
        self.acc = jnp.zeros((m, 2 * HEAD_DIM if self.ones_col else dv), F32)

    def score_steps(self):
        return [functools.partial(self._score, j) for j in range(len(self.chunks))]

    def value_steps(self):
        return [functools.partial(self._value, j) for j in range(len(self.chunks))]

    def _score(self, j):
        k_fn, _, bias_fn = self.chunks[j]
        s = _nt_dot(self.q, k_fn())
        if bias_fn is not None:
            s = s + bias_fn()
        self.s_ref[:, j * KEY_CHUNK:(j + 1) * KEY_CHUNK] = s
        for t in range(KEY_CHUNK // HEAD_DIM):
            self.mx = jnp.maximum(self.mx, s[:, t * HEAD_DIM:(t + 1) * HEAD_DIM])

    def finish_scores(self):
        m = jnp.max(self.mx, axis=-1, keepdims=True)
        if self.sink is not None:
            m = jnp.maximum(m, self.sink)
        self.m = m
        self.m_wide = jnp.broadcast_to(m, self.mx.shape)

    def _value(self, j):
        _, v_fn, _ = self.chunks[j]
        parts = []
        for t in range(KEY_CHUNK // HEAD_DIM):
            lo = j * KEY_CHUNK + t * HEAD_DIM
            e = jnp.exp2(self.s_ref[:, lo:lo + HEAD_DIM] - self.m_wide)
            if not self.ones_col:
                self.ls = self.ls + e
            parts.append(e.astype(BF16))
        p = jnp.concatenate(parts, axis=1)
        v = v_fn()
        if self.ones_col:
            v = jnp.concatenate([v, jnp.ones(v.shape, BF16)], axis=1)
        self.acc = self.acc + jnp.dot(p, v, preferred_element_type=F32)

    def finish(self):
        if self.ones_col:
            num, den = self.acc[:, :self.dv], self.acc[:, self.dv:]
        else:
            num, den = self.acc, jnp.sum(self.ls, axis=-1, keepdims=True)
        if self.sink is not None:
            den = den + jnp.exp2(self.sink - self.m)
        return num, den


def _run_chains(chains):
    prev = None
    for chain in chains:
        scores = chain.score_steps()
        values = prev.value_steps() if prev is not None else []
        for i in range(max(len(scores), len(values))):
            if i < len(values):
                values[i]()
            if i < len(scores):
                scores[i]()
        chain.finish_scores()
        prev = chain
    for step in prev.value_steps():
        step()


def _ref_chunks(k_ref, v_ref, start, length, bias=None, k_cols=slice(None), v_cols=slice(None)):
    assert length % KEY_CHUNK == 0
    out = []
    for j in range(length // KEY_CHUNK):
        if isinstance(start, int):
            rows = slice(start + j * KEY_CHUNK, start + (j + 1) * KEY_CHUNK)
        else:
            rows = pl.ds(start + j * KEY_CHUNK, KEY_CHUNK)
        out.append((lambda rows=rows: k_ref[0, rows, k_cols], lambda rows=rows: v_ref[0, rows, v_cols],
                    (lambda j=j: bias(j)) if bias is not None else None))
    return out


def _gqa_kernel(*refs, rep, tq, mode, seq, use_sink):
    if use_sink:
        sink_ref, refs = refs[0], refs[1:]
    s_ref = refs[-1]
    refs = refs[:-1]
    if mode == 'none':
        q_ref, kx_ref, vx_ref, o_ref = refs
    else:
        q_ref, kx_ref, vx_ref, kl_ref, vl_ref, o_ref = refs
    g = pl.program_id(1)
    qi = pl.program_id(2)
    chunks = _ref_chunks(kx_ref, vx_ref, 0, kx_ref.shape[1])
    if mode == 'full':
        chunks += _ref_chunks(kl_ref, vl_ref, 0, seq)
    elif mode == 'band':
        band = tq + 2 * A_WINDOW
        start = pl.multiple_of(jnp.clip(qi * tq - A_WINDOW, 0, seq - band), HEAD_DIM)
        rel = (lax.broadcasted_iota(jnp.int32, (tq, KEY_CHUNK), 0)
               - lax.broadcasted_iota(jnp.int32, (tq, KEY_CHUNK), 1))

        def band_bias(j):
            dist = rel + (qi * tq - start - j * KEY_CHUNK)
            return jnp.where(jnp.abs(dist) <= A_WINDOW, 0.0, NEG_INF)

        chunks += _ref_chunks(kl_ref, vl_ref, start, band, band_bias)
    q = q_ref[0]
    chains = [_SoftmaxChain(q[:, r * HEAD_DIM:(r + 1) * HEAD_DIM], chunks, s_ref.at[r], HEAD_DIM,
                            sink=sink_ref[g * rep + r] * LOG2E if use_sink else None) for r in range(rep)]
    _run_chains(chains)
    for r, chain in enumerate(chains):
        acc, den = chain.finish()
        o_ref[0, :, r * HEAD_DIM:(r + 1) * HEAD_DIM] = (acc / den).astype(BF16)


def gqa_attention(p_q, p_ctx, p_lat, *, kv_heads, rep, q_col, k_col, v_col, tq, mode, sink=None):
    b, nq, _ = p_q.shape
    c = p_ctx.shape[1]
    qw = rep * HEAD_DIM
    use_sink = sink is not None
    assert tq & (tq - 1) == 0
    in_specs = [pl.BlockSpec((1, tq, qw), lambda bi, g, qi, *_: (bi, qi, q_col // qw + g)),
                pl.BlockSpec((1, c, HEAD_DIM), lambda bi, g, qi, *_: (bi, 0, k_col // HEAD_DIM + g)),
                pl.BlockSpec((1, c, HEAD_DIM), lambda bi, g, qi, *_: (bi, 0, v_col // HEAD_DIM + g))]
    args = [p_q, p_ctx, p_ctx]
    seq = 0
    if mode != 'none':
        seq = p_lat.shape[1]
        in_specs += [pl.BlockSpec((1, seq, HEAD_DIM), lambda bi, g, qi, *_: (bi, 0, k_col // HEAD_DIM + g)),
                     pl.BlockSpec((1, seq, HEAD_DIM), lambda bi, g, qi, *_: (bi, 0, v_col // HEAD_DIM + g))]
        args += [p_lat, p_lat]
    if use_sink:
        args = [sink] + args
    n_keys = c + {'none': 0, 'full': seq, 'band': tq + 2 * A_WINDOW}[mode]
    grid_spec = pltpu.PrefetchScalarGridSpec(
        num_scalar_prefetch=1 if use_sink else 0,
        grid=(b, kv_heads, nq // tq),
        in_specs=in_specs,
        out_specs=pl.BlockSpec((1, tq, qw), lambda bi, g, qi, *_: (bi, qi, g)),
        scratch_shapes=[pltpu.VMEM((rep, tq, n_keys), F32)])
    return pl.pallas_call(
        functools.partial(_gqa_kernel, rep=rep, tq=tq, mode=mode, seq=seq, use_sink=use_sink),
        grid_spec=grid_spec,
        out_shape=jax.ShapeDtypeStruct((b, nq, kv_heads * qw), BF16),
        compiler_params=_params(("arbitrary", "arbitrary", "arbitrary")),
        name=f"gqa_attention_{mode}",
    )(*args)


def _diff_kernel(*refs, has_lat, lam_init):
    if has_lat:
        lq1, lk1, lq2, lk2, sub_ref, q_ref, kx_ref, vx_ref, kl_ref, vl_ref, o_ref, s_ref = refs
    else:
        lq1, lk1, lq2, lk2, sub_ref, q_ref, kx_ref, vx_ref, o_ref, s_ref = refs
    lam = (jnp.exp(jnp.sum(lq1[...] * lk1[...], axis=-1, keepdims=True))
           - jnp.exp(jnp.sum(lq2[...] * lk2[...], axis=-1, keepdims=True)) + lam_init)
    n_ctx = vx_ref.shape[1]
    dv = 2 * HEAD_DIM
    chains = []
    for h in range(DIFF_HEADS_PER_STEP):
        v_cols = slice(h * dv, (h + 1) * dv)
        for m in range(2):
            k_cols = slice((2 * h + m) * HEAD_DIM, (2 * h + m + 1) * HEAD_DIM)
            chunks = _ref_chunks(kx_ref, vx_ref, 0, n_ctx, k_cols=k_cols, v_cols=v_cols)
            if has_lat:
                chunks += _ref_chunks(kl_ref, vl_ref, 0, vl_ref.shape[1], k_cols=k_cols, v_cols=v_cols)
            chains.append(_SoftmaxChain(q_ref[0, :, k_cols], chunks, s_ref.at[2 * h + m], dv))
    _run_chains(chains)
    for h in range(DIFF_HEADS_PER_STEP):
        acc1, den1 = chains[2 * h].finish()
        acc2, den2 = chains[2 * h + 1].finish()
        o = acc1 / den1 - lam * (acc2 / den2)
        ms = jnp.mean(o * o, axis=-1, keepdims=True)
        o_ref[0, :, h * dv:(h + 1) * dv] = (o * lax.rsqrt(ms + EPS) * sub_ref[...] * (1.0 - lam_init)).astype(BF16)


DIFF_HEADS_PER_STEP = 2


def diff_attention(p_q, p_ctx, p_lat, lam_vecs, subln, *, lam_init, tq, has_lat):
    b, nq, _ = p_q.shape
    c = p_ctx.shape[1]
    qb = (A_HEADS + 2 * A_KV_HEADS) * HEAD_DIM
    kb = qb + B_HEADS * 2 * HEAD_DIM
    vb = kb + B_HEADS * 2 * HEAD_DIM
    dv = 2 * HEAD_DIM
    wide = DIFF_HEADS_PER_STEP * dv
    assert qb % wide == 0 and kb % wide == 0 and vb % wide == 0 and B_HEADS % DIFF_HEADS_PER_STEP == 0
    vec = pl.BlockSpec((1, HEAD_DIM), lambda bi, h, qi: (0, 0))

    def cols(n, base):
        return pl.BlockSpec((1, n, wide), lambda bi, h, qi: (bi, 0, base // wide + h))

    in_specs = [vec, vec, vec, vec, pl.BlockSpec((1, dv), lambda bi, h, qi: (0, 0)),
                pl.BlockSpec((1, tq, wide), lambda bi, h, qi: (bi, qi, qb // wide + h)),
                cols(c, kb), cols(c, vb)]
    args = [v.reshape(1, HEAD_DIM) for v in lam_vecs] + [subln.reshape(1, dv), p_q, p_ctx, p_ctx]
    seq = 0
    if has_lat:
        seq = p_lat.shape[1]
        in_specs += [cols(seq, kb), cols(seq, vb)]
        args += [p_lat, p_lat]
    return pl.pallas_call(
        functools.partial(_diff_kernel, has_lat=has_lat, lam_init=lam_init),
        grid=(b, B_HEADS // DIFF_HEADS_PER_STEP, nq // tq),
        in_specs=in_specs,
        out_specs=pl.BlockSpec((1, tq, wide), lambda bi, h, qi: (bi, qi, h)),
        out_shape=jax.ShapeDtypeStruct((b, nq, B_HEADS * dv), BF16),
        scratch_shapes=[pltpu.VMEM((2 * DIFF_HEADS_PER_STEP, tq, c + seq), F32)],
        compiler_params=_params(("arbitrary", "arbitrary", "arbitrary")),
        name="diff_attention",
    )(*args)


NA_QROWS = 4
NA_WROWS = NA_QROWS + NA_ROWS
NA_BLOCKS_PER_STEP = 2


def _natten_kernel(q_ref, kx_ref, vx_ref, kl_ref, vl_ref, tb_ref, o_ref, s_ref, *, n_rows):
    n_blocks = n_rows // NA_QROWS
    tq = NA_QROWS * GRID_W
    n_ctx = kx_ref.shape[1]
    chains = []
    for i in range(NA_BLOCKS_PER_STEP):
        blk = pl.program_id(2) * NA_BLOCKS_PER_STEP + i
        first_row = jnp.clip(blk * NA_QROWS - NA_ROWS // 2, 0, n_rows - NA_WROWS)
        start = pl.multiple_of(first_row * GRID_W, GRID_W)
        variant = jnp.where(blk == 0, 0, jnp.where(blk == n_blocks - 1, 2, 1))

        def bias(j, variant=variant):
            return tb_ref[0, variant, :, j * KEY_CHUNK:(j + 1) * KEY_CHUNK]

        chunks = (_ref_chunks(kx_ref, vx_ref, 0, n_ctx)
                  + _ref_chunks(kl_ref, vl_ref, start, NA_WROWS * GRID_W, bias))
        chains.append(_SoftmaxChain(q_ref[0, i * tq:(i + 1) * tq, :], chunks, s_ref.at[i], HEAD_DIM))
    _run_chains(chains)
    for i, chain in enumerate(chains):
        acc, den = chain.finish()
        o_ref[0, i * tq:(i + 1) * tq, :] = (acc / den).astype(BF16)


def natten_bias_slabs(rpb):
    col = jnp.arange(GRID_W)
    cs = jnp.clip(col - NA_COLS // 2, 0, GRID_W - NA_COLS)
    col_mask = (col[None, :] >= cs[:, None]) & (col[None, :] < cs[:, None] + NA_COLS)
    col_idx = jnp.clip(col[None, :] - col[:, None], -(NA_COLS - 1), NA_COLS - 1) + NA_COLS - 1
    rpb_c = jnp.where(col_mask[None, None], rpb[:, :, col_idx], NEG_INF)
    i = jnp.arange(NA_QROWS)[:, None]
    a = jnp.arange(NA_WROWS)[None, :]
    half = NA_ROWS // 2
    kinds = [(a < NA_ROWS, a - i + NA_ROWS - 1),
             ((a >= i) & (a < i + NA_ROWS), a - i - half + NA_ROWS - 1),
             (a >= NA_WROWS - NA_ROWS, a - i - NA_ROWS + NA_ROWS - 1)]
    slabs = []
    for valid, roff in kinds:
        t = rpb_c[:, jnp.clip(roff, 0, 2 * NA_ROWS - 2)] * LOG2E
        t = jnp.where(valid[None, :, :, None, None], t, NEG_INF)
        slabs.append(jnp.transpose(t, (0, 1, 3, 2, 4)).reshape(rpb.shape[0], NA_QROWS * GRID_W, NA_WROWS * GRID_W))
    return jnp.stack(slabs, axis=1).astype(F32)


def natten_attention(p_lat, p_ctx, slabs):
    b, seq, _ = p_lat.shape
    c = p_ctx.shape[1]
    n_rows = seq // GRID_W
    rows_per_step = NA_QROWS * NA_BLOCKS_PER_STEP
    assert n_rows >= NA_WROWS and n_rows % rows_per_step == 0
    qd = (C_HEADS + 2 * C_KV_HEADS) * HEAD_DIM
    kd = qd + D_HEADS * HEAD_DIM
    vd = kd + D_HEADS * HEAD_DIM
    tq = rows_per_step * GRID_W

    def col(n, base):
        return pl.BlockSpec((1, n, HEAD_DIM), lambda bi, h, i: (bi, 0, base // HEAD_DIM + h))

    in_specs = [pl.BlockSpec((1, tq, HEAD_DIM), lambda bi, h, i: (bi, i, qd // HEAD_DIM + h)),
                col(c, kd), col(c, vd), col(seq, kd), col(seq, vd),
                pl.BlockSpec((1, 3, NA_QROWS * GRID_W, NA_WROWS * GRID_W), lambda bi, h, i: (h, 0, 0, 0))]
    return pl.pallas_call(
        functools.partial(_natten_kernel, n_rows=n_rows),
        grid=(b, D_HEADS, n_rows // rows_per_step),
        in_specs=in_specs,
        out_specs=pl.BlockSpec((1, tq, HEAD_DIM), lambda bi, h, i: (bi, i, h)),
        out_shape=jax.ShapeDtypeStruct((b, seq, D_HEADS * HEAD_DIM), BF16),
        scratch_shapes=[pltpu.VMEM((NA_BLOCKS_PER_STEP, NA_QROWS * GRID_W, c + NA_WROWS * GRID_W), F32)],
        compiler_params=_params(("arbitrary", "arbitrary", "arbitrary")),
        name="natten_attention",
    )(p_lat, p_ctx, p_ctx, p_lat, p_lat, slabs)


def _out_proj_kernel(o1_ref, o2_ref, w_ref, x_ref, gt_ref, g_ref, sc_ref, sh_ref, wr_ref, xo_ref, h_ref, aff_ref,
                     *, n_experts):
    y = (jnp.dot(o1_ref[0], w_ref[0], preferred_element_type=F32)
         + jnp.dot(o2_ref[0], w_ref[1], preferred_element_type=F32))
    x = x_ref[0] + gt_ref[0] * y
    xo_ref[0] = x
    ms = jnp.mean(x * x, axis=-1, keepdims=True)
    h = x * lax.rsqrt(ms + EPS) * g_ref[...]
    h = h * (1.0 + sc_ref[0]) + sh_ref[0]
    h_ref[0] = h
    logits = jnp.dot(h.astype(BF16), wr_ref[...], preferred_element_type=F32)
    lane = lax.broadcasted_iota(jnp.int32, logits.shape, 1)
    logits = jnp.where(lane < n_experts, logits, NEG_INF)
    e = jnp.exp(logits - jnp.max(logits, axis=-1, keepdims=True))
    aff_ref[0] = e / jnp.sum(e, axis=-1, keepdims=True)


def out_projection(o1, o2, w_bf16, x, gate1, g_ffn, scale2, shift2, w_router_pad, n_experts, tm):
    b, n, d = x.shape
    k = o1.shape[2]
    assert o2.shape[2] == k and w_bf16.shape == (2, k, d)
    per_batch = gate1.shape[0] > 1
    mod_map = (lambda bi, i: (bi, 0, 0)) if per_batch else (lambda bi, i: (0, 0, 0))
    row = lambda bi, i: (bi, i, 0)
    return pl.pallas_call(
        functools.partial(_out_proj_kernel, n_experts=n_experts),
        grid=(b, n // tm),
        in_specs=[pl.BlockSpec((1, tm, k), row),
                  pl.BlockSpec((1, tm, k), row),
                  pl.BlockSpec((2, k, d), lambda bi, i: (0, 0, 0)),
                  pl.BlockSpec((1, tm, d), row),
                  pl.BlockSpec((1, 1, d), mod_map),
                  pl.BlockSpec((1, d), lambda bi, i: (0, 0)),
                  pl.BlockSpec((1, 1, d), mod_map),
                  pl.BlockSpec((1, 1, d), mod_map),
                  pl.BlockSpec((d, ROUTER_LANES), lambda bi, i: (0, 0))],
        out_specs=[pl.BlockSpec((1, tm, d), row), pl.BlockSpec((1, tm, d), row),
                   pl.BlockSpec((1, tm, ROUTER_LANES), row)],
        out_shape=[jax.ShapeDtypeStruct((b, n, d), F32), jax.ShapeDtypeStruct((b, n, d), F32),
                   jax.ShapeDtypeStruct((b, n, ROUTER_LANES), F32)],
        compiler_params=_params(("arbitrary", "arbitrary")),
        name="out_projection",
    )(o1, o2, w_bf16, x, gate1, g_ffn.reshape(1, d), scale2, shift2, w_router_pad)


MOE_ROW_CHUNK = 1024


SUBLANES = 8


def _split_rows(idx):
    b, e, cap = idx.shape
    return (idx // SUBLANES).reshape(b, e, 1, cap), (idx % SUBLANES).reshape(b, e, 1, cap)


def _gather_kernel(hi_ref, lo_ref, *refs, cap, dc):
    h_refs, o_ref, stage_ref = refs[:-2], refs[-2], refs[-1]

    def body(i, carry):
        for r in range(SUBLANES):
            c = i * SUBLANES + r
            hi, lo = hi_ref[0, 0, 0, c], lo_ref[0, 0, 0, c]
            for k, h_ref in enumerate(h_refs):
                stage_ref[i, r:r + 1, k * dc:(k + 1) * dc] = h_ref[0, hi, pl.ds(lo, 1), :]
        return carry
    lax.fori_loop(0, cap // SUBLANES, body, 0)
    o_ref[0, 0] = stage_ref[...].reshape(cap, stage_ref.shape[2]).astype(BF16)


def moe_gather(h, idx, dc):
    b, n, d = h.shape
    _, e, cap = idx.shape
    assert n % SUBLANES == 0 and cap % SUBLANES == 0
    n_chunks = d // dc
    h_specs = [pl.BlockSpec((1, n // SUBLANES, SUBLANES, dc), lambda bi, ei, k=k: (bi, 0, 0, k),
                            pipeline_mode=pl.Buffered(1)) for k in range(n_chunks)]
    smem = pl.BlockSpec((1, 1, 1, cap), lambda bi, ei: (bi, ei, 0, 0), memory_space=pltpu.SMEM)
    h_tiles = h.reshape(b, n // SUBLANES, SUBLANES, d)
    return pl.pallas_call(
        functools.partial(_gather_kernel, cap=cap, dc=dc),
        grid=(b, e),
        in_specs=[smem, smem] + h_specs,
        out_specs=pl.BlockSpec((1, 1, cap, d), lambda bi, ei: (bi, ei, 0, 0)),
        out_shape=jax.ShapeDtypeStruct((b, e, cap, d), BF16),
        scratch_shapes=[pltpu.VMEM((cap // SUBLANES, SUBLANES, d), F32)],
        compiler_params=_params(("arbitrary", "arbitrary")),
        name="moe_gather",
    )(*_split_rows(idx), *([h_tiles] * n_chunks))


EXPERT_F_CHUNK = 256


def _expert_kernel(*refs, n_chunks, step_ranges):
    n_groups = len(step_ranges)
    in_refs = refs[:2 * n_groups]
    wg_ref, wu_ref, wd_ref = refs[2 * n_groups:2 * n_groups + 3]
    out_refs = refs[2 * n_groups + 3:3 * n_groups + 3]
    wg_s, wu_s, wd_s = refs[-3:]
    step = pl.program_id(1)

    @pl.when(step < n_chunks)
    def _():
        wg_s[step] = wg_ref[0, 0].astype(BF16)
        wu_s[step] = wu_ref[0, 0].astype(BF16)
        wd_s[step] = wd_ref[0, 0].astype(BF16)

    for g, (lo, hi) in enumerate(step_ranges):
        xe_ref, gate_ref, o_ref = in_refs[2 * g], in_refs[2 * g + 1], out_refs[g]

        @pl.when((step >= lo) & (step < hi))
        def _(xe_ref=xe_ref, gate_ref=gate_ref, o_ref=o_ref):
            bs, _, cap, d = xe_ref.shape
            xe = xe_ref[...].reshape(bs * cap, d)
            ye = jnp.zeros((bs * cap, d), F32)
            for k in range(n_chunks):
                a = jnp.dot(xe, wg_s[k], preferred_element_type=F32)
                u = jnp.dot(xe, wu_s[k], preferred_element_type=F32)
                hid = (a * (1.0 / (1.0 + jnp.exp(-a))) * u).astype(BF16)
                ye = ye + jnp.dot(hid, wd_s[k], preferred_element_type=F32)
            o_ref[...] = (ye * gate_ref[...].reshape(bs * cap, 1)).reshape(bs, 1, cap, d)


def moe_experts(groups, wg, wu, wd, layer):
    _, e, d, f = wg.shape
    fc = math.gcd(f, EXPERT_F_CHUNK)
    n_chunks = f // fc
    in_specs, args, out_specs, out_shapes, step_ranges = [], [], [], [], []
    lo = n_chunks
    for xe, gate in groups:
        b, _, cap, _ = xe.shape
        bs = max(1, min(b, 512 // cap))
        assert b % bs == 0
        nb = b // bs
        tile = lambda ei, s, lo=lo, nb=nb: (jnp.clip(s - lo, 0, nb - 1), ei, 0, 0)
        in_specs += [pl.BlockSpec((bs, 1, cap, d), tile), pl.BlockSpec((bs, 1, cap, 1), tile)]
        args += [xe, gate]
        out_specs.append(pl.BlockSpec((bs, 1, cap, d), tile))
        out_shapes.append(jax.ShapeDtypeStruct((b, e, cap, d), F32))
        step_ranges.append((lo, lo + nb))
        lo += nb
    chunk = lambda ei, s: jnp.minimum(s, n_chunks - 1)
    in_specs += [pl.BlockSpec((1, 1, d, fc), lambda ei, s: (layer, ei, 0, chunk(ei, s))),
                 pl.BlockSpec((1, 1, d, fc), lambda ei, s: (layer, ei, 0, chunk(ei, s))),
                 pl.BlockSpec((1, 1, fc, d), lambda ei, s: (layer, ei, chunk(ei, s), 0))]
    return pl.pallas_call(
        functools.partial(_expert_kernel, n_chunks=n_chunks, step_ranges=tuple(step_ranges)),
        grid=(e, lo),
        in_specs=in_specs,
        out_specs=out_specs,
        out_shape=out_shapes,
        scratch_shapes=[pltpu.VMEM((n_chunks, d, fc), BF16), pltpu.VMEM((n_chunks, d, fc), BF16),
                        pltpu.VMEM((n_chunks, fc, d), BF16)],
        compiler_params=_params(("arbitrary", "arbitrary")),
        name="moe_experts",
    )(*args, wg, wu, wd)


def _combine_kernel(hi_ref, lo_ref, ye_ref, x_ref, gt_ref, *refs, cap, n_experts, rows_out, final_norm):
    gf_ref = refs[0] if final_norm else None
    o_ref, acc_ref = refs[-2:]
    step = pl.program_id(1)

    @pl.when(step == 0)
    def _():
        acc_ref[...] = jnp.zeros_like(acc_ref)

    @pl.when(step < n_experts)
    def _():
        def body(i, carry):
            rows = [(hi_ref[0, 0, 0, i * SUBLANES + r], lo_ref[0, 0, 0, i * SUBLANES + r])
                    for r in range(SUBLANES)]
            old = [acc_ref[hi, pl.ds(lo, 1), :] for hi, lo in rows]
            for r, (hi, lo) in enumerate(rows):
                acc_ref[hi, pl.ds(lo, 1), :] = old[r] + ye_ref[0, 0, i, r:r + 1, :]
            return carry
        lax.fori_loop(0, cap // SUBLANES, body, 0)

    @pl.when(step >= n_experts)
    def _():
        tiles_out = rows_out // SUBLANES
        acc = acc_ref[pl.ds((step - n_experts) * tiles_out, tiles_out)]
        x = x_ref[0] + gt_ref[0] * acc.reshape(rows_out, acc.shape[2])
        if final_norm:
            ms = jnp.mean(x * x, axis=-1, keepdims=True)
            x = x * lax.rsqrt(ms + EPS) * gf_ref[...]
        o_ref[0] = x


def moe_combine(ye, idx, x, gate2, final_gain=None):
    b, n, d = x.shape
    final_norm = final_gain is not None
    extra_specs = [pl.BlockSpec((1, d), lambda bi, s: (0, 0))] if final_norm else []
    extra_args = [final_gain.reshape(1, d)] if final_norm else []
    _, e, cap = idx.shape
    rows_out = min(n, 256)
    per_batch = gate2.shape[0] > 1
    mod_map = (lambda bi, s: (bi, 0, 0)) if per_batch else (lambda bi, s: (0, 0, 0))
    assert n % SUBLANES == 0 and cap % SUBLANES == 0 and rows_out % SUBLANES == 0
    expert = lambda bi, s: (bi, jnp.minimum(s, e - 1), 0, 0)
    expert_rows = lambda bi, s: (bi, jnp.minimum(s, e - 1), 0, 0, 0)
    tile = lambda bi, s: (bi, jnp.maximum(s - e, 0), 0)
    smem = pl.BlockSpec((1, 1, 1, cap), expert, memory_space=pltpu.SMEM)
    return pl.pallas_call(
        functools.partial(_combine_kernel, cap=cap, n_experts=e, rows_out=rows_out, final_norm=final_norm),
        grid=(b, e + n // rows_out),
        in_specs=[smem, smem,
                  pl.BlockSpec((1, 1, cap // SUBLANES, SUBLANES, d), expert_rows),
                  pl.BlockSpec((1, rows_out, d), tile),
                  pl.BlockSpec((1, 1, d), mod_map)] + extra_specs,
        out_specs=pl.BlockSpec((1, rows_out, d), tile),
        out_shape=jax.ShapeDtypeStruct((b, n, d), F32),
        scratch_shapes=[pltpu.VMEM((n // SUBLANES, SUBLANES, d), F32)],
        compiler_params=_params(("arbitrary", "arbitrary")),
        name="moe_combine",
    )(*_split_rows(idx), ye.reshape(b, e, cap // SUBLANES, SUBLANES, d), x, gate2, *extra_args)


def moe_dispatch(h, aff, n_experts):
    b, n, d = h.shape
    cap = max(1, EC_CAPACITY_FACTOR * n // n_experts)
    gate, idx = lax.top_k(jnp.swapaxes(aff[..., :n_experts], 1, 2), cap)
    return idx, gate[..., None], moe_gather(h, idx, min(d, MOE_ROW_CHUNK))


def _rope_tables(n):
    t = jnp.arange(n, dtype=jnp.int32)
    row = (t // GRID_W).astype(F32)
    col = (t % GRID_W).astype(F32)
    n_freq = HEAD_DIM // 4
    inv = ROPE_THETA ** (-jnp.arange(n_freq, dtype=F32) / n_freq)
    ar = row[:, None] * inv
    ac = col[:, None] * inv
    ang = jnp.concatenate([ar, ar, ac, ac], axis=-1)
    sign = jnp.where((jnp.arange(HEAD_DIM) % (2 * n_freq)) < n_freq, -1.0, 1.0).astype(F32)
    return jnp.cos(ang), jnp.sin(ang) * sign


def _tile_codes(layer, rope):
    if layer % 2 == 0:
        groups = [(A_HEADS * HEAD_DIM, ROPE_BIT | SCALE_BIT), (A_KV_HEADS * HEAD_DIM, ROPE_BIT),
                  (A_KV_HEADS * HEAD_DIM, 0), (B_HEADS * 2 * HEAD_DIM, ROPE_BIT | SCALE_BIT),
                  (B_HEADS * 2 * HEAD_DIM, ROPE_BIT), (B_HEADS * 2 * HEAD_DIM, 0)]
    else:
        groups = [(C_HEADS * HEAD_DIM, QNORM_BIT | ROPE_BIT | SCALE_BIT), (C_KV_HEADS * HEAD_DIM, KNORM_BIT | ROPE_BIT),
                  (C_KV_HEADS * HEAD_DIM, 0), (D_HEADS * HEAD_DIM, SCALE_BIT), (D_HEADS * HEAD_DIM, 0),
                  (D_HEADS * HEAD_DIM, 0)]
    codes = []
    for width, code in groups:
        assert width % IN_TN == 0
        codes += [code if rope else code & ~ROPE_BIT] * (width // IN_TN)
    return codes


def _lambda_init(layer):
    return 0.8 - 0.6 * math.exp(-0.3 * layer)


def _row_tile(n, target):
    t = min(n, target)
    assert n % t == 0
    return t


def kernel(x, c, ctx, c_ctx, w_ada, b_ada, g_mix, g_ffn, w_in, w_out, a_sink, b_lam_q1, b_lam_k1, b_lam_q2,
           b_lam_k2, b_subln, c_q_norm, c_k_norm, d_rpb, w_router, w_gate, w_up, w_down, g_final):
    bsz, seq, d = x.shape
    n_ctx = ctx.shape[1]
    depth = w_ada.shape[0]
    n_experts = w_router.shape[2]
    cos, sin_signed = _rope_tables(seq)
    ones_head = jnp.ones((1, HEAD_DIM), F32)

    n_rows = -(-(bsz + 1) // 8) * 8
    c_all = jnp.zeros((n_rows, d), F32).at[:bsz].set(c).at[bsz].set(c_ctx)
    mod = ada_modulation(c_all, w_ada, b_ada)

    xc = ctx
    for l in range(depth):
        need_ctx = l < depth - 1
        e = l // 2
        m_lat = mod[l, :bsz].reshape(bsz, 1, 6, d)
        m_ctx = mod[l, bsz:bsz + 1].reshape(1, 1, 6, d)
        sh1, sc1, gt1, sh2, sc2, gt2 = (m_lat[:, :, i] for i in range(6))
        sh1x, sc1x, gt1x, sh2x, sc2x, gt2x = (m_ctx[:, :, i] for i in range(6))
        w_in_l = w_in[l].astype(BF16)
        w_out_l = w_out[l].astype(BF16).reshape(2, MIX_WIDTH // 2, d)
        if l % 2 == 0:
            qn = kn = ones_head
        else:
            qn, kn = c_q_norm[e].reshape(1, HEAD_DIM), c_k_norm[e].reshape(1, HEAD_DIM)
        tm_lat = _row_tile(seq, 512)
        tm_ctx = _row_tile(n_ctx, 512)
        p = in_projection(x, g_mix[l], sc1, sh1, w_in_l, _tile_codes(l, True), cos, sin_signed, qn, kn, tm_lat)
        px = in_projection(xc, g_mix[l], sc1x, sh1x, w_in_l, _tile_codes(l, False), cos[:tm_ctx],
                           sin_signed[:tm_ctx], qn, kn, tm_ctx)
        ox1 = ox2 = None
        if l % 2 == 0:
            lam_vecs = (b_lam_q1[e], b_lam_k1[e], b_lam_q2[e], b_lam_k2[e])
            ka = A_HEADS * HEAD_DIM
            va = ka + A_KV_HEADS * HEAD_DIM
            a_cfg = dict(kv_heads=A_KV_HEADS, rep=A_HEADS // A_KV_HEADS, q_col=0, k_col=ka, v_col=va,
                         sink=a_sink[e])
            o1 = gqa_attention(p, px, p, tq=_row_tile(seq, 256), mode='band', **a_cfg)
            o2 = diff_attention(p, px, p, lam_vecs, b_subln[e], lam_init=_lambda_init(l),
                                tq=_row_tile(seq, 256), has_lat=True)
            if need_ctx:
                ox1 = gqa_attention(px, px, None, tq=_row_tile(n_ctx, 256), mode='none', **a_cfg)
                ox2 = diff_attention(px, px, None, lam_vecs, b_subln[e], lam_init=_lambda_init(l),
                                     tq=_row_tile(n_ctx, 256), has_lat=False)
        else:
            kc = C_HEADS * HEAD_DIM
            vc = kc + C_KV_HEADS * HEAD_DIM
            c_cfg = dict(kv_heads=C_KV_HEADS, rep=C_HEADS // C_KV_HEADS, q_col=0, k_col=kc, v_col=vc)
            o1 = gqa_attention(p, px, p, tq=_row_tile(seq, 256), mode='full', **c_cfg)
            o2 = natten_attention(p, px, natten_bias_slabs(d_rpb[e]))
            if need_ctx:
                qd = vc + C_KV_HEADS * HEAD_DIM
                kd = qd + D_HEADS * HEAD_DIM
                vd = kd + D_HEADS * HEAD_DIM
                ox1 = gqa_attention(px, px, None, tq=_row_tile(n_ctx, 256), mode='none', **c_cfg)
                ox2 = gqa_attention(px, px, None, kv_heads=D_HEADS, rep=1, q_col=qd, k_col=kd, v_col=vd,
                                    tq=_row_tile(n_ctx, 256), mode='none')
        w_router_pad = jnp.zeros((d, ROUTER_LANES), BF16).at[:, :n_experts].set(w_router[l].astype(BF16))
        x1, h, aff = out_projection(o1, o2, w_out_l, x, gt1, g_ffn[l], sc2, sh2, w_router_pad, n_experts,
                                    _row_tile(seq, 256))
        idx, gate, xe = moe_dispatch(h, aff, n_experts)
        groups = [(xe, gate)]
        if need_ctx:
            xc1, hx, affx = out_projection(ox1, ox2, w_out_l, xc, gt1x, g_ffn[l], sc2x, sh2x, w_router_pad,
                                           n_experts, _row_tile(n_ctx, 256))
            idx_x, gate_x, xe_x = moe_dispatch(hx, affx, n_experts)
            groups.append((xe_x, gate_x))
        ye = moe_experts(groups, w_gate, w_up, w_down, l)
        x = moe_combine(ye[0], idx, x1, gt2, g_final if l == depth - 1 else None)
        if need_ctx:
            xc = moe_combine(ye[1], idx_x, xc1, gt2x)
    return x
```

```python
import functools
import math

import jax
import jax.numpy as jnp
from jax import lax
from jax.experimental import pallas as pl
from jax.experimental.pallas import tpu as pltpu

F32 = jnp.float32
BF16 = jnp.bfloat16

HEAD_DIM = 128
GRID_W = 64
ROPE_THETA = 10000.0
EPS = 1e-6
NEG_INF = -1e30
LOG2E = math.log2(math.e)
A_HEADS, A_KV_HEADS, A_WINDOW = 8, 2, 128
B_HEADS = 4
C_HEADS, C_KV_HEADS = 8, 2
D_HEADS, NA_ROWS, NA_COLS = 8, 8, 16
N_EXPERTS = 16
EC_CAPACITY_FACTOR = 2
IN_WIDTH = 4608
MIX_WIDTH = 2048
ROUTER_LANES = 128

ROPE_BIT, QNORM_BIT, KNORM_BIT, SCALE_BIT = 1, 2, 4, 8
IN_TN = 256

V7X_VMEM_LIMIT = 56 * 1024 * 1024
V7X_VMEM_LIMIT_LARGE = 60 * 1024 * 1024


def _params(sem, vmem=V7X_VMEM_LIMIT):
    return pltpu.CompilerParams(dimension_semantics=sem, vmem_limit_bytes=vmem)


def _nt_dot(a, b):
    return lax.dot_general(a, b, (((1,), (1,)), ((), ())), preferred_element_type=F32)


def _ada_kernel(c_ref, w_ref, b_ref, o_ref):
    c = c_ref[...]
    cond = c * (1.0 / (1.0 + jnp.exp(-c)))
    o_ref[0] = jnp.dot(cond.astype(BF16), w_ref[0].astype(BF16), preferred_element_type=F32) + b_ref[0]


def ada_modulation(c_all, w_ada, b_ada):
    depth, d, n = w_ada.shape
    r = c_all.shape[0]
    tn = math.gcd(n, 1024)
    return pl.pallas_call(
        _ada_kernel,
        grid=(depth, n // tn),
        in_specs=[pl.BlockSpec((r, d), lambda l, j: (0, 0)),
                  pl.BlockSpec((1, d, tn), lambda l, j: (l, 0, j)),
                  pl.BlockSpec((1, 1, tn), lambda l, j: (l, 0, j))],
        out_specs=pl.BlockSpec((1, r, tn), lambda l, j: (l, 0, j)),
        out_shape=jax.ShapeDtypeStruct((depth, r, n), F32),
        compiler_params=_params(("arbitrary", "arbitrary")),
        name="ada_modulation",
    )(c_all, w_ada, b_ada.reshape(depth, 1, n))


def _rope_rotate(y, sin_signed):
    lane = lax.broadcasted_iota(jnp.int32, y.shape, 1)
    first = (lane & 63) < 32
    return jnp.where(first, pltpu.roll(y, 96, 1), pltpu.roll(y, 32, 1)) * sin_signed


def _in_proj_kernel(x_ref, g_ref, sc_ref, sh_ref, w_ref, cos_ref, sin_ref, qn_ref, kn_ref, o_ref, xn_ref,
                    *, codes, scale):
    x = x_ref[0]
    ms = jnp.mean(x * x, axis=-1, keepdims=True)
    y = x * lax.rsqrt(ms + EPS) * g_ref[...]
    xn_ref[...] = (y * (1.0 + sc_ref[0]) + sh_ref[0]).astype(BF16)
    for j, code in enumerate(codes):
        acc = jnp.dot(xn_ref[...], w_ref[:, j * IN_TN:(j + 1) * IN_TN], preferred_element_type=F32)
        for h in range(IN_TN // HEAD_DIM):
            y = acc[:, h * HEAD_DIM:(h + 1) * HEAD_DIM]
            if code & (QNORM_BIT | KNORM_BIT):
                gn_ref = qn_ref if code & QNORM_BIT else kn_ref
                ms = jnp.mean(y * y, axis=-1, keepdims=True)
                y = y * lax.rsqrt(ms + EPS) * gn_ref[...]
            if code & ROPE_BIT:
                y = y * cos_ref[...] + _rope_rotate(y, sin_ref[...])
            if code & SCALE_BIT:
                y = y * scale
            lo = j * IN_TN + h * HEAD_DIM
            o_ref[0, :, lo:lo + HEAD_DIM] = y.astype(BF16)


def in_projection(x, g, scale1, shift1, w_bf16, codes, cos, sin_signed, qn, kn, tm):
    b, n, d = x.shape
    width = w_bf16.shape[1]
    assert len(codes) * IN_TN == width
    per_batch = scale1.shape[0] > 1
    mod_map = (lambda bi, i: (bi, 0, 0)) if per_batch else (lambda bi, i: (0, 0, 0))
    const = lambda bi, i: (0, 0)
    return pl.pallas_call(
        functools.partial(_in_proj_kernel, codes=tuple(codes), scale=HEAD_DIM ** -0.5 * LOG2E),
        grid=(b, n // tm),
        in_specs=[pl.BlockSpec((1, tm, d), lambda bi, i: (bi, i, 0)),
                  pl.BlockSpec((1, d), const),
                  pl.BlockSpec((1, 1, d), mod_map),
                  pl.BlockSpec((1, 1, d), mod_map),
                  pl.BlockSpec((d, width), const, pipeline_mode=pl.Buffered(1)),
                  pl.BlockSpec((tm, HEAD_DIM), lambda bi, i: (i, 0)),
                  pl.BlockSpec((tm, HEAD_DIM), lambda bi, i: (i, 0)),
                  pl.BlockSpec((1, HEAD_DIM), const),
                  pl.BlockSpec((1, HEAD_DIM), const)],
        out_specs=pl.BlockSpec((1, tm, width), lambda bi, i: (bi, i, 0)),
        out_shape=jax.ShapeDtypeStruct((b, n, width), BF16),
        scratch_shapes=[pltpu.VMEM((tm, d), BF16)],
        compiler_params=_params(("arbitrary", "arbitrary")),
        name="in_projection",
    )(x, g.reshape(1, d), scale1, shift1, w_bf16, cos, sin_signed, qn, kn)


KEY_CHUNK = 256

class _SoftmaxChain:
    def __init__(self, q, chunks, s_ref, dv, sink=None):
        self.q, self.chunks, self.s_ref, self.sink = q, chunks, s_ref, sink
        m = q.shape[0]
        self.ones_col = dv == HEAD_DIM
        self.dv = dv
        self.mx = jnp.full((m, HEAD_DIM), NEG_INF, F32)
        self.ls = jnp.zeros((m, HEAD_DIM), F32)
        self.acc = jnp.zeros((m, 2 * HEAD_DIM if self.ones_col else dv), F32)

    def score_steps(self):
        return [functools.partial(self._score, j) for j in range(len(self.chunks))]

    def value_steps(self):
        return [functools.partial(self._value, j) for j in range(len(self.chunks))]

    def _score(self, j):
        k_fn, _, bias_fn = self.chunks[j]
        s = _nt_dot(self.q, k_fn())
        if bias_fn is not None:
            s = s + bias_fn()
        self.s_ref[:, j * KEY_CHUNK:(j + 1) * KEY_CHUNK] = s
        for t in range(KEY_CHUNK // HEAD_DIM):
            self.mx = jnp.maximum(self.mx, s[:, t * HEAD_DIM:(t + 1) * HEAD_DIM])

    def finish_scores(self):
        m = jnp.max(self.mx, axis=-1, keepdims=True)
        if self.sink is not None:
            m = jnp.maximum(m, self.sink)
        self.m = m
        self.m_wide = jnp.broadcast_to(m, self.mx.shape)

    def _value(self, j):
        _, v_fn, _ = self.chunks[j]
        parts = []
        for t in range(KEY_CHUNK // HEAD_DIM):
            lo = j * KEY_CHUNK + t * HEAD_DIM
            e = jnp.exp2(self.s_ref[:, lo:lo + HEAD_DIM] - self.m_wide)
            if not self.ones_col:
                self.ls = self.ls + e
            parts.append(e.astype(BF16))
        p = jnp.concatenate(parts, axis=1)
        v = v_fn()
        if self.ones_col:
            v = jnp.concatenate([v, jnp.ones(v.shape, BF16)], axis=1)
        self.acc = self.acc + jnp.dot(p, v, preferred_element_type=F32)

    def result(self):
        if self.ones_col:
            num, den = self.acc[:, :self.dv], self.acc[:, self.dv:]
        else:
            num, den = self.acc, jnp.sum(self.ls, axis=-1, keepdims=True)
        if self.sink is not None:
            den = den + jnp.exp2(self.sink - self.m)
        return num / den


def _run_chains(chains):
    prev = None
    for chain in chains:
        scores = chain.score_steps()
        values = prev.value_steps() if prev is not None else []
        for i in range(max(len(scores), len(values))):
            if i < len(values):
                values[i]()
            if i < len(scores):
                scores[i]()
        chain.finish_scores()
        prev = chain
    for step in prev.value_steps():
        step()


def _ref_chunks(k_ref, v_ref, start, length, bias=None, k_cols=slice(None), v_cols=slice(None)):
    assert length % KEY_CHUNK == 0
    out = []
    for j in range(length // KEY_CHUNK):
        if isinstance(start, int):
            rows = slice(start + j * KEY_CHUNK, start + (j + 1) * KEY_CHUNK)
        else:
            rows = pl.ds(start + j * KEY_CHUNK, KEY_CHUNK)
        out.append((lambda rows=rows: k_ref[0, rows, k_cols], lambda rows=rows: v_ref[0, rows, v_cols],
                    (lambda j=j: bias(j)) if bias is not None else None))
    return out


def _gqa_kernel(*refs, rep, tq, mode, seq, use_sink):
    if use_sink:
        sink_ref, refs = refs[0], refs[1:]
    s_ref = refs[-1]
    refs = refs[:-1]
    if mode == 'none':
        q_ref, kx_ref, vx_ref, o_ref = refs
    else:
        q_ref, kx_ref, vx_ref, kl_ref, vl_ref, o_ref = refs
    g = pl.program_id(1)
    qi = pl.program_id(2)
    chunks = _ref_chunks(kx_ref, vx_ref, 0, kx_ref.shape[1])
    if mode == 'full':
        chunks += _ref_chunks(kl_ref, vl_ref, 0, seq)
    elif mode == 'band':
        band = tq + 2 * A_WINDOW
        start = pl.multiple_of(jnp.clip(qi * tq - A_WINDOW, 0, seq - band), HEAD_DIM)
        rel = (lax.broadcasted_iota(jnp.int32, (tq, KEY_CHUNK), 0)
               - lax.broadcasted_iota(jnp.int32, (tq, KEY_CHUNK), 1))

        def band_bias(j):
            dist = rel + (qi * tq - start - j * KEY_CHUNK)
            return jnp.where(jnp.abs(dist) <= A_WINDOW, 0.0, NEG_INF)

        chunks += _ref_chunks(kl_ref, vl_ref, start, band, band_bias)
    q = q_ref[0]
    chains = [_SoftmaxChain(q[:, r * HEAD_DIM:(r + 1) * HEAD_DIM], chunks, s_ref.at[r], HEAD_DIM,
                            sink=sink_ref[g * rep + r] * LOG2E if use_sink else None) for r in range(rep)]
    _run_chains(chains)
    for r, chain in enumerate(chains):
        o_ref[0, :, r * HEAD_DIM:(r + 1) * HEAD_DIM] = chain.result().astype(BF16)


def gqa_attention(p_q, p_ctx, p_lat, *, kv_heads, rep, q_col, k_col, v_col, tq, mode, sink=None):
    b, nq, _ = p_q.shape
    c = p_ctx.shape[1]
    qw = rep * HEAD_DIM
    use_sink = sink is not None
    assert tq & (tq - 1) == 0
    in_specs = [pl.BlockSpec((1, tq, qw), lambda bi, g, qi, *_: (bi, qi, q_col // qw + g)),
                pl.BlockSpec((1, c, HEAD_DIM), lambda bi, g, qi, *_: (bi, 0, k_col // HEAD_DIM + g)),
                pl.BlockSpec((1, c, HEAD_DIM), lambda bi, g, qi, *_: (bi, 0, v_col // HEAD_DIM + g))]
    args = [p_q, p_ctx, p_ctx]
    seq = 0
    if mode != 'none':
        seq = p_lat.shape[1]
        in_specs += [pl.BlockSpec((1, seq, HEAD_DIM), lambda bi, g, qi, *_: (bi, 0, k_col // HEAD_DIM + g)),
                     pl.BlockSpec((1, seq, HEAD_DIM), lambda bi, g, qi, *_: (bi, 0, v_col // HEAD_DIM + g))]
        args += [p_lat, p_lat]
    if use_sink:
        args = [sink] + args
    n_keys = c + {'none': 0, 'full': seq, 'band': tq + 2 * A_WINDOW}[mode]
    grid_spec = pltpu.PrefetchScalarGridSpec(
        num_scalar_prefetch=1 if use_sink else 0,
        grid=(b, kv_heads, nq // tq),
        in_specs=in_specs,
        out_specs=pl.BlockSpec((1, tq, qw), lambda bi, g, qi, *_: (bi, qi, g)),
        scratch_shapes=[pltpu.VMEM((rep, tq, n_keys), F32)])
    return pl.pallas_call(
        functools.partial(_gqa_kernel, rep=rep, tq=tq, mode=mode, seq=seq, use_sink=use_sink),
        grid_spec=grid_spec,
        out_shape=jax.ShapeDtypeStruct((b, nq, kv_heads * qw), BF16),
        compiler_params=_params(("arbitrary", "arbitrary", "arbitrary")),
        name=f"gqa_attention_{mode}",
    )(*args)


def _diff_kernel(*refs, has_lat, lam_init):
    if has_lat:
        lq1, lk1, lq2, lk2, sub_ref, q_ref, kx_ref, vx_ref, kl_ref, vl_ref, o_ref, s_ref = refs
    else:
        lq1, lk1, lq2, lk2, sub_ref, q_ref, kx_ref, vx_ref, o_ref, s_ref = refs
    lam = (jnp.exp(jnp.sum(lq1[...] * lk1[...], axis=-1, keepdims=True))
           - jnp.exp(jnp.sum(lq2[...] * lk2[...], axis=-1, keepdims=True)) + lam_init)
    n_ctx = vx_ref.shape[1]
    dv = 2 * HEAD_DIM
    chains = []
    for h in range(DIFF_HEADS_PER_STEP):
        v_cols = slice(h * dv, (h + 1) * dv)
        for m in range(2):
            k_cols = slice((2 * h + m) * HEAD_DIM, (2 * h + m + 1) * HEAD_DIM)
            chunks = _ref_chunks(kx_ref, vx_ref, 0, n_ctx, k_cols=k_cols, v_cols=v_cols)
            if has_lat:
                chunks += _ref_chunks(kl_ref, vl_ref, 0, vl_ref.shape[1], k_cols=k_cols, v_cols=v_cols)
            chains.append(_SoftmaxChain(q_ref[0, :, k_cols], chunks, s_ref.at[2 * h + m], dv))
    _run_chains(chains)
    for h in range(DIFF_HEADS_PER_STEP):
        o = chains[2 * h].result() - lam * chains[2 * h + 1].result()
        ms = jnp.mean(o * o, axis=-1, keepdims=True)
        o_ref[0, :, h * dv:(h + 1) * dv] = (o * lax.rsqrt(ms + EPS) * sub_ref[...] * (1.0 - lam_init)).astype(BF16)


DIFF_HEADS_PER_STEP = 2


def diff_attention(p_q, p_ctx, p_lat, lam_vecs, subln, *, lam_init, tq, has_lat):
    b, nq, _ = p_q.shape
    c = p_ctx.shape[1]
    qb = (A_HEADS + 2 * A_KV_HEADS) * HEAD_DIM
    kb = qb + B_HEADS * 2 * HEAD_DIM
    vb = kb + B_HEADS * 2 * HEAD_DIM
    dv = 2 * HEAD_DIM
    wide = DIFF_HEADS_PER_STEP * dv
    assert qb % wide == 0 and kb % wide == 0 and vb % wide == 0 and B_HEADS % DIFF_HEADS_PER_STEP == 0
    vec = pl.BlockSpec((1, HEAD_DIM), lambda bi, h, qi: (0, 0))

    def cols(n, base):
        return pl.BlockSpec((1, n, wide), lambda bi, h, qi: (bi, 0, base // wide + h))

    in_specs = [vec, vec, vec, vec, pl.BlockSpec((1, dv), lambda bi, h, qi: (0, 0)),
                pl.BlockSpec((1, tq, wide), lambda bi, h, qi: (bi, qi, qb // wide + h)),
                cols(c, kb), cols(c, vb)]
    args = [v.reshape(1, HEAD_DIM) for v in lam_vecs] + [subln.reshape(1, dv), p_q, p_ctx, p_ctx]
    seq = 0
    if has_lat:
        seq = p_lat.shape[1]
        in_specs += [cols(seq, kb), cols(seq, vb)]
        args += [p_lat, p_lat]
    return pl.pallas_call(
        functools.partial(_diff_kernel, has_lat=has_lat, lam_init=lam_init),
        grid=(b, B_HEADS // DIFF_HEADS_PER_STEP, nq // tq),
        in_specs=in_specs,
        out_specs=pl.BlockSpec((1, tq, wide), lambda bi, h, qi: (bi, qi, h)),
        out_shape=jax.ShapeDtypeStruct((b, nq, B_HEADS * dv), BF16),
        scratch_shapes=[pltpu.VMEM((2 * DIFF_HEADS_PER_STEP, tq, c + seq), F32)],
        compiler_params=_params(("arbitrary", "arbitrary", "arbitrary")),
        name="diff_attention",
    )(*args)


NA_QROWS = 4
NA_WROWS = NA_QROWS + NA_ROWS
NA_BLOCKS_PER_STEP = 4


def _natten_kernel(q_ref, kx_ref, vx_ref, kl_ref, vl_ref, tb_ref, o_ref, s_ref, *, n_rows):
    n_blocks = n_rows // NA_QROWS
    tq = NA_QROWS * GRID_W
    n_ctx = kx_ref.shape[1]
    chains = []
    for i in range(NA_BLOCKS_PER_STEP):
        blk = pl.program_id(2) * NA_BLOCKS_PER_STEP + i
        first_row = jnp.clip(blk * NA_QROWS - NA_ROWS // 2, 0, n_rows - NA_WROWS)
        start = pl.multiple_of(first_row * GRID_W, GRID_W)
        variant = jnp.where(blk == 0, 0, jnp.where(blk == n_blocks - 1, 2, 1))

        def bias(j, variant=variant):
            return tb_ref[0, variant, :, j * KEY_CHUNK:(j + 1) * KEY_CHUNK]

        chunks = (_ref_chunks(kx_ref, vx_ref, 0, n_ctx)
                  + _ref_chunks(kl_ref, vl_ref, start, NA_WROWS * GRID_W, bias))
        chains.append(_SoftmaxChain(q_ref[0, i * tq:(i + 1) * tq, :], chunks, s_ref.at[i], HEAD_DIM))
    _run_chains(chains)
    for i, chain in enumerate(chains):
        o_ref[0, i * tq:(i + 1) * tq, :] = chain.result().astype(BF16)


def natten_bias_slabs(rpb):
    col = jnp.arange(GRID_W)
    cs = jnp.clip(col - NA_COLS // 2, 0, GRID_W - NA_COLS)
    col_mask = (col[None, :] >= cs[:, None]) & (col[None, :] < cs[:, None] + NA_COLS)
    col_idx = jnp.clip(col[None, :] - col[:, None], -(NA_COLS - 1), NA_COLS - 1) + NA_COLS - 1
    rpb_c = jnp.where(col_mask[None, None], rpb[:, :, col_idx], NEG_INF)
    i = jnp.arange(NA_QROWS)[:, None]
    a = jnp.arange(NA_WROWS)[None, :]
    half = NA_ROWS // 2
    kinds = [(a < NA_ROWS, a - i + NA_ROWS - 1),
             ((a >= i) & (a < i + NA_ROWS), a - i - half + NA_ROWS - 1),
             (a >= NA_WROWS - NA_ROWS, a - i - NA_ROWS + NA_ROWS - 1)]
    slabs = []
    for valid, roff in kinds:
        t = rpb_c[:, jnp.clip(roff, 0, 2 * NA_ROWS - 2)] * LOG2E
        t = jnp.where(valid[None, :, :, None, None], t, NEG_INF)
        slabs.append(jnp.transpose(t, (0, 1, 3, 2, 4)).reshape(rpb.shape[0], NA_QROWS * GRID_W, NA_WROWS * GRID_W))
    return jnp.stack(slabs, axis=1).astype(F32)


def natten_attention(p_lat, p_ctx, slabs):
    b, seq, _ = p_lat.shape
    c = p_ctx.shape[1]
    n_rows = seq // GRID_W
    rows_per_step = NA_QROWS * NA_BLOCKS_PER_STEP
    assert n_rows >= NA_WROWS and n_rows % rows_per_step == 0
    qd = (C_HEADS + 2 * C_KV_HEADS) * HEAD_DIM
    kd = qd + D_HEADS * HEAD_DIM
    vd = kd + D_HEADS * HEAD_DIM
    tq = rows_per_step * GRID_W

    def col(n, base):
        return pl.BlockSpec((1, n, HEAD_DIM), lambda bi, h, i: (bi, 0, base // HEAD_DIM + h))

    in_specs = [pl.BlockSpec((1, tq, HEAD_DIM), lambda bi, h, i: (bi, i, qd // HEAD_DIM + h)),
                col(c, kd), col(c, vd), col(seq, kd), col(seq, vd),
                pl.BlockSpec((1, 3, NA_QROWS * GRID_W, NA_WROWS * GRID_W), lambda bi, h, i: (h, 0, 0, 0))]
    return pl.pallas_call(
        functools.partial(_natten_kernel, n_rows=n_rows),
        grid=(b, D_HEADS, n_rows // rows_per_step),
        in_specs=in_specs,
        out_specs=pl.BlockSpec((1, tq, HEAD_DIM), lambda bi, h, i: (bi, i, h)),
        out_shape=jax.ShapeDtypeStruct((b, seq, D_HEADS * HEAD_DIM), BF16),
        scratch_shapes=[pltpu.VMEM((NA_BLOCKS_PER_STEP, NA_QROWS * GRID_W, c + NA_WROWS * GRID_W), F32)],
        compiler_params=_params(("arbitrary", "arbitrary", "arbitrary")),
        name="natten_attention",
    )(p_lat, p_ctx, p_ctx, p_lat, p_lat, slabs)


def _out_proj_kernel(o1_ref, o2_ref, w_ref, x_ref, gt_ref, g_ref, sc_ref, sh_ref, wr_ref, xo_ref, h_ref, aff_ref,
                     *, n_experts, sub_rows):
    for r0 in range(0, x_ref.shape[1], sub_rows):
        rows = slice(r0, r0 + sub_rows)
        y = (jnp.dot(o1_ref[0, rows, :], w_ref[0], preferred_element_type=F32)
             + jnp.dot(o2_ref[0, rows, :], w_ref[1], preferred_element_type=F32))
        x = x_ref[0, rows, :] + gt_ref[0] * y
        xo_ref[0, rows, :] = x
        ms = jnp.mean(x * x, axis=-1, keepdims=True)
        h = x * lax.rsqrt(ms + EPS) * g_ref[...]
        h = h * (1.0 + sc_ref[0]) + sh_ref[0]
        h_ref[0, rows, :] = h
        logits = jnp.dot(h.astype(BF16), wr_ref[...], preferred_element_type=F32)
        lane = lax.broadcasted_iota(jnp.int32, logits.shape, 1)
        logits = jnp.where(lane < n_experts, logits, NEG_INF)
        e = jnp.exp(logits - jnp.max(logits, axis=-1, keepdims=True))
        aff_ref[0, rows, :] = e / jnp.sum(e, axis=-1, keepdims=True)


def out_projection(o1, o2, w_bf16, x, gate1, g_ffn, scale2, shift2, w_router_pad, n_experts, tm):
    b, n, d = x.shape
    k = o1.shape[2]
    assert o2.shape[2] == k and w_bf16.shape == (2, k, d)
    per_batch = gate1.shape[0] > 1
    mod_map = (lambda bi, i: (bi, 0, 0)) if per_batch else (lambda bi, i: (0, 0, 0))
    row = lambda bi, i: (bi, i, 0)
    return pl.pallas_call(
        functools.partial(_out_proj_kernel, n_experts=n_experts, sub_rows=min(tm, 256)),
        grid=(b, n // tm),
        in_specs=[pl.BlockSpec((1, tm, k), row),
                  pl.BlockSpec((1, tm, k), row),
                  pl.BlockSpec((2, k, d), lambda bi, i: (0, 0, 0), pipeline_mode=pl.Buffered(1)),
                  pl.BlockSpec((1, tm, d), row),
                  pl.BlockSpec((1, 1, d), mod_map),
                  pl.BlockSpec((1, d), lambda bi, i: (0, 0)),
                  pl.BlockSpec((1, 1, d), mod_map),
                  pl.BlockSpec((1, 1, d), mod_map),
                  pl.BlockSpec((d, ROUTER_LANES), lambda bi, i: (0, 0))],
        out_specs=[pl.BlockSpec((1, tm, d), row), pl.BlockSpec((1, tm, d), row),
                   pl.BlockSpec((1, tm, ROUTER_LANES), row)],
        out_shape=[jax.ShapeDtypeStruct((b, n, d), F32), jax.ShapeDtypeStruct((b, n, d), F32),
                   jax.ShapeDtypeStruct((b, n, ROUTER_LANES), F32)],
        compiler_params=_params(("arbitrary", "arbitrary")),
        name="out_projection",
    )(o1, o2, w_bf16, x, gate1, g_ffn.reshape(1, d), scale2, shift2, w_router_pad)


MOE_ROW_CHUNK = 1024


SUBLANES = 8


def _split_rows(idx):
    b, e, cap = idx.shape
    return (idx // SUBLANES).reshape(b, e, 1, cap), (idx % SUBLANES).reshape(b, e, 1, cap)


def _gather_kernel(hi_ref, lo_ref, *refs, cap, dc):
    h_refs, o_ref, stage_ref = refs[:-2], refs[-2], refs[-1]

    def body(i, carry):
        for r in range(SUBLANES):
            c = i * SUBLANES + r
            hi, lo = hi_ref[0, 0, 0, c], lo_ref[0, 0, 0, c]
            for k, h_ref in enumerate(h_refs):
                stage_ref[i, r:r + 1, k * dc:(k + 1) * dc] = h_ref[0, hi, pl.ds(lo, 1), :]
        return carry
    lax.fori_loop(0, cap // SUBLANES, body, 0)
    o_ref[0, 0] = stage_ref[...].reshape(cap, stage_ref.shape[2]).astype(BF16)


def moe_gather(h, idx, dc):
    b, n, d = h.shape
    _, e, cap = idx.shape
    assert n % SUBLANES == 0 and cap % SUBLANES == 0
    n_chunks = d // dc
    h_specs = [pl.BlockSpec((1, n // SUBLANES, SUBLANES, dc), lambda bi, ei, k=k: (bi, 0, 0, k),
                            pipeline_mode=pl.Buffered(1)) for k in range(n_chunks)]
    smem = pl.BlockSpec((1, 1, 1, cap), lambda bi, ei: (bi, ei, 0, 0), memory_space=pltpu.SMEM)
    h_tiles = h.reshape(b, n // SUBLANES, SUBLANES, d)
    return pl.pallas_call(
        functools.partial(_gather_kernel, cap=cap, dc=dc),
        grid=(b, e),
        in_specs=[smem, smem] + h_specs,
        out_specs=pl.BlockSpec((1, 1, cap, d), lambda bi, ei: (bi, ei, 0, 0)),
        out_shape=jax.ShapeDtypeStruct((b, e, cap, d), BF16),
        scratch_shapes=[pltpu.VMEM((cap // SUBLANES, SUBLANES, d), F32)],
        compiler_params=_params(("arbitrary", "arbitrary")),
        name="moe_gather",
    )(*_split_rows(idx), *([h_tiles] * n_chunks))


EXPERT_F_CHUNK = 256


def _expert_kernel(*refs, n_chunks, step_ranges):
    n_groups = len(step_ranges)
    in_refs = refs[:2 * n_groups]
    wg_ref, wu_ref, wd_ref = refs[2 * n_groups:2 * n_groups + 3]
    out_refs = refs[2 * n_groups + 3:3 * n_groups + 3]
    wg_s, wu_s, wd_s = refs[-3:]
    step = pl.program_id(1)

    @pl.when(step < n_chunks)
    def _():
        wg_s[step] = wg_ref[0, 0].astype(BF16)
        wu_s[step] = wu_ref[0, 0].astype(BF16)
        wd_s[step] = wd_ref[0, 0].astype(BF16)

    for g, (lo, hi) in enumerate(step_ranges):
        xe_ref, gate_ref, o_ref = in_refs[2 * g], in_refs[2 * g + 1], out_refs[g]

        @pl.when((step >= lo) & (step < hi))
        def _(xe_ref=xe_ref, gate_ref=gate_ref, o_ref=o_ref):
            bs, _, cap, d = xe_ref.shape
            xe = xe_ref[...].reshape(bs * cap, d)
            ye = jnp.zeros((bs * cap, d), F32)
            for k in range(n_chunks):
                a = jnp.dot(xe, wg_s[k], preferred_element_type=F32)
                u = jnp.dot(xe, wu_s[k], preferred_element_type=F32)
                hid = (a * (1.0 / (1.0 + jnp.exp(-a))) * u).astype(BF16)
                ye = ye + jnp.dot(hid, wd_s[k], preferred_element_type=F32)
            o_ref[...] = (ye * gate_ref[...].reshape(bs * cap, 1)).reshape(bs, 1, cap, d)


def moe_experts(groups, wg, wu, wd, layer):
    _, e, d, f = wg.shape
    fc = math.gcd(f, EXPERT_F_CHUNK)
    n_chunks = f // fc
    in_specs, args, out_specs, out_shapes, step_ranges = [], [], [], [], []
    lo = n_chunks
    for xe, gate in groups:
        b, _, cap, _ = xe.shape
        bs = max(1, min(b, 512 // cap))
        assert b % bs == 0
        nb = b // bs
        tile = lambda ei, s, lo=lo, nb=nb: (jnp.clip(s - lo, 0, nb - 1), ei, 0, 0)
        in_specs += [pl.BlockSpec((bs, 1, cap, d), tile), pl.BlockSpec((bs, 1, cap, 1), tile)]
        args += [xe, gate]
        out_specs.append(pl.BlockSpec((bs, 1, cap, d), tile))
        out_shapes.append(jax.ShapeDtypeStruct((b, e, cap, d), F32))
        step_ranges.append((lo, lo + nb))
        lo += nb
    chunk = lambda ei, s: jnp.minimum(s, n_chunks - 1)
    in_specs += [pl.BlockSpec((1, 1, d, fc), lambda ei, s: (layer, ei, 0, chunk(ei, s))),
                 pl.BlockSpec((1, 1, d, fc), lambda ei, s: (layer, ei, 0, chunk(ei, s))),
                 pl.BlockSpec((1, 1, fc, d), lambda ei, s: (layer, ei, chunk(ei, s), 0))]
    return pl.pallas_call(
        functools.partial(_expert_kernel, n_chunks=n_chunks, step_ranges=tuple(step_ranges)),
        grid=(e, lo),
        in_specs=in_specs,
        out_specs=out_specs,
        out_shape=out_shapes,
        scratch_shapes=[pltpu.VMEM((n_chunks, d, fc), BF16), pltpu.VMEM((n_chunks, d, fc), BF16),
                        pltpu.VMEM((n_chunks, fc, d), BF16)],
        compiler_params=_params(("arbitrary", "arbitrary")),
        name="moe_experts",
    )(*args, wg, wu, wd)


def _combine_kernel(hi_ref, lo_ref, ye_ref, x_ref, gt_ref, *refs, cap, n_experts, rows_out, final_norm):
    gf_ref = refs[0] if final_norm else None
    o_ref, acc_ref = refs[-2:]
    step = pl.program_id(1)

    @pl.when(step == 0)
    def _():
        acc_ref[...] = jnp.zeros_like(acc_ref)

    @pl.when(step < n_experts)
    def _():
        def body(i, carry):
            rows = [(hi_ref[0, 0, 0, i * SUBLANES + r], lo_ref[0, 0, 0, i * SUBLANES + r])
                    for r in range(SUBLANES)]
            old = [acc_ref[hi, pl.ds(lo, 1), :] for hi, lo in rows]
            for r, (hi, lo) in enumerate(rows):
                acc_ref[hi, pl.ds(lo, 1), :] = old[r] + ye_ref[0, 0, i, r:r + 1, :]
            return carry
        lax.fori_loop(0, cap // SUBLANES, body, 0)

    @pl.when(step >= n_experts)
    def _():
        tiles_out = rows_out // SUBLANES
        acc = acc_ref[pl.ds((step - n_experts) * tiles_out, tiles_out)]
        x = x_ref[0] + gt_ref[0] * acc.reshape(rows_out, acc.shape[2])
        if final_norm:
            ms = jnp.mean(x * x, axis=-1, keepdims=True)
            x = x * lax.rsqrt(ms + EPS) * gf_ref[...]
        o_ref[0] = x


def moe_combine(ye, idx, x, gate2, final_gain=None):
    b, n, d = x.shape
    final_norm = final_gain is not None
    extra_specs = [pl.BlockSpec((1, d), lambda bi, s: (0, 0))] if final_norm else []
    extra_args = [final_gain.reshape(1, d)] if final_norm else []
    _, e, cap = idx.shape
    rows_out = min(n, 256)
    per_batch = gate2.shape[0] > 1
    mod_map = (lambda bi, s: (bi, 0, 0)) if per_batch else (lambda bi, s: (0, 0, 0))
    assert n % SUBLANES == 0 and cap % SUBLANES == 0 and rows_out % SUBLANES == 0
    expert = lambda bi, s: (bi, jnp.minimum(s, e - 1), 0, 0)
    expert_rows = lambda bi, s: (bi, jnp.minimum(s, e - 1), 0, 0, 0)
    tile = lambda bi, s: (bi, jnp.maximum(s - e, 0), 0)
    smem = pl.BlockSpec((1, 1, 1, cap), expert, memory_space=pltpu.SMEM)
    return pl.pallas_call(
        functools.partial(_combine_kernel, cap=cap, n_experts=e, rows_out=rows_out, final_norm=final_norm),
        grid=(b, e + n // rows_out),
        in_specs=[smem, smem,
                  pl.BlockSpec((1, 1, cap // SUBLANES, SUBLANES, d), expert_rows),
                  pl.BlockSpec((1, rows_out, d), tile),
                  pl.BlockSpec((1, 1, d), mod_map)] + extra_specs,
        out_specs=pl.BlockSpec((1, rows_out, d), tile),
        out_shape=jax.ShapeDtypeStruct((b, n, d), F32),
        scratch_shapes=[pltpu.VMEM((n // SUBLANES, SUBLANES, d), F32)],
        compiler_params=_params(("arbitrary", "arbitrary")),
        name="moe_combine",
    )(*_split_rows(idx), ye.reshape(b, e, cap // SUBLANES, SUBLANES, d), x, gate2, *extra_args)


def moe_dispatch(h, aff, n_experts):
    b, n, d = h.shape
    cap = max(1, EC_CAPACITY_FACTOR * n // n_experts)
    gate, idx = lax.top_k(jnp.swapaxes(aff[..., :n_experts], 1, 2), cap)
    return idx, gate[..., None], moe_gather(h, idx, min(d, MOE_ROW_CHUNK))


def _rope_tables(n):
    t = jnp.arange(n, dtype=jnp.int32)
    row = (t // GRID_W).astype(F32)
    col = (t % GRID_W).astype(F32)
    n_freq = HEAD_DIM // 4
    inv = ROPE_THETA ** (-jnp.arange(n_freq, dtype=F32) / n_freq)
    ar = row[:, None] * inv
    ac = col[:, None] * inv
    ang = jnp.concatenate([ar, ar, ac, ac], axis=-1)
    sign = jnp.where((jnp.arange(HEAD_DIM) % (2 * n_freq)) < n_freq, -1.0, 1.0).astype(F32)
    return jnp.cos(ang), jnp.sin(ang) * sign


def _tile_codes(layer, rope):
    if layer % 2 == 0:
        groups = [(A_HEADS * HEAD_DIM, ROPE_BIT | SCALE_BIT), (A_KV_HEADS * HEAD_DIM, ROPE_BIT),
                  (A_KV_HEADS * HEAD_DIM, 0), (B_HEADS * 2 * HEAD_DIM, ROPE_BIT | SCALE_BIT),
                  (B_HEADS * 2 * HEAD_DIM, ROPE_BIT), (B_HEADS * 2 * HEAD_DIM, 0)]
    else:
        groups = [(C_HEADS * HEAD_DIM, QNORM_BIT | ROPE_BIT | SCALE_BIT), (C_KV_HEADS * HEAD_DIM, KNORM_BIT | ROPE_BIT),
                  (C_KV_HEADS * HEAD_DIM, 0), (D_HEADS * HEAD_DIM, SCALE_BIT), (D_HEADS * HEAD_DIM, 0),
                  (D_HEADS * HEAD_DIM, 0)]
    codes = []
    for width, code in groups:
        assert width % IN_TN == 0
        codes += [code if rope else code & ~ROPE_BIT] * (width // IN_TN)
    return codes


def _lambda_init(layer):
    return 0.8 - 0.6 * math.exp(-0.3 * layer)


def _row_tile(n, target):
    t = min(n, target)
    assert n % t == 0
    return t


def kernel(x, c, ctx, c_ctx, w_ada, b_ada, g_mix, g_ffn, w_in, w_out, a_sink, b_lam_q1, b_lam_k1, b_lam_q2,
           b_lam_k2, b_subln, c_q_norm, c_k_norm, d_rpb, w_router, w_gate, w_up, w_down, g_final):
    bsz, seq, d = x.shape
    n_ctx = ctx.shape[1]
    depth = w_ada.shape[0]
    n_experts = w_router.shape[2]
    cos, sin_signed = _rope_tables(seq)
    ones_head = jnp.ones((1, HEAD_DIM), F32)

    n_rows = -(-(bsz + 1) // 8) * 8
    c_all = jnp.zeros((n_rows, d), F32).at[:bsz].set(c).at[bsz].set(c_ctx)
    mod = ada_modulation(c_all, w_ada, b_ada)

    xc = ctx
    for l in range(depth):
        need_ctx = l < depth - 1
        e = l // 2
        m_lat = mod[l, :bsz].reshape(bsz, 1, 6, d)
        m_ctx = mod[l, bsz:bsz + 1].reshape(1, 1, 6, d)
        sh1, sc1, gt1, sh2, sc2, gt2 = (m_lat[:, :, i] for i in range(6))
        sh1x, sc1x, gt1x, sh2x, sc2x, gt2x = (m_ctx[:, :, i] for i in range(6))
        w_in_l = w_in[l].astype(BF16)
        w_out_l = w_out[l].astype(BF16).reshape(2, MIX_WIDTH // 2, d)
        if l % 2 == 0:
            qn = kn = ones_head
        else:
            qn, kn = c_q_norm[e].reshape(1, HEAD_DIM), c_k_norm[e].reshape(1, HEAD_DIM)
        tm_lat = _row_tile(seq, 512)
        tm_ctx = _row_tile(n_ctx, 512)
        p = in_projection(x, g_mix[l], sc1, sh1, w_in_l, _tile_codes(l, True), cos, sin_signed, qn, kn, tm_lat)
        px = in_projection(xc, g_mix[l], sc1x, sh1x, w_in_l, _tile_codes(l, False), cos[:tm_ctx],
                           sin_signed[:tm_ctx], qn, kn, tm_ctx)
        ox1 = ox2 = None
        if l % 2 == 0:
            lam_vecs = (b_lam_q1[e], b_lam_k1[e], b_lam_q2[e], b_lam_k2[e])
            ka = A_HEADS * HEAD_DIM
            va = ka + A_KV_HEADS * HEAD_DIM
            a_cfg = dict(kv_heads=A_KV_HEADS, rep=A_HEADS // A_KV_HEADS, q_col=0, k_col=ka, v_col=va,
                         sink=a_sink[e])
            o1 = gqa_attention(p, px, p, tq=_row_tile(seq, 256), mode='band', **a_cfg)
            o2 = diff_attention(p, px, p, lam_vecs, b_subln[e], lam_init=_lambda_init(l),
                                tq=_row_tile(seq, 256), has_lat=True)
            if need_ctx:
                ox1 = gqa_attention(px, px, None, tq=_row_tile(n_ctx, 256), mode='none', **a_cfg)
                ox2 = diff_attention(px, px, None, lam_vecs, b_subln[e], lam_init=_lambda_init(l),
                                     tq=_row_tile(n_ctx, 256), has_lat=False)
        else:
            kc = C_HEADS * HEAD_DIM
            vc = kc + C_KV_HEADS * HEAD_DIM
            c_cfg = dict(kv_heads=C_KV_HEADS, rep=C_HEADS // C_KV_HEADS, q_col=0, k_col=kc, v_col=vc)
            o1 = gqa_attention(p, px, p, tq=_row_tile(seq, 256), mode='full', **c_cfg)
            o2 = natten_attention(p, px, natten_bias_slabs(d_rpb[e]))
            if need_ctx:
                qd = vc + C_KV_HEADS * HEAD_DIM
                kd = qd + D_HEADS * HEAD_DIM
                vd = kd + D_HEADS * HEAD_DIM
                ox1 = gqa_attention(px, px, None, tq=_row_tile(n_ctx, 256), mode='none', **c_cfg)
                ox2 = gqa_attention(px, px, None, kv_heads=D_HEADS, rep=1, q_col=qd, k_col=kd, v_col=vd,
                                    tq=_row_tile(n_ctx, 256), mode='none')
        w_router_pad = jnp.zeros((d, ROUTER_LANES), BF16).at[:, :n_experts].set(w_router[l].astype(BF16))
        x1, h, aff = out_projection(o1, o2, w_out_l, x, gt1, g_ffn[l], sc2, sh2, w_router_pad, n_experts,
                                    _row_tile(seq, 512))
        idx, gate, xe = moe_dispatch(h, aff, n_experts)
        groups = [(xe, gate)]
        if need_ctx:
            xc1, hx, affx = out_projection(ox1, ox2, w_out_l, xc, gt1x, g_ffn[l], sc2x, sh2x, w_router_pad,
                                           n_experts, _row_tile(n_ctx, 256))
            idx_x, gate_x, xe_x = moe_dispatch(hx, affx, n_experts)
            groups.append((xe_x, gate_x))
        ye = moe_experts(groups, w_gate, w_up, w_down, l)
        x = moe_combine(ye[0], idx, x1, gt2, g_final if l == depth - 1 else None)
        if need_ctx:
            xc = moe_combine(ye[1], idx_x, xc1, gt2x)
    return x
```

```python
import functools
import math

import jax
import jax.numpy as jnp
from jax import lax
from jax.experimental import pallas as pl
from jax.experimental.pallas import tpu as pltpu

F32 = jnp.float32
BF16 = jnp.bfloat16

HEAD_DIM = 128
GRID_W = 64
ROPE_THETA = 10000.0
EPS = 1e-6
NEG_INF = -1e30
LOG2E = math.log2(math.e)
A_HEADS, A_KV_HEADS, A_WINDOW = 8, 2, 128
B_HEADS = 4
C_HEADS, C_KV_HEADS = 8, 2
D_HEADS, NA_ROWS, NA_COLS = 8, 8, 16
N_EXPERTS = 16
EC_CAPACITY_FACTOR = 2
IN_WIDTH = 4608
MIX_WIDTH = 2048
ROUTER_LANES = 128

ROPE_BIT, QNORM_BIT, KNORM_BIT, SCALE_BIT = 1, 2, 4, 8
IN_TN = 256

V7X_VMEM_LIMIT = 56 * 1024 * 1024
V7X_VMEM_LIMIT_LARGE = 60 * 1024 * 1024


def _params(sem, vmem=V7X_VMEM_LIMIT):
    return pltpu.CompilerParams(dimension_semantics=sem, vmem_limit_bytes=vmem)


def _nt_dot(a, b):
    return lax.dot_general(a, b, (((1,), (1,)), ((), ())), preferred_element_type=F32)


def _ada_kernel(c_ref, w_ref, b_ref, o_ref):
    c = c_ref[...]
    cond = c * (1.0 / (1.0 + jnp.exp(-c)))
    o_ref[0] = jnp.dot(cond.astype(BF16), w_ref[0].astype(BF16), preferred_element_type=F32) + b_ref[0]


def ada_modulation(c_all, w_ada, b_ada):
    depth, d, n = w_ada.shape
    r = c_all.shape[0]
    tn = math.gcd(n, 1024)
    return pl.pallas_call(
        _ada_kernel,
        grid=(depth, n // tn),
        in_specs=[pl.BlockSpec((r, d), lambda l, j: (0, 0)),
                  pl.BlockSpec((1, d, tn), lambda l, j: (l, 0, j)),
                  pl.BlockSpec((1, 1, tn), lambda l, j: (l, 0, j))],
        out_specs=pl.BlockSpec((1, r, tn), lambda l, j: (l, 0, j)),
        out_shape=jax.ShapeDtypeStruct((depth, r, n), F32),
        compiler_params=_params(("arbitrary", "arbitrary")),
        name="ada_modulation",
    )(c_all, w_ada, b_ada.reshape(depth, 1, n))


def _rope_rotate(y, sin_signed):
    lane = lax.broadcasted_iota(jnp.int32, y.shape, 1)
    first = (lane & 63) < 32
    return jnp.where(first, pltpu.roll(y, 96, 1), pltpu.roll(y, 32, 1)) * sin_signed


def _in_proj_kernel(x_ref, g_ref, sc_ref, sh_ref, w_ref, cos_ref, sin_ref, qn_ref, kn_ref, o_ref, xn_ref,
                    *, codes, scale):
    x = x_ref[0]
    ms = jnp.mean(x * x, axis=-1, keepdims=True)
    y = x * lax.rsqrt(ms + EPS) * g_ref[...]
    xn_ref[...] = (y * (1.0 + sc_ref[0]) + sh_ref[0]).astype(BF16)
    for j, code in enumerate(codes):
        acc = jnp.dot(xn_ref[...], w_ref[:, j * IN_TN:(j + 1) * IN_TN], preferred_element_type=F32)
        for h in range(IN_TN // HEAD_DIM):
            y = acc[:, h * HEAD_DIM:(h + 1) * HEAD_DIM]
            if code & (QNORM_BIT | KNORM_BIT):
                gn_ref = qn_ref if code & QNORM_BIT else kn_ref
                ms = jnp.mean(y * y, axis=-1, keepdims=True)
                y = y * lax.rsqrt(ms + EPS) * gn_ref[...]
            if code & ROPE_BIT:
                y = y * cos_ref[...] + _rope_rotate(y, sin_ref[...])
            if code & SCALE_BIT:
                y = y * scale
            lo = j * IN_TN + h * HEAD_DIM
            o_ref[0, :, lo:lo + HEAD_DIM] = y.astype(BF16)


def in_projection(x, g, scale1, shift1, w_bf16, codes, cos, sin_signed, qn, kn, tm):
    b, n, d = x.shape
    width = w_bf16.shape[1]
    assert len(codes) * IN_TN == width
    per_batch = scale1.shape[0] > 1
    mod_map = (lambda bi, i: (bi, 0, 0)) if per_batch else (lambda bi, i: (0, 0, 0))
    const = lambda bi, i: (0, 0)
    return pl.pallas_call(
        functools.partial(_in_proj_kernel, codes=tuple(codes), scale=HEAD_DIM ** -0.5 * LOG2E),
        grid=(b, n // tm),
        in_specs=[pl.BlockSpec((1, tm, d), lambda bi, i: (bi, i, 0)),
                  pl.BlockSpec((1, d), const),
                  pl.BlockSpec((1, 1, d), mod_map),
                  pl.BlockSpec((1, 1, d), mod_map),
                  pl.BlockSpec((d, width), const, pipeline_mode=pl.Buffered(1)),
                  pl.BlockSpec((tm, HEAD_DIM), lambda bi, i: (i, 0)),
                  pl.BlockSpec((tm, HEAD_DIM), lambda bi, i: (i, 0)),
                  pl.BlockSpec((1, HEAD_DIM), const),
                  pl.BlockSpec((1, HEAD_DIM), const)],
        out_specs=pl.BlockSpec((1, tm, width), lambda bi, i: (bi, i, 0)),
        out_shape=jax.ShapeDtypeStruct((b, n, width), BF16),
        scratch_shapes=[pltpu.VMEM((tm, d), BF16)],
        compiler_params=_params(("arbitrary", "arbitrary")),
        name="in_projection",
    )(x, g.reshape(1, d), scale1, shift1, w_bf16, cos, sin_signed, qn, kn)


KEY_CHUNK = 256

class _SoftmaxChain:
    def __init__(self, q, chunks, s_ref, dv, sink=None):
        self.q, self.chunks, self.s_ref, self.sink = q, chunks, s_ref, sink
        m = q.shape[0]
        self.ones_col = dv == HEAD_DIM
        self.dv = dv
        self.mx = jnp.full((m, HEAD_DIM), NEG_INF, F32)
        self.ls = jnp.zeros((m, HEAD_DIM), F32)
        self.acc = jnp.zeros((m, 2 * HEAD_DIM if self.ones_col else dv), F32)

    def score_steps(self):
        return [functools.partial(self._score, j) for j in range(len(self.chunks))]

    def value_steps(self):
        return [functools.partial(self._value, j) for j in range(len(self.chunks))]

    def _score(self, j):
        k_fn, _, bias_fn = self.chunks[j]
        s = _nt_dot(self.q, k_fn())
        if bias_fn is not None:
            s = s + bias_fn()
        self.s_ref[:, j * KEY_CHUNK:(j + 1) * KEY_CHUNK] = s
        for t in range(KEY_CHUNK // HEAD_DIM):
            self.mx = jnp.maximum(self.mx, s[:, t * HEAD_DIM:(t + 1) * HEAD_DIM])

    def finish_scores(self):
        m = jnp.max(self.mx, axis=-1, keepdims=True)
        if self.sink is not None:
            m = jnp.maximum(m, self.sink)
        self.m = m
        self.m_wide = jnp.broadcast_to(m, self.mx.shape)

    def _value(self, j):
        _, v_fn, _ = self.chunks[j]
        parts = []
        for t in range(KEY_CHUNK // HEAD_DIM):
            lo = j * KEY_CHUNK + t * HEAD_DIM
            e = jnp.exp2(self.s_ref[:, lo:lo + HEAD_DIM] - self.m_wide)
            if not self.ones_col:
                self.ls = self.ls + e
            parts.append(e.astype(BF16))
        p = jnp.concatenate(parts, axis=1)
        v = v_fn()
        if self.ones_col:
            v = jnp.concatenate([v, jnp.ones(v.shape, BF16)], axis=1)
        self.acc = self.acc + jnp.dot(p, v, preferred_element_type=F32)

    def result(self):
        if self.ones_col:
            num, den = self.acc[:, :self.dv], self.acc[:, self.dv:]
        else:
            num, den = self.acc, jnp.sum(self.ls, axis=-1, keepdims=True)
        if self.sink is not None:
            den = den + jnp.exp2(self.sink - self.m)
        return num / den


def _run_chains(chains):
    prev = None
    for chain in chains:
        scores = chain.score_steps()
        values = prev.value_steps() if prev is not None else []
        for i in range(max(len(scores), len(values))):
            if i < len(values):
                values[i]()
            if i < len(scores):
                scores[i]()
        chain.finish_scores()
        prev = chain
    for step in prev.value_steps():
        step()


def _ref_chunks(k_ref, v_ref, start, length, bias=None, k_cols=slice(None), v_cols=slice(None)):
    assert length % KEY_CHUNK == 0
    out = []
    for j in range(length // KEY_CHUNK):
        if isinstance(start, int):
            rows = slice(start + j * KEY_CHUNK, start + (j + 1) * KEY_CHUNK)
        else:
            rows = pl.ds(start + j * KEY_CHUNK, KEY_CHUNK)
        out.append((lambda rows=rows: k_ref[0, rows, k_cols], lambda rows=rows: v_ref[0, rows, v_cols],
                    (lambda j=j: bias(j)) if bias is not None else None))
    return out


def _gqa_kernel(*refs, rep, groups, tq, mode, seq, use_sink):
    if use_sink:
        sink_ref, refs = refs[0], refs[1:]
    s_ref = refs[-1]
    refs = refs[:-1]
    if mode == 'none':
        q_ref, kx_ref, vx_ref, o_ref = refs
    else:
        q_ref, kx_ref, vx_ref, kl_ref, vl_ref, o_ref = refs
    gi = pl.program_id(1)
    qi = pl.program_id(2)
    bias = None
    if mode == 'band':
        band = tq + 2 * A_WINDOW
        start = pl.multiple_of(jnp.clip(qi * tq - A_WINDOW, 0, seq - band), HEAD_DIM)
        rel = (lax.broadcasted_iota(jnp.int32, (tq, KEY_CHUNK), 0)
               - lax.broadcasted_iota(jnp.int32, (tq, KEY_CHUNK), 1))

        def bias(j):
            dist = rel + (qi * tq - start - j * KEY_CHUNK)
            return jnp.where(jnp.abs(dist) <= A_WINDOW, 0.0, NEG_INF)

    chains = []
    for g in range(groups):
        kv_cols = slice(g * HEAD_DIM, (g + 1) * HEAD_DIM)
        chunks = _ref_chunks(kx_ref, vx_ref, 0, kx_ref.shape[1], k_cols=kv_cols, v_cols=kv_cols)
        if mode == 'full':
            chunks += _ref_chunks(kl_ref, vl_ref, 0, seq, k_cols=kv_cols, v_cols=kv_cols)
        elif mode == 'band':
            chunks += _ref_chunks(kl_ref, vl_ref, start, band, bias, k_cols=kv_cols, v_cols=kv_cols)
        for r in range(rep):
            head = g * rep + r
            q = q_ref[0, :, head * HEAD_DIM:(head + 1) * HEAD_DIM]
            sink = sink_ref[gi * groups * rep + head] * LOG2E if use_sink else None
            chains.append(_SoftmaxChain(q, chunks, s_ref.at[head], HEAD_DIM, sink=sink))
    _run_chains(chains)
    for head, chain in enumerate(chains):
        o_ref[0, :, head * HEAD_DIM:(head + 1) * HEAD_DIM] = chain.result().astype(BF16)


def gqa_attention(p_q, p_ctx, p_lat, *, kv_heads, rep, q_col, k_col, v_col, tq, mode, sink=None, groups=1):
    b, nq, _ = p_q.shape
    c = p_ctx.shape[1]
    qw = groups * rep * HEAD_DIM
    kw = groups * HEAD_DIM
    use_sink = sink is not None
    assert kv_heads % groups == 0 and q_col % qw == 0 and k_col % kw == 0 and v_col % kw == 0
    in_specs = [pl.BlockSpec((1, tq, qw), lambda bi, g, qi, *_: (bi, qi, q_col // qw + g)),
                pl.BlockSpec((1, c, kw), lambda bi, g, qi, *_: (bi, 0, k_col // kw + g)),
                pl.BlockSpec((1, c, kw), lambda bi, g, qi, *_: (bi, 0, v_col // kw + g))]
    args = [p_q, p_ctx, p_ctx]
    seq = 0
    if mode != 'none':
        seq = p_lat.shape[1]
        in_specs += [pl.BlockSpec((1, seq, kw), lambda bi, g, qi, *_: (bi, 0, k_col // kw + g)),
                     pl.BlockSpec((1, seq, kw), lambda bi, g, qi, *_: (bi, 0, v_col // kw + g))]
        args += [p_lat, p_lat]
    if use_sink:
        args = [sink] + args
    n_keys = c + {'none': 0, 'full': seq, 'band': tq + 2 * A_WINDOW}[mode]
    grid_spec = pltpu.PrefetchScalarGridSpec(
        num_scalar_prefetch=1 if use_sink else 0,
        grid=(b, kv_heads // groups, nq // tq),
        in_specs=in_specs,
        out_specs=pl.BlockSpec((1, tq, qw), lambda bi, g, qi, *_: (bi, qi, g)),
        scratch_shapes=[pltpu.VMEM((groups * rep, tq, n_keys), F32)])
    return pl.pallas_call(
        functools.partial(_gqa_kernel, rep=rep, groups=groups, tq=tq, mode=mode, seq=seq, use_sink=use_sink),
        grid_spec=grid_spec,
        out_shape=jax.ShapeDtypeStruct((b, nq, kv_heads * rep * HEAD_DIM), BF16),
        compiler_params=_params(("arbitrary", "arbitrary", "arbitrary")),
        name=f"gqa_attention_{mode}",
    )(*args)


def _diff_kernel(*refs, has_lat, lam_init):
    if has_lat:
        lq1, lk1, lq2, lk2, sub_ref, q_ref, kx_ref, vx_ref, kl_ref, vl_ref, o_ref, s_ref = refs
    else:
        lq1, lk1, lq2, lk2, sub_ref, q_ref, kx_ref, vx_ref, o_ref, s_ref = refs
    lam = (jnp.exp(jnp.sum(lq1[...] * lk1[...], axis=-1, keepdims=True))
           - jnp.exp(jnp.sum(lq2[...] * lk2[...], axis=-1, keepdims=True)) + lam_init)
    n_ctx = vx_ref.shape[1]
    dv = 2 * HEAD_DIM
    chains = []
    for h in range(DIFF_HEADS_PER_STEP):
        v_cols = slice(h * dv, (h + 1) * dv)
        for m in range(2):
            k_cols = slice((2 * h + m) * HEAD_DIM, (2 * h + m + 1) * HEAD_DIM)
            chunks = _ref_chunks(kx_ref, vx_ref, 0, n_ctx, k_cols=k_cols, v_cols=v_cols)
            if has_lat:
                chunks += _ref_chunks(kl_ref, vl_ref, 0, vl_ref.shape[1], k_cols=k_cols, v_cols=v_cols)
            chains.append(_SoftmaxChain(q_ref[0, :, k_cols], chunks, s_ref.at[2 * h + m], dv))
    _run_chains(chains)
    for h in range(DIFF_HEADS_PER_STEP):
        o = chains[2 * h].result() - lam * chains[2 * h + 1].result()
        ms = jnp.mean(o * o, axis=-1, keepdims=True)
        o_ref[0, :, h * dv:(h + 1) * dv] = (o * lax.rsqrt(ms + EPS) * sub_ref[...] * (1.0 - lam_init)).astype(BF16)


DIFF_HEADS_PER_STEP = 2


def diff_attention(p_q, p_ctx, p_lat, lam_vecs, subln, *, lam_init, tq, has_lat):
    b, nq, _ = p_q.shape
    c = p_ctx.shape[1]
    qb = (A_HEADS + 2 * A_KV_HEADS) * HEAD_DIM
    kb = qb + B_HEADS * 2 * HEAD_DIM
    vb = kb + B_HEADS * 2 * HEAD_DIM
    dv = 2 * HEAD_DIM
    wide = DIFF_HEADS_PER_STEP * dv
    assert qb % wide == 0 and kb % wide == 0 and vb % wide == 0 and B_HEADS % DIFF_HEADS_PER_STEP == 0
    vec = pl.BlockSpec((1, HEAD_DIM), lambda bi, h, qi: (0, 0))

    def cols(n, base):
        return pl.BlockSpec((1, n, wide), lambda bi, h, qi: (bi, 0, base // wide + h))

    in_specs = [vec, vec, vec, vec, pl.BlockSpec((1, dv), lambda bi, h, qi: (0, 0)),
                pl.BlockSpec((1, tq, wide), lambda bi, h, qi: (bi, qi, qb // wide + h)),
                cols(c, kb), cols(c, vb)]
    args = [v.reshape(1, HEAD_DIM) for v in lam_vecs] + [subln.reshape(1, dv), p_q, p_ctx, p_ctx]
    seq = 0
    if has_lat:
        seq = p_lat.shape[1]
        in_specs += [cols(seq, kb), cols(seq, vb)]
        args += [p_lat, p_lat]
    return pl.pallas_call(
        functools.partial(_diff_kernel, has_lat=has_lat, lam_init=lam_init),
        grid=(b, B_HEADS // DIFF_HEADS_PER_STEP, nq // tq),
        in_specs=in_specs,
        out_specs=pl.BlockSpec((1, tq, wide), lambda bi, h, qi: (bi, qi, h)),
        out_shape=jax.ShapeDtypeStruct((b, nq, B_HEADS * dv), BF16),
        scratch_shapes=[pltpu.VMEM((2 * DIFF_HEADS_PER_STEP, tq, c + seq), F32)],
        compiler_params=_params(("arbitrary", "arbitrary", "arbitrary")),
        name="diff_attention",
    )(*args)


NA_QROWS = 4
NA_WROWS = NA_QROWS + NA_ROWS
NA_BLOCKS_PER_STEP = 4


def _natten_kernel(q_ref, kx_ref, vx_ref, kl_ref, vl_ref, tb_ref, o_ref, s_ref, *, n_rows):
    n_blocks = n_rows // NA_QROWS
    tq = NA_QROWS * GRID_W
    n_ctx = kx_ref.shape[1]
    chains = []
    for i in range(NA_BLOCKS_PER_STEP):
        blk = pl.program_id(2) * NA_BLOCKS_PER_STEP + i
        first_row = jnp.clip(blk * NA_QROWS - NA_ROWS // 2, 0, n_rows - NA_WROWS)
        start = pl.multiple_of(first_row * GRID_W, GRID_W)
        variant = jnp.where(blk == 0, 0, jnp.where(blk == n_blocks - 1, 2, 1))

        def bias(j, variant=variant):
            return tb_ref[0, variant, :, j * KEY_CHUNK:(j + 1) * KEY_CHUNK]

        chunks = (_ref_chunks(kx_ref, vx_ref, 0, n_ctx)
                  + _ref_chunks(kl_ref, vl_ref, start, NA_WROWS * GRID_W, bias))
        chains.append(_SoftmaxChain(q_ref[0, i * tq:(i + 1) * tq, :], chunks, s_ref.at[i], HEAD_DIM))
    _run_chains(chains)
    for i, chain in enumerate(chains):
        o_ref[0, i * tq:(i + 1) * tq, :] = chain.result().astype(BF16)


def natten_bias_slabs(rpb):
    col = jnp.arange(GRID_W)
    cs = jnp.clip(col - NA_COLS // 2, 0, GRID_W - NA_COLS)
    col_mask = (col[None, :] >= cs[:, None]) & (col[None, :] < cs[:, None] + NA_COLS)
    col_idx = jnp.clip(col[None, :] - col[:, None], -(NA_COLS - 1), NA_COLS - 1) + NA_COLS - 1
    rpb_c = jnp.where(col_mask[None, None], rpb[:, :, col_idx], NEG_INF)
    i = jnp.arange(NA_QROWS)[:, None]
    a = jnp.arange(NA_WROWS)[None, :]
    half = NA_ROWS // 2
    kinds = [(a < NA_ROWS, a - i + NA_ROWS - 1),
             ((a >= i) & (a < i + NA_ROWS), a - i - half + NA_ROWS - 1),
             (a >= NA_WROWS - NA_ROWS, a - i - NA_ROWS + NA_ROWS - 1)]
    slabs = []
    for valid, roff in kinds:
        t = rpb_c[:, jnp.clip(roff, 0, 2 * NA_ROWS - 2)] * LOG2E
        t = jnp.where(valid[None, :, :, None, None], t, NEG_INF)
        slabs.append(jnp.transpose(t, (0, 1, 3, 2, 4)).reshape(rpb.shape[0], NA_QROWS * GRID_W, NA_WROWS * GRID_W))
    return jnp.stack(slabs, axis=1).astype(F32)


def natten_attention(p_lat, p_ctx, slabs):
    b, seq, _ = p_lat.shape
    c = p_ctx.shape[1]
    n_rows = seq // GRID_W
    rows_per_step = NA_QROWS * NA_BLOCKS_PER_STEP
    assert n_rows >= NA_WROWS and n_rows % rows_per_step == 0
    qd = (C_HEADS + 2 * C_KV_HEADS) * HEAD_DIM
    kd = qd + D_HEADS * HEAD_DIM
    vd = kd + D_HEADS * HEAD_DIM
    tq = rows_per_step * GRID_W

    def col(n, base):
        return pl.BlockSpec((1, n, HEAD_DIM), lambda bi, h, i: (bi, 0, base // HEAD_DIM + h))

    in_specs = [pl.BlockSpec((1, tq, HEAD_DIM), lambda bi, h, i: (bi, i, qd // HEAD_DIM + h)),
                col(c, kd), col(c, vd), col(seq, kd), col(seq, vd),
                pl.BlockSpec((1, 3, NA_QROWS * GRID_W, NA_WROWS * GRID_W), lambda bi, h, i: (h, 0, 0, 0))]
    return pl.pallas_call(
        functools.partial(_natten_kernel, n_rows=n_rows),
        grid=(b, D_HEADS, n_rows // rows_per_step),
        in_specs=in_specs,
        out_specs=pl.BlockSpec((1, tq, HEAD_DIM), lambda bi, h, i: (bi, i, h)),
        out_shape=jax.ShapeDtypeStruct((b, seq, D_HEADS * HEAD_DIM), BF16),
        scratch_shapes=[pltpu.VMEM((NA_BLOCKS_PER_STEP, NA_QROWS * GRID_W, c + NA_WROWS * GRID_W), F32)],
        compiler_params=_params(("arbitrary", "arbitrary", "arbitrary")),
        name="natten_attention",
    )(p_lat, p_ctx, p_ctx, p_lat, p_lat, slabs)


def _out_proj_kernel(o1_ref, o2_ref, w_ref, x_ref, gt_ref, g_ref, sc_ref, sh_ref, wr_ref, xo_ref, h_ref, aff_ref,
                     *, n_experts, sub_rows):
    for r0 in range(0, x_ref.shape[1], sub_rows):
        rows = slice(r0, r0 + sub_rows)
        y = (jnp.dot(o1_ref[0, rows, :], w_ref[0], preferred_element_type=F32)
             + jnp.dot(o2_ref[0, rows, :], w_ref[1], preferred_element_type=F32))
        x = x_ref[0, rows, :] + gt_ref[0] * y
        xo_ref[0, rows, :] = x
        ms = jnp.mean(x * x, axis=-1, keepdims=True)
        h = x * lax.rsqrt(ms + EPS) * g_ref[...]
        h = h * (1.0 + sc_ref[0]) + sh_ref[0]
        h_ref[0, rows, :] = h
        logits = jnp.dot(h.astype(BF16), wr_ref[...], preferred_element_type=F32)
        lane = lax.broadcasted_iota(jnp.int32, logits.shape, 1)
        logits = jnp.where(lane < n_experts, logits, NEG_INF)
        e = jnp.exp(logits - jnp.max(logits, axis=-1, keepdims=True))
        aff_ref[0, rows, :] = e / jnp.sum(e, axis=-1, keepdims=True)


def out_projection(o1, o2, w_bf16, x, gate1, g_ffn, scale2, shift2, w_router_pad, n_experts, tm):
    b, n, d = x.shape
    k = o1.shape[2]
    assert o2.shape[2] == k and w_bf16.shape == (2, k, d)
    per_batch = gate1.shape[0] > 1
    mod_map = (lambda bi, i: (bi, 0, 0)) if per_batch else (lambda bi, i: (0, 0, 0))
    row = lambda bi, i: (bi, i, 0)
    return pl.pallas_call(
        functools.partial(_out_proj_kernel, n_experts=n_experts, sub_rows=min(tm, 256)),
        grid=(b, n // tm),
        in_specs=[pl.BlockSpec((1, tm, k), row),
                  pl.BlockSpec((1, tm, k), row),
                  pl.BlockSpec((2, k, d), lambda bi, i: (0, 0, 0), pipeline_mode=pl.Buffered(1)),
                  pl.BlockSpec((1, tm, d), row),
                  pl.BlockSpec((1, 1, d), mod_map),
                  pl.BlockSpec((1, d), lambda bi, i: (0, 0)),
                  pl.BlockSpec((1, 1, d), mod_map),
                  pl.BlockSpec((1, 1, d), mod_map),
                  pl.BlockSpec((d, ROUTER_LANES), lambda bi, i: (0, 0))],
        out_specs=[pl.BlockSpec((1, tm, d), row), pl.BlockSpec((1, tm, d), row),
                   pl.BlockSpec((1, tm, ROUTER_LANES), row)],
        out_shape=[jax.ShapeDtypeStruct((b, n, d), F32), jax.ShapeDtypeStruct((b, n, d), F32),
                   jax.ShapeDtypeStruct((b, n, ROUTER_LANES), F32)],
        compiler_params=_params(("arbitrary", "arbitrary")),
        name="out_projection",
    )(o1, o2, w_bf16, x, gate1, g_ffn.reshape(1, d), scale2, shift2, w_router_pad)


MOE_ROW_CHUNK = 1024


SUBLANES = 8


def _split_rows(idx):
    b, e, cap = idx.shape
    return (idx // SUBLANES).reshape(b, e, 1, cap), (idx % SUBLANES).reshape(b, e, 1, cap)


def _gather_kernel(hi_ref, lo_ref, *refs, cap, dc):
    h_refs, o_ref, stage_ref = refs[:-2], refs[-2], refs[-1]

    def body(i, carry):
        for r in range(SUBLANES):
            c = i * SUBLANES + r
            hi, lo = hi_ref[0, 0, 0, c], lo_ref[0, 0, 0, c]
            for k, h_ref in enumerate(h_refs):
                stage_ref[i, r:r + 1, k * dc:(k + 1) * dc] = h_ref[0, hi, pl.ds(lo, 1), :]
        return carry
    lax.fori_loop(0, cap // SUBLANES, body, 0)
    o_ref[0, 0] = stage_ref[...].reshape(cap, stage_ref.shape[2]).astype(BF16)


def moe_gather(h, idx, dc):
    b, n, d = h.shape
    _, e, cap = idx.shape
    assert n % SUBLANES == 0 and cap % SUBLANES == 0
    n_chunks = d // dc
    h_specs = [pl.BlockSpec((1, n // SUBLANES, SUBLANES, dc), lambda bi, ei, k=k: (bi, 0, 0, k),
                            pipeline_mode=pl.Buffered(1)) for k in range(n_chunks)]
    smem = pl.BlockSpec((1, 1, 1, cap), lambda bi, ei: (bi, ei, 0, 0), memory_space=pltpu.SMEM)
    h_tiles = h.reshape(b, n // SUBLANES, SUBLANES, d)
    return pl.pallas_call(
        functools.partial(_gather_kernel, cap=cap, dc=dc),
        grid=(b, e),
        in_specs=[smem, smem] + h_specs,
        out_specs=pl.BlockSpec((1, 1, cap, d), lambda bi, ei: (bi, ei, 0, 0)),
        out_shape=jax.ShapeDtypeStruct((b, e, cap, d), BF16),
        scratch_shapes=[pltpu.VMEM((cap // SUBLANES, SUBLANES, d), F32)],
        compiler_params=_params(("arbitrary", "arbitrary")),
        name="moe_gather",
    )(*_split_rows(idx), *([h_tiles] * n_chunks))


EXPERT_F_CHUNK = 256


def _expert_kernel(*refs, n_chunks, step_ranges):
    n_groups = len(step_ranges)
    in_refs = refs[:2 * n_groups]
    wg_ref, wu_ref, wd_ref = refs[2 * n_groups:2 * n_groups + 3]
    out_refs = refs[2 * n_groups + 3:3 * n_groups + 3]
    wg_s, wu_s, wd_s = refs[-3:]
    step = pl.program_id(1)

    @pl.when(step < n_chunks)
    def _():
        wg_s[step] = wg_ref[0, 0].astype(BF16)
        wu_s[step] = wu_ref[0, 0].astype(BF16)
        wd_s[step] = wd_ref[0, 0].astype(BF16)

    for g, (lo, hi) in enumerate(step_ranges):
        xe_ref, gate_ref, o_ref = in_refs[2 * g], in_refs[2 * g + 1], out_refs[g]

        @pl.when((step >= lo) & (step < hi))
        def _(xe_ref=xe_ref, gate_ref=gate_ref, o_ref=o_ref):
            bs, _, cap, d = xe_ref.shape
            xe = xe_ref[...].reshape(bs * cap, d)
            ye = jnp.zeros((bs * cap, d), F32)
            for k in range(n_chunks):
                a = jnp.dot(xe, wg_s[k], preferred_element_type=F32)
                u = jnp.dot(xe, wu_s[k], preferred_element_type=F32)
                hid = (a * (1.0 / (1.0 + jnp.exp(-a))) * u).astype(BF16)
                ye = ye + jnp.dot(hid, wd_s[k], preferred_element_type=F32)
            o_ref[...] = (ye * gate_ref[...].reshape(bs * cap, 1)).reshape(bs, 1, cap, d)


def moe_experts(groups, wg, wu, wd, layer):
    _, e, d, f = wg.shape
    fc = math.gcd(f, EXPERT_F_CHUNK)
    n_chunks = f // fc
    in_specs, args, out_specs, out_shapes, step_ranges = [], [], [], [], []
    lo = n_chunks
    for xe, gate in groups:
        b, _, cap, _ = xe.shape
        bs = max(1, min(b, 512 // cap))
        assert b % bs == 0
        nb = b // bs
        tile = lambda ei, s, lo=lo, nb=nb: (jnp.clip(s - lo, 0, nb - 1), ei, 0, 0)
        in_specs += [pl.BlockSpec((bs, 1, cap, d), tile), pl.BlockSpec((bs, 1, cap, 1), tile)]
        args += [xe, gate]
        out_specs.append(pl.BlockSpec((bs, 1, cap, d), tile))
        out_shapes.append(jax.ShapeDtypeStruct((b, e, cap, d), F32))
        step_ranges.append((lo, lo + nb))
        lo += nb
    chunk = lambda ei, s: jnp.minimum(s, n_chunks - 1)
    in_specs += [pl.BlockSpec((1, 1, d, fc), lambda ei, s: (layer, ei, 0, chunk(ei, s))),
                 pl.BlockSpec((1, 1, d, fc), lambda ei, s: (layer, ei, 0, chunk(ei, s))),
                 pl.BlockSpec((1, 1, fc, d), lambda ei, s: (layer, ei, chunk(ei, s), 0))]
    return pl.pallas_call(
        functools.partial(_expert_kernel, n_chunks=n_chunks, step_ranges=tuple(step_ranges)),
        grid=(e, lo),
        in_specs=in_specs,
        out_specs=out_specs,
        out_shape=out_shapes,
        scratch_shapes=[pltpu.VMEM((n_chunks, d, fc), BF16), pltpu.VMEM((n_chunks, d, fc), BF16),
                        pltpu.VMEM((n_chunks, fc, d), BF16)],
        compiler_params=_params(("arbitrary", "arbitrary")),
        name="moe_experts",
    )(*args, wg, wu, wd)


def _combine_kernel(hi_ref, lo_ref, ye_ref, x_ref, gt_ref, *refs, cap, n_experts, rows_out, final_norm):
    gf_ref = refs[0] if final_norm else None
    o_ref, acc_ref = refs[-2:]
    step = pl.program_id(1)

    @pl.when(step == 0)
    def _():
        acc_ref[...] = jnp.zeros_like(acc_ref)

    @pl.when(step < n_experts)
    def _():
        def body(i, carry):
            rows = [(hi_ref[0, 0, 0, i * SUBLANES + r], lo_ref[0, 0, 0, i * SUBLANES + r])
                    for r in range(SUBLANES)]
            old = [acc_ref[hi, pl.ds(lo, 1), :] for hi, lo in rows]
            for r, (hi, lo) in enumerate(rows):
                acc_ref[hi, pl.ds(lo, 1), :] = old[r] + ye_ref[0, 0, i, r:r + 1, :]
            return carry
        lax.fori_loop(0, cap // SUBLANES, body, 0)

    @pl.when(step >= n_experts)
    def _():
        tiles_out = rows_out // SUBLANES
        acc = acc_ref[pl.ds((step - n_experts) * tiles_out, tiles_out)]
        x = x_ref[0] + gt_ref[0] * acc.reshape(rows_out, acc.shape[2])
        if final_norm:
            ms = jnp.mean(x * x, axis=-1, keepdims=True)
            x = x * lax.rsqrt(ms + EPS) * gf_ref[...]
        o_ref[0] = x


def moe_combine(ye, idx, x, gate2, final_gain=None):
    b, n, d = x.shape
    final_norm = final_gain is not None
    extra_specs = [pl.BlockSpec((1, d), lambda bi, s: (0, 0))] if final_norm else []
    extra_args = [final_gain.reshape(1, d)] if final_norm else []
    _, e, cap = idx.shape
    rows_out = min(n, 256)
    per_batch = gate2.shape[0] > 1
    mod_map = (lambda bi, s: (bi, 0, 0)) if per_batch else (lambda bi, s: (0, 0, 0))
    assert n % SUBLANES == 0 and cap % SUBLANES == 0 and rows_out % SUBLANES == 0
    expert = lambda bi, s: (bi, jnp.minimum(s, e - 1), 0, 0)
    expert_rows = lambda bi, s: (bi, jnp.minimum(s, e - 1), 0, 0, 0)
    tile = lambda bi, s: (bi, jnp.maximum(s - e, 0), 0)
    smem = pl.BlockSpec((1, 1, 1, cap), expert, memory_space=pltpu.SMEM)
    return pl.pallas_call(
        functools.partial(_combine_kernel, cap=cap, n_experts=e, rows_out=rows_out, final_norm=final_norm),
        grid=(b, e + n // rows_out),
        in_specs=[smem, smem,
                  pl.BlockSpec((1, 1, cap // SUBLANES, SUBLANES, d), expert_rows),
                  pl.BlockSpec((1, rows_out, d), tile),
                  pl.BlockSpec((1, 1, d), mod_map)] + extra_specs,
        out_specs=pl.BlockSpec((1, rows_out, d), tile),
        out_shape=jax.ShapeDtypeStruct((b, n, d), F32),
        scratch_shapes=[pltpu.VMEM((n // SUBLANES, SUBLANES, d), F32)],
        compiler_params=_params(("arbitrary", "arbitrary")),
        name="moe_combine",
    )(*_split_rows(idx), ye.reshape(b, e, cap // SUBLANES, SUBLANES, d), x, gate2, *extra_args)


def moe_dispatch(h, aff, n_experts):
    b, n, d = h.shape
    cap = max(1, EC_CAPACITY_FACTOR * n // n_experts)
    gate, idx = lax.top_k(jnp.swapaxes(aff[..., :n_experts], 1, 2), cap)
    return idx, gate[..., None], moe_gather(h, idx, min(d, MOE_ROW_CHUNK))


def _rope_tables(n):
    t = jnp.arange(n, dtype=jnp.int32)
    row = (t // GRID_W).astype(F32)
    col = (t % GRID_W).astype(F32)
    n_freq = HEAD_DIM // 4
    inv = ROPE_THETA ** (-jnp.arange(n_freq, dtype=F32) / n_freq)
    ar = row[:, None] * inv
    ac = col[:, None] * inv
    ang = jnp.concatenate([ar, ar, ac, ac], axis=-1)
    sign = jnp.where((jnp.arange(HEAD_DIM) % (2 * n_freq)) < n_freq, -1.0, 1.0).astype(F32)
    return jnp.cos(ang), jnp.sin(ang) * sign


def _tile_codes(layer, rope):
    if layer % 2 == 0:
        groups = [(A_HEADS * HEAD_DIM, ROPE_BIT | SCALE_BIT), (A_KV_HEADS * HEAD_DIM, ROPE_BIT),
                  (A_KV_HEADS * HEAD_DIM, 0), (B_HEADS * 2 * HEAD_DIM, ROPE_BIT | SCALE_BIT),
                  (B_HEADS * 2 * HEAD_DIM, ROPE_BIT), (B_HEADS * 2 * HEAD_DIM, 0)]
    else:
        groups = [(C_HEADS * HEAD_DIM, QNORM_BIT | ROPE_BIT | SCALE_BIT), (C_KV_HEADS * HEAD_DIM, KNORM_BIT | ROPE_BIT),
                  (C_KV_HEADS * HEAD_DIM, 0), (D_HEADS * HEAD_DIM, SCALE_BIT), (D_HEADS * HEAD_DIM, 0),
                  (D_HEADS * HEAD_DIM, 0)]
    codes = []
    for width, code in groups:
        assert width % IN_TN == 0
        codes += [code if rope else code & ~ROPE_BIT] * (width // IN_TN)
    return codes


def _lambda_init(layer):
    return 0.8 - 0.6 * math.exp(-0.3 * layer)


def _row_tile(n, target):
    t = min(n, target)
    assert n % t == 0
    return t


def kernel(x, c, ctx, c_ctx, w_ada, b_ada, g_mix, g_ffn, w_in, w_out, a_sink, b_lam_q1, b_lam_k1, b_lam_q2,
           b_lam_k2, b_subln, c_q_norm, c_k_norm, d_rpb, w_router, w_gate, w_up, w_down, g_final):
    bsz, seq, d = x.shape
    n_ctx = ctx.shape[1]
    depth = w_ada.shape[0]
    n_experts = w_router.shape[2]
    cos, sin_signed = _rope_tables(seq)
    ones_head = jnp.ones((1, HEAD_DIM), F32)

    n_rows = -(-(bsz + 1) // 8) * 8
    c_all = jnp.zeros((n_rows, d), F32).at[:bsz].set(c).at[bsz].set(c_ctx)
    mod = ada_modulation(c_all, w_ada, b_ada)

    xc = ctx
    for l in range(depth):
        need_ctx = l < depth - 1
        e = l // 2
        m_lat = mod[l, :bsz].reshape(bsz, 1, 6, d)
        m_ctx = mod[l, bsz:bsz + 1].reshape(1, 1, 6, d)
        sh1, sc1, gt1, sh2, sc2, gt2 = (m_lat[:, :, i] for i in range(6))
        sh1x, sc1x, gt1x, sh2x, sc2x, gt2x = (m_ctx[:, :, i] for i in range(6))
        w_in_l = w_in[l].astype(BF16)
        w_out_l = w_out[l].astype(BF16).reshape(2, MIX_WIDTH // 2, d)
        if l % 2 == 0:
            qn = kn = ones_head
        else:
            qn, kn = c_q_norm[e].reshape(1, HEAD_DIM), c_k_norm[e].reshape(1, HEAD_DIM)
        tm_lat = _row_tile(seq, 512)
        tm_ctx = _row_tile(n_ctx, 512)
        p = in_projection(x, g_mix[l], sc1, sh1, w_in_l, _tile_codes(l, True), cos, sin_signed, qn, kn, tm_lat)
        px = in_projection(xc, g_mix[l], sc1x, sh1x, w_in_l, _tile_codes(l, False), cos[:tm_ctx],
                           sin_signed[:tm_ctx], qn, kn, tm_ctx)
        ox1 = ox2 = None
        if l % 2 == 0:
            lam_vecs = (b_lam_q1[e], b_lam_k1[e], b_lam_q2[e], b_lam_k2[e])
            ka = A_HEADS * HEAD_DIM
            va = ka + A_KV_HEADS * HEAD_DIM
            a_cfg = dict(kv_heads=A_KV_HEADS, rep=A_HEADS // A_KV_HEADS, q_col=0, k_col=ka, v_col=va,
                         sink=a_sink[e], groups=A_KV_HEADS)
            o1 = gqa_attention(p, px, p, tq=_row_tile(seq, 256), mode='band', **a_cfg)
            o2 = diff_attention(p, px, p, lam_vecs, b_subln[e], lam_init=_lambda_init(l),
                                tq=_row_tile(seq, 256), has_lat=True)
            if need_ctx:
                ox1 = gqa_attention(px, px, None, tq=_row_tile(n_ctx, 256), mode='none', **a_cfg)
                ox2 = diff_attention(px, px, None, lam_vecs, b_subln[e], lam_init=_lambda_init(l),
                                     tq=_row_tile(n_ctx, 256), has_lat=False)
        else:
            kc = C_HEADS * HEAD_DIM
            vc = kc + C_KV_HEADS * HEAD_DIM
            c_cfg = dict(kv_heads=C_KV_HEADS, rep=C_HEADS // C_KV_HEADS, q_col=0, k_col=kc, v_col=vc,
                         groups=C_KV_HEADS)
            o1 = gqa_attention(p, px, p, tq=_row_tile(seq, 256), mode='full', **c_cfg)
            o2 = natten_attention(p, px, natten_bias_slabs(d_rpb[e]))
            if need_ctx:
                qd = vc + C_KV_HEADS * HEAD_DIM
                kd = qd + D_HEADS * HEAD_DIM
                vd = kd + D_HEADS * HEAD_DIM
                ox1 = gqa_attention(px, px, None, tq=_row_tile(n_ctx, 256), mode='none', **c_cfg)
                ox2 = gqa_attention(px, px, None, kv_heads=D_HEADS, rep=1, q_col=qd, k_col=kd, v_col=vd,
                                    tq=_row_tile(n_ctx, 256), mode='none')
        w_router_pad = jnp.zeros((d, ROUTER_LANES), BF16).at[:, :n_experts].set(w_router[l].astype(BF16))
        x1, h, aff = out_projection(o1, o2, w_out_l, x, gt1, g_ffn[l], sc2, sh2, w_router_pad, n_experts,
                                    _row_tile(seq, 512))
        idx, gate, xe = moe_dispatch(h, aff, n_experts)
        groups = [(xe, gate)]
        if need_ctx:
            xc1, hx, affx = out_projection(ox1, ox2, w_out_l, xc, gt1x, g_ffn[l], sc2x, sh2x, w_router_pad,
                                           n_experts, _row_tile(n_ctx, 256))
            idx_x, gate_x, xe_x = moe_dispatch(hx, affx, n_experts)
            groups.append((xe_x, gate_x))
        ye = moe_experts(groups, w_gate, w_up, w_down, l)
        x = moe_combine(ye[0], idx, x1, gt2, g_final if l == depth - 1 else None)
        if need_ctx:
            xc = moe_combine(ye[1], idx_x, xc1, gt2x)
    return x
```

```python
import functools
import math

import jax
import jax.numpy as jnp
from jax import lax
from jax.experimental import pallas as pl
from jax.experimental.pallas import tpu as pltpu

F32 = jnp.float32
BF16 = jnp.bfloat16

HEAD_DIM = 128
GRID_W = 64
ROPE_THETA = 10000.0
EPS = 1e-6
NEG_INF = -1e30
LOG2E = math.log2(math.e)
A_HEADS, A_KV_HEADS, A_WINDOW = 8, 2, 128
B_HEADS = 4
C_HEADS, C_KV_HEADS = 8, 2
D_HEADS, NA_ROWS, NA_COLS = 8, 8, 16
N_EXPERTS = 16
EC_CAPACITY_FACTOR = 2
IN_WIDTH = 4608
MIX_WIDTH = 2048
ROUTER_LANES = 128

ROPE_BIT, QNORM_BIT, KNORM_BIT, SCALE_BIT = 1, 2, 4, 8
IN_TN = 256

V7X_VMEM_LIMIT = 56 * 1024 * 1024
V7X_VMEM_LIMIT_LARGE = 60 * 1024 * 1024


def _params(sem, vmem=V7X_VMEM_LIMIT):
    return pltpu.CompilerParams(dimension_semantics=sem, vmem_limit_bytes=vmem)


def _nt_dot(a, b):
    return lax.dot_general(a, b, (((1,), (1,)), ((), ())), preferred_element_type=F32)


def _ada_kernel(c_ref, w_ref, b_ref, o_ref):
    c = c_ref[...]
    cond = c * (1.0 / (1.0 + jnp.exp(-c)))
    o_ref[0] = jnp.dot(cond.astype(BF16), w_ref[0].astype(BF16), preferred_element_type=F32) + b_ref[0]


def ada_modulation(c_all, w_ada, b_ada):
    depth, d, n = w_ada.shape
    r = c_all.shape[0]
    tn = math.gcd(n, 1024)
    return pl.pallas_call(
        _ada_kernel,
        grid=(depth, n // tn),
        in_specs=[pl.BlockSpec((r, d), lambda l, j: (0, 0)),
                  pl.BlockSpec((1, d, tn), lambda l, j: (l, 0, j)),
                  pl.BlockSpec((1, 1, tn), lambda l, j: (l, 0, j))],
        out_specs=pl.BlockSpec((1, r, tn), lambda l, j: (l, 0, j)),
        out_shape=jax.ShapeDtypeStruct((depth, r, n), F32),
        compiler_params=_params(("arbitrary", "arbitrary")),
        name="ada_modulation",
    )(c_all, w_ada, b_ada.reshape(depth, 1, n))


def _rope_rotate(y, sin_signed):
    lane = lax.broadcasted_iota(jnp.int32, y.shape, 1)
    first = (lane & 63) < 32
    return jnp.where(first, pltpu.roll(y, 96, 1), pltpu.roll(y, 32, 1)) * sin_signed


def _in_proj_kernel(x_ref, g_ref, sc_ref, sh_ref, w_ref, cos_ref, sin_ref, qn_ref, kn_ref, o_ref, xn_ref,
                    *, codes, scale):
    x = x_ref[0]
    ms = jnp.mean(x * x, axis=-1, keepdims=True)
    y = x * lax.rsqrt(ms + EPS) * g_ref[...]
    xn_ref[...] = (y * (1.0 + sc_ref[0]) + sh_ref[0]).astype(BF16)
    for j, code in enumerate(codes):
        acc = jnp.dot(xn_ref[...], w_ref[:, j * IN_TN:(j + 1) * IN_TN], preferred_element_type=F32)
        for h in range(IN_TN // HEAD_DIM):
            y = acc[:, h * HEAD_DIM:(h + 1) * HEAD_DIM]
            if code & (QNORM_BIT | KNORM_BIT):
                gn_ref = qn_ref if code & QNORM_BIT else kn_ref
                ms = jnp.mean(y * y, axis=-1, keepdims=True)
                y = y * lax.rsqrt(ms + EPS) * gn_ref[...]
            if code & ROPE_BIT:
                y = y * cos_ref[...] + _rope_rotate(y, sin_ref[...])
            if code & SCALE_BIT:
                y = y * scale
            lo = j * IN_TN + h * HEAD_DIM
            o_ref[0, :, lo:lo + HEAD_DIM] = y.astype(BF16)


def in_projection(x, g, scale1, shift1, w_bf16, codes, cos, sin_signed, qn, kn, tm):
    b, n, d = x.shape
    width = w_bf16.shape[1]
    assert len(codes) * IN_TN == width
    per_batch = scale1.shape[0] > 1
    mod_map = (lambda bi, i: (bi, 0, 0)) if per_batch else (lambda bi, i: (0, 0, 0))
    const = lambda bi, i: (0, 0)
    return pl.pallas_call(
        functools.partial(_in_proj_kernel, codes=tuple(codes), scale=HEAD_DIM ** -0.5 * LOG2E),
        grid=(b, n // tm),
        in_specs=[pl.BlockSpec((1, tm, d), lambda bi, i: (bi, i, 0)),
                  pl.BlockSpec((1, d), const),
                  pl.BlockSpec((1, 1, d), mod_map),
                  pl.BlockSpec((1, 1, d), mod_map),
                  pl.BlockSpec((d, width), const, pipeline_mode=pl.Buffered(1)),
                  pl.BlockSpec((tm, HEAD_DIM), lambda bi, i: (i, 0)),
                  pl.BlockSpec((tm, HEAD_DIM), lambda bi, i: (i, 0)),
                  pl.BlockSpec((1, HEAD_DIM), const),
                  pl.BlockSpec((1, HEAD_DIM), const)],
        out_specs=pl.BlockSpec((1, tm, width), lambda bi, i: (bi, i, 0)),
        out_shape=jax.ShapeDtypeStruct((b, n, width), BF16),
        scratch_shapes=[pltpu.VMEM((tm, d), BF16)],
        compiler_params=_params(("arbitrary", "arbitrary")),
        name="in_projection",
    )(x, g.reshape(1, d), scale1, shift1, w_bf16, cos, sin_signed, qn, kn)


KEY_CHUNK = 256

class _SoftmaxChain:
    def __init__(self, q, chunks, s_ref, dv, sink=None):
        self.q, self.chunks, self.s_ref, self.sink = q, chunks, s_ref, sink
        m = q.shape[0]
        self.ones_col = dv == HEAD_DIM
        self.dv = dv
        self.mx = jnp.full((m, HEAD_DIM), NEG_INF, F32)
        self.ls = jnp.zeros((m, HEAD_DIM), F32)
        self.acc = jnp.zeros((m, 2 * HEAD_DIM if self.ones_col else dv), F32)

    def score_steps(self):
        return [functools.partial(self._score, j) for j in range(len(self.chunks))]

    def value_steps(self):
        return [functools.partial(self._value, j) for j in range(len(self.chunks))]

    def _score(self, j):
        k_fn, _, bias_fn = self.chunks[j]
        s = _nt_dot(self.q, k_fn())
        if bias_fn is not None:
            s = s + bias_fn()
        self.s_ref[:, j * KEY_CHUNK:(j + 1) * KEY_CHUNK] = s
        for t in range(KEY_CHUNK // HEAD_DIM):
            self.mx = jnp.maximum(self.mx, s[:, t * HEAD_DIM:(t + 1) * HEAD_DIM])

    def finish_scores(self):
        m = jnp.max(self.mx, axis=-1, keepdims=True)
        if self.sink is not None:
            m = jnp.maximum(m, self.sink)
        self.m = m
        self.m_wide = jnp.broadcast_to(m, self.mx.shape)

    def _value(self, j):
        _, v_fn, _ = self.chunks[j]
        parts = []
        for t in range(KEY_CHUNK // HEAD_DIM):
            lo = j * KEY_CHUNK + t * HEAD_DIM
            e = jnp.exp2(self.s_ref[:, lo:lo + HEAD_DIM] - self.m_wide)
            if not self.ones_col:
                self.ls = self.ls + e
            parts.append(e.astype(BF16))
        p = jnp.concatenate(parts, axis=1)
        v = v_fn()
        if self.ones_col:
            v = jnp.concatenate([v, jnp.ones(v.shape, BF16)], axis=1)
        self.acc = self.acc + jnp.dot(p, v, preferred_element_type=F32)

    def result(self):
        if self.ones_col:
            num, den = self.acc[:, :self.dv], self.acc[:, self.dv:]
        else:
            num, den = self.acc, jnp.sum(self.ls, axis=-1, keepdims=True)
        if self.sink is not None:
            den = den + jnp.exp2(self.sink - self.m)
        return num / den


def _run_chains(chains):
    prev = None
    for chain in chains:
        scores = chain.score_steps()
        values = prev.value_steps() if prev is not None else []
        for i in range(max(len(scores), len(values))):
            if i < len(values):
                values[i]()
            if i < len(scores):
                scores[i]()
        chain.finish_scores()
        prev = chain
    for step in prev.value_steps():
        step()


def _ref_chunks(k_ref, v_ref, start, length, bias=None, k_cols=slice(None), v_cols=slice(None)):
    assert length % KEY_CHUNK == 0
    out = []
    for j in range(length // KEY_CHUNK):
        if isinstance(start, int):
            rows = slice(start + j * KEY_CHUNK, start + (j + 1) * KEY_CHUNK)
        else:
            rows = pl.ds(start + j * KEY_CHUNK, KEY_CHUNK)
        out.append((lambda rows=rows: k_ref[0, rows, k_cols], lambda rows=rows: v_ref[0, rows, v_cols],
                    (lambda j=j: bias(j)) if bias is not None else None))
    return out


def _gqa_kernel(*refs, rep, groups, tq, mode, seq, use_sink):
    if use_sink:
        sink_ref, refs = refs[0], refs[1:]
    s_ref = refs[-1]
    refs = refs[:-1]
    if mode == 'none':
        q_ref, kx_ref, vx_ref, o_ref = refs
    else:
        q_ref, kx_ref, vx_ref, kl_ref, vl_ref, o_ref = refs
    gi = pl.program_id(1)
    qi = pl.program_id(2)
    bias = None
    if mode == 'band':
        band = tq + 2 * A_WINDOW
        start = pl.multiple_of(jnp.clip(qi * tq - A_WINDOW, 0, seq - band), HEAD_DIM)
        rel = (lax.broadcasted_iota(jnp.int32, (tq, KEY_CHUNK), 0)
               - lax.broadcasted_iota(jnp.int32, (tq, KEY_CHUNK), 1))

        def bias(j):
            dist = rel + (qi * tq - start - j * KEY_CHUNK)
            return jnp.where(jnp.abs(dist) <= A_WINDOW, 0.0, NEG_INF)

    chains = []
    for g in range(groups):
        kv_cols = slice(g * HEAD_DIM, (g + 1) * HEAD_DIM)
        chunks = _ref_chunks(kx_ref, vx_ref, 0, kx_ref.shape[1], k_cols=kv_cols, v_cols=kv_cols)
        if mode == 'full':
            chunks += _ref_chunks(kl_ref, vl_ref, 0, seq, k_cols=kv_cols, v_cols=kv_cols)
        elif mode == 'band':
            chunks += _ref_chunks(kl_ref, vl_ref, start, band, bias, k_cols=kv_cols, v_cols=kv_cols)
        for r in range(rep):
            head = g * rep + r
            q = q_ref[0, :, head * HEAD_DIM:(head + 1) * HEAD_DIM]
            sink = sink_ref[gi * groups * rep + head] * LOG2E if use_sink else None
            chains.append(_SoftmaxChain(q, chunks, s_ref.at[head], HEAD_DIM, sink=sink))
    _run_chains(chains)
    for head, chain in enumerate(chains):
        o_ref[0, :, head * HEAD_DIM:(head + 1) * HEAD_DIM] = chain.result().astype(BF16)


def gqa_attention(p_q, p_ctx, p_lat, *, kv_heads, rep, q_col, k_col, v_col, tq, mode, sink=None, groups=1):
    b, nq, _ = p_q.shape
    c = p_ctx.shape[1]
    qw = groups * rep * HEAD_DIM
    kw = groups * HEAD_DIM
    use_sink = sink is not None
    assert kv_heads % groups == 0 and q_col % qw == 0 and k_col % kw == 0 and v_col % kw == 0
    in_specs = [pl.BlockSpec((1, tq, qw), lambda bi, g, qi, *_: (bi, qi, q_col // qw + g)),
                pl.BlockSpec((1, c, kw), lambda bi, g, qi, *_: (bi, 0, k_col // kw + g)),
                pl.BlockSpec((1, c, kw), lambda bi, g, qi, *_: (bi, 0, v_col // kw + g))]
    args = [p_q, p_ctx, p_ctx]
    seq = 0
    if mode != 'none':
        seq = p_lat.shape[1]
        in_specs += [pl.BlockSpec((1, seq, kw), lambda bi, g, qi, *_: (bi, 0, k_col // kw + g)),
                     pl.BlockSpec((1, seq, kw), lambda bi, g, qi, *_: (bi, 0, v_col // kw + g))]
        args += [p_lat, p_lat]
    if use_sink:
        args = [sink] + args
    n_keys = c + {'none': 0, 'full': seq, 'band': tq + 2 * A_WINDOW}[mode]
    grid_spec = pltpu.PrefetchScalarGridSpec(
        num_scalar_prefetch=1 if use_sink else 0,
        grid=(b, kv_heads // groups, nq // tq),
        in_specs=in_specs,
        out_specs=pl.BlockSpec((1, tq, qw), lambda bi, g, qi, *_: (bi, qi, g)),
        scratch_shapes=[pltpu.VMEM((groups * rep, tq, n_keys), F32)])
    return pl.pallas_call(
        functools.partial(_gqa_kernel, rep=rep, groups=groups, tq=tq, mode=mode, seq=seq, use_sink=use_sink),
        grid_spec=grid_spec,
        out_shape=jax.ShapeDtypeStruct((b, nq, kv_heads * rep * HEAD_DIM), BF16),
        compiler_params=_params(("arbitrary", "arbitrary", "arbitrary")),
        name=f"gqa_attention_{mode}",
    )(*args)


def _diff_kernel(*refs, has_lat, lam_init):
    if has_lat:
        lq1, lk1, lq2, lk2, sub_ref, q_ref, kx_ref, vx_ref, kl_ref, vl_ref, o_ref, s_ref = refs
    else:
        lq1, lk1, lq2, lk2, sub_ref, q_ref, kx_ref, vx_ref, o_ref, s_ref = refs
    lam = (jnp.exp(jnp.sum(lq1[...] * lk1[...], axis=-1, keepdims=True))
           - jnp.exp(jnp.sum(lq2[...] * lk2[...], axis=-1, keepdims=True)) + lam_init)
    n_ctx = vx_ref.shape[1]
    dv = 2 * HEAD_DIM
    chains = []
    for h in range(DIFF_HEADS_PER_STEP):
        v_cols = slice(h * dv, (h + 1) * dv)
        for m in range(2):
            k_cols = slice((2 * h + m) * HEAD_DIM, (2 * h + m + 1) * HEAD_DIM)
            chunks = _ref_chunks(kx_ref, vx_ref, 0, n_ctx, k_cols=k_cols, v_cols=v_cols)
            if has_lat:
                chunks += _ref_chunks(kl_ref, vl_ref, 0, vl_ref.shape[1], k_cols=k_cols, v_cols=v_cols)
            chains.append(_SoftmaxChain(q_ref[0, :, k_cols], chunks, s_ref.at[2 * h + m], dv))
    _run_chains(chains)
    for h in range(DIFF_HEADS_PER_STEP):
        o = chains[2 * h].result() - lam * chains[2 * h + 1].result()
        ms = jnp.mean(o * o, axis=-1, keepdims=True)
        o_ref[0, :, h * dv:(h + 1) * dv] = (o * lax.rsqrt(ms + EPS) * sub_ref[...] * (1.0 - lam_init)).astype(BF16)


DIFF_HEADS_PER_STEP = 2


def diff_attention(p_q, p_ctx, p_lat, lam_vecs, subln, *, lam_init, tq, has_lat):
    b, nq, _ = p_q.shape
    c = p_ctx.shape[1]
    qb = (A_HEADS + 2 * A_KV_HEADS) * HEAD_DIM
    kb = qb + B_HEADS * 2 * HEAD_DIM
    vb = kb + B_HEADS * 2 * HEAD_DIM
    dv = 2 * HEAD_DIM
    wide = DIFF_HEADS_PER_STEP * dv
    assert qb % wide == 0 and kb % wide == 0 and vb % wide == 0 and B_HEADS % DIFF_HEADS_PER_STEP == 0
    vec = pl.BlockSpec((1, HEAD_DIM), lambda bi, h, qi: (0, 0))

    def cols(n, base):
        return pl.BlockSpec((1, n, wide), lambda bi, h, qi: (bi, 0, base // wide + h))

    in_specs = [vec, vec, vec, vec, pl.BlockSpec((1, dv), lambda bi, h, qi: (0, 0)),
                pl.BlockSpec((1, tq, wide), lambda bi, h, qi: (bi, qi, qb // wide + h)),
                cols(c, kb), cols(c, vb)]
    args = [v.reshape(1, HEAD_DIM) for v in lam_vecs] + [subln.reshape(1, dv), p_q, p_ctx, p_ctx]
    seq = 0
    if has_lat:
        seq = p_lat.shape[1]
        in_specs += [cols(seq, kb), cols(seq, vb)]
        args += [p_lat, p_lat]
    return pl.pallas_call(
        functools.partial(_diff_kernel, has_lat=has_lat, lam_init=lam_init),
        grid=(b, B_HEADS // DIFF_HEADS_PER_STEP, nq // tq),
        in_specs=in_specs,
        out_specs=pl.BlockSpec((1, tq, wide), lambda bi, h, qi: (bi, qi, h)),
        out_shape=jax.ShapeDtypeStruct((b, nq, B_HEADS * dv), BF16),
        scratch_shapes=[pltpu.VMEM((2 * DIFF_HEADS_PER_STEP, tq, c + seq), F32)],
        compiler_params=_params(("arbitrary", "arbitrary", "arbitrary")),
        name="diff_attention",
    )(*args)


NA_QROWS = 4
NA_WROWS = NA_QROWS + NA_ROWS
NA_BLOCKS_PER_STEP = 8


def _natten_kernel(q_ref, kx_ref, vx_ref, kl_ref, vl_ref, tb_ref, o_ref, s_ref, *, n_rows):
    n_blocks = n_rows // NA_QROWS
    tq = NA_QROWS * GRID_W
    n_ctx = kx_ref.shape[1]
    chains = []
    for i in range(NA_BLOCKS_PER_STEP):
        blk = pl.program_id(2) * NA_BLOCKS_PER_STEP + i
        first_row = jnp.clip(blk * NA_QROWS - NA_ROWS // 2, 0, n_rows - NA_WROWS)
        start = pl.multiple_of(first_row * GRID_W, GRID_W)
        variant = jnp.where(blk == 0, 0, jnp.where(blk == n_blocks - 1, 2, 1))

        def bias(j, variant=variant):
            return tb_ref[0, variant, :, j * KEY_CHUNK:(j + 1) * KEY_CHUNK]

        chunks = (_ref_chunks(kx_ref, vx_ref, 0, n_ctx)
                  + _ref_chunks(kl_ref, vl_ref, start, NA_WROWS * GRID_W, bias))
        chains.append(_SoftmaxChain(q_ref[0, i * tq:(i + 1) * tq, :], chunks, s_ref.at[i], HEAD_DIM))
    _run_chains(chains)
    for i, chain in enumerate(chains):
        o_ref[0, i * tq:(i + 1) * tq, :] = chain.result().astype(BF16)


def natten_bias_slabs(rpb):
    col = jnp.arange(GRID_W)
    cs = jnp.clip(col - NA_COLS // 2, 0, GRID_W - NA_COLS)
    col_mask = (col[None, :] >= cs[:, None]) & (col[None, :] < cs[:, None] + NA_COLS)
    col_idx = jnp.clip(col[None, :] - col[:, None], -(NA_COLS - 1), NA_COLS - 1) + NA_COLS - 1
    rpb_c = jnp.where(col_mask[None, None], rpb[:, :, col_idx], NEG_INF)
    i = jnp.arange(NA_QROWS)[:, None]
    a = jnp.arange(NA_WROWS)[None, :]
    half = NA_ROWS // 2
    kinds = [(a < NA_ROWS, a - i + NA_ROWS - 1),
             ((a >= i) & (a < i + NA_ROWS), a - i - half + NA_ROWS - 1),
             (a >= NA_WROWS - NA_ROWS, a - i - NA_ROWS + NA_ROWS - 1)]
    slabs = []
    for valid, roff in kinds:
        t = rpb_c[:, jnp.clip(roff, 0, 2 * NA_ROWS - 2)] * LOG2E
        t = jnp.where(valid[None, :, :, None, None], t, NEG_INF)
        slabs.append(jnp.transpose(t, (0, 1, 3, 2, 4)).reshape(rpb.shape[0], NA_QROWS * GRID_W, NA_WROWS * GRID_W))
    return jnp.stack(slabs, axis=1).astype(F32)


def natten_attention(p_lat, p_ctx, slabs):
    b, seq, _ = p_lat.shape
    c = p_ctx.shape[1]
    n_rows = seq // GRID_W
    rows_per_step = NA_QROWS * NA_BLOCKS_PER_STEP
    assert n_rows >= NA_WROWS and n_rows % rows_per_step == 0
    qd = (C_HEADS + 2 * C_KV_HEADS) * HEAD_DIM
    kd = qd + D_HEADS * HEAD_DIM
    vd = kd + D_HEADS * HEAD_DIM
    tq = rows_per_step * GRID_W

    def col(n, base):
        return pl.BlockSpec((1, n, HEAD_DIM), lambda bi, h, i: (bi, 0, base // HEAD_DIM + h))

    in_specs = [pl.BlockSpec((1, tq, HEAD_DIM), lambda bi, h, i: (bi, i, qd // HEAD_DIM + h)),
                col(c, kd), col(c, vd), col(seq, kd), col(seq, vd),
                pl.BlockSpec((1, 3, NA_QROWS * GRID_W, NA_WROWS * GRID_W), lambda bi, h, i: (h, 0, 0, 0))]
    return pl.pallas_call(
        functools.partial(_natten_kernel, n_rows=n_rows),
        grid=(b, D_HEADS, n_rows // rows_per_step),
        in_specs=in_specs,
        out_specs=pl.BlockSpec((1, tq, HEAD_DIM), lambda bi, h, i: (bi, i, h)),
        out_shape=jax.ShapeDtypeStruct((b, seq, D_HEADS * HEAD_DIM), BF16),
        scratch_shapes=[pltpu.VMEM((NA_BLOCKS_PER_STEP, NA_QROWS * GRID_W, c + NA_WROWS * GRID_W), F32)],
        compiler_params=_params(("arbitrary", "arbitrary", "arbitrary")),
        name="natten_attention",
    )(p_lat, p_ctx, p_ctx, p_lat, p_lat, slabs)


def _out_proj_kernel(o1_ref, o2_ref, w_ref, x_ref, gt_ref, g_ref, sc_ref, sh_ref, wr_ref, xo_ref, h_ref, aff_ref,
                     *, n_experts, sub_rows):
    for r0 in range(0, x_ref.shape[1], sub_rows):
        rows = slice(r0, r0 + sub_rows)
        y = (jnp.dot(o1_ref[0, rows, :], w_ref[0], preferred_element_type=F32)
             + jnp.dot(o2_ref[0, rows, :], w_ref[1], preferred_element_type=F32))
        x = x_ref[0, rows, :] + gt_ref[0] * y
        xo_ref[0, rows, :] = x
        ms = jnp.mean(x * x, axis=-1, keepdims=True)
        h = x * lax.rsqrt(ms + EPS) * g_ref[...]
        h = h * (1.0 + sc_ref[0]) + sh_ref[0]
        h_ref[0, rows, :] = h
        logits = jnp.dot(h.astype(BF16), wr_ref[...], preferred_element_type=F32)
        lane = lax.broadcasted_iota(jnp.int32, logits.shape, 1)
        logits = jnp.where(lane < n_experts, logits, NEG_INF)
        e = jnp.exp(logits - jnp.max(logits, axis=-1, keepdims=True))
        aff_ref[0, rows, :] = e / jnp.sum(e, axis=-1, keepdims=True)


def out_projection(o1, o2, w_bf16, x, gate1, g_ffn, scale2, shift2, w_router_pad, n_experts, tm):
    b, n, d = x.shape
    k = o1.shape[2]
    assert o2.shape[2] == k and w_bf16.shape == (2, k, d)
    per_batch = gate1.shape[0] > 1
    mod_map = (lambda bi, i: (bi, 0, 0)) if per_batch else (lambda bi, i: (0, 0, 0))
    row = lambda bi, i: (bi, i, 0)
    return pl.pallas_call(
        functools.partial(_out_proj_kernel, n_experts=n_experts, sub_rows=min(tm, 256)),
        grid=(b, n // tm),
        in_specs=[pl.BlockSpec((1, tm, k), row),
                  pl.BlockSpec((1, tm, k), row),
                  pl.BlockSpec((2, k, d), lambda bi, i: (0, 0, 0), pipeline_mode=pl.Buffered(1)),
                  pl.BlockSpec((1, tm, d), row),
                  pl.BlockSpec((1, 1, d), mod_map),
                  pl.BlockSpec((1, d), lambda bi, i: (0, 0)),
                  pl.BlockSpec((1, 1, d), mod_map),
                  pl.BlockSpec((1, 1, d), mod_map),
                  pl.BlockSpec((d, ROUTER_LANES), lambda bi, i: (0, 0))],
        out_specs=[pl.BlockSpec((1, tm, d), row), pl.BlockSpec((1, tm, d), row),
                   pl.BlockSpec((1, tm, ROUTER_LANES), row)],
        out_shape=[jax.ShapeDtypeStruct((b, n, d), F32), jax.ShapeDtypeStruct((b, n, d), F32),
                   jax.ShapeDtypeStruct((b, n, ROUTER_LANES), F32)],
        compiler_params=_params(("arbitrary", "arbitrary")),
        name="out_projection",
    )(o1, o2, w_bf16, x, gate1, g_ffn.reshape(1, d), scale2, shift2, w_router_pad)


MOE_ROW_CHUNK = 1024


SUBLANES = 8


def _split_rows(idx):
    b, e, cap = idx.shape
    return (idx // SUBLANES).reshape(b, e, 1, cap), (idx % SUBLANES).reshape(b, e, 1, cap)


def _gather_kernel(hi_ref, lo_ref, *refs, cap, dc):
    h_refs, o_ref, stage_ref = refs[:-2], refs[-2], refs[-1]

    def body(i, carry):
        for r in range(SUBLANES):
            c = i * SUBLANES + r
            hi, lo = hi_ref[0, 0, 0, c], lo_ref[0, 0, 0, c]
            for k, h_ref in enumerate(h_refs):
                stage_ref[i, r:r + 1, k * dc:(k + 1) * dc] = h_ref[0, hi, pl.ds(lo, 1), :]
        return carry
    lax.fori_loop(0, cap // SUBLANES, body, 0)
    o_ref[0, 0] = stage_ref[...].reshape(cap, stage_ref.shape[2]).astype(BF16)


def moe_gather(h, idx, dc):
    b, n, d = h.shape
    _, e, cap = idx.shape
    assert n % SUBLANES == 0 and cap % SUBLANES == 0
    n_chunks = d // dc
    h_specs = [pl.BlockSpec((1, n // SUBLANES, SUBLANES, dc), lambda bi, ei, k=k: (bi, 0, 0, k),
                            pipeline_mode=pl.Buffered(1)) for k in range(n_chunks)]
    smem = pl.BlockSpec((1, 1, 1, cap), lambda bi, ei: (bi, ei, 0, 0), memory_space=pltpu.SMEM)
    h_tiles = h.reshape(b, n // SUBLANES, SUBLANES, d)
    return pl.pallas_call(
        functools.partial(_gather_kernel, cap=cap, dc=dc),
        grid=(b, e),
        in_specs=[smem, smem] + h_specs,
        out_specs=pl.BlockSpec((1, 1, cap, d), lambda bi, ei: (bi, ei, 0, 0)),
        out_shape=jax.ShapeDtypeStruct((b, e, cap, d), BF16),
        scratch_shapes=[pltpu.VMEM((cap // SUBLANES, SUBLANES, d), F32)],
        compiler_params=_params(("arbitrary", "arbitrary")),
        name="moe_gather",
    )(*_split_rows(idx), *([h_tiles] * n_chunks))


EXPERT_F_CHUNK = 256


def _expert_kernel(*refs, n_chunks, step_ranges):
    n_groups = len(step_ranges)
    in_refs = refs[:2 * n_groups]
    wg_ref, wu_ref, wd_ref = refs[2 * n_groups:2 * n_groups + 3]
    out_refs = refs[2 * n_groups + 3:3 * n_groups + 3]
    wg_s, wu_s, wd_s = refs[-3:]
    step = pl.program_id(1)

    @pl.when(step < n_chunks)
    def _():
        wg_s[step] = wg_ref[0, 0].astype(BF16)
        wu_s[step] = wu_ref[0, 0].astype(BF16)
        wd_s[step] = wd_ref[0, 0].astype(BF16)

    for g, (lo, hi) in enumerate(step_ranges):
        xe_ref, gate_ref, o_ref = in_refs[2 * g], in_refs[2 * g + 1], out_refs[g]

        @pl.when((step >= lo) & (step < hi))
        def _(xe_ref=xe_ref, gate_ref=gate_ref, o_ref=o_ref):
            bs, _, cap, d = xe_ref.shape
            xe = xe_ref[...].reshape(bs * cap, d)
            ye = jnp.zeros((bs * cap, d), F32)
            for k in range(n_chunks):
                a = jnp.dot(xe, wg_s[k], preferred_element_type=F32)
                u = jnp.dot(xe, wu_s[k], preferred_element_type=F32)
                hid = (a * (1.0 / (1.0 + jnp.exp(-a))) * u).astype(BF16)
                ye = ye + jnp.dot(hid, wd_s[k], preferred_element_type=F32)
            o_ref[...] = (ye * gate_ref[...].reshape(bs * cap, 1)).reshape(bs, 1, cap, d)


def moe_experts(groups, wg, wu, wd, layer):
    _, e, d, f = wg.shape
    fc = math.gcd(f, EXPERT_F_CHUNK)
    n_chunks = f // fc
    in_specs, args, out_specs, out_shapes, step_ranges = [], [], [], [], []
    lo = n_chunks
    for xe, gate in groups:
        b, _, cap, _ = xe.shape
        bs = max(1, min(b, 512 // cap))
        assert b % bs == 0
        nb = b // bs
        tile = lambda ei, s, lo=lo, nb=nb: (jnp.clip(s - lo, 0, nb - 1), ei, 0, 0)
        in_specs += [pl.BlockSpec((bs, 1, cap, d), tile), pl.BlockSpec((bs, 1, cap, 1), tile)]
        args += [xe, gate]
        out_specs.append(pl.BlockSpec((bs, 1, cap, d), tile))
        out_shapes.append(jax.ShapeDtypeStruct((b, e, cap, d), F32))
        step_ranges.append((lo, lo + nb))
        lo += nb
    chunk = lambda ei, s: jnp.minimum(s, n_chunks - 1)
    in_specs += [pl.BlockSpec((1, 1, d, fc), lambda ei, s: (layer, ei, 0, chunk(ei, s))),
                 pl.BlockSpec((1, 1, d, fc), lambda ei, s: (layer, ei, 0, chunk(ei, s))),
                 pl.BlockSpec((1, 1, fc, d), lambda ei, s: (layer, ei, chunk(ei, s), 0))]
    return pl.pallas_call(
        functools.partial(_expert_kernel, n_chunks=n_chunks, step_ranges=tuple(step_ranges)),
        grid=(e, lo),
        in_specs=in_specs,
        out_specs=out_specs,
        out_shape=out_shapes,
        scratch_shapes=[pltpu.VMEM((n_chunks, d, fc), BF16), pltpu.VMEM((n_chunks, d, fc), BF16),
                        pltpu.VMEM((n_chunks, fc, d), BF16)],
        compiler_params=_params(("arbitrary", "arbitrary")),
        name="moe_experts",
    )(*args, wg, wu, wd)


def _combine_kernel(hi_ref, lo_ref, ye_ref, x_ref, gt_ref, *refs, cap, n_experts, rows_out, final_norm):
    gf_ref = refs[0] if final_norm else None
    o_ref, acc_ref = refs[-2:]
    step = pl.program_id(1)

    @pl.when(step == 0)
    def _():
        acc_ref[...] = jnp.zeros_like(acc_ref)

    @pl.when(step < n_experts)
    def _():
        def body(i, carry):
            rows = [(hi_ref[0, 0, 0, i * SUBLANES + r], lo_ref[0, 0, 0, i * SUBLANES + r])
                    for r in range(SUBLANES)]
            old = [acc_ref[hi, pl.ds(lo, 1), :] for hi, lo in rows]
            for r, (hi, lo) in enumerate(rows):
                acc_ref[hi, pl.ds(lo, 1), :] = old[r] + ye_ref[0, 0, i, r:r + 1, :]
            return carry
        lax.fori_loop(0, cap // SUBLANES, body, 0)

    @pl.when(step >= n_experts)
    def _():
        tiles_out = rows_out // SUBLANES
        acc = acc_ref[pl.ds((step - n_experts) * tiles_out, tiles_out)]
        x = x_ref[0] + gt_ref[0] * acc.reshape(rows_out, acc.shape[2])
        if final_norm:
            ms = jnp.mean(x * x, axis=-1, keepdims=True)
            x = x * lax.rsqrt(ms + EPS) * gf_ref[...]
        o_ref[0] = x


def moe_combine(ye, idx, x, gate2, final_gain=None):
    b, n, d = x.shape
    final_norm = final_gain is not None
    extra_specs = [pl.BlockSpec((1, d), lambda bi, s: (0, 0))] if final_norm else []
    extra_args = [final_gain.reshape(1, d)] if final_norm else []
    _, e, cap = idx.shape
    rows_out = min(n, 256)
    per_batch = gate2.shape[0] > 1
    mod_map = (lambda bi, s: (bi, 0, 0)) if per_batch else (lambda bi, s: (0, 0, 0))
    assert n % SUBLANES == 0 and cap % SUBLANES == 0 and rows_out % SUBLANES == 0
    expert = lambda bi, s: (bi, jnp.minimum(s, e - 1), 0, 0)
    expert_rows = lambda bi, s: (bi, jnp.minimum(s, e - 1), 0, 0, 0)
    tile = lambda bi, s: (bi, jnp.maximum(s - e, 0), 0)
    smem = pl.BlockSpec((1, 1, 1, cap), expert, memory_space=pltpu.SMEM)
    return pl.pallas_call(
        functools.partial(_combine_kernel, cap=cap, n_experts=e, rows_out=rows_out, final_norm=final_norm),
        grid=(b, e + n // rows_out),
        in_specs=[smem, smem,
                  pl.BlockSpec((1, 1, cap // SUBLANES, SUBLANES, d), expert_rows),
                  pl.BlockSpec((1, rows_out, d), tile),
                  pl.BlockSpec((1, 1, d), mod_map)] + extra_specs,
        out_specs=pl.BlockSpec((1, rows_out, d), tile),
        out_shape=jax.ShapeDtypeStruct((b, n, d), F32),
        scratch_shapes=[pltpu.VMEM((n // SUBLANES, SUBLANES, d), F32)],
        compiler_params=_params(("arbitrary", "arbitrary")),
        name="moe_combine",
    )(*_split_rows(idx), ye.reshape(b, e, cap // SUBLANES, SUBLANES, d), x, gate2, *extra_args)


def moe_dispatch(h, aff, n_experts):
    b, n, d = h.shape
    cap = max(1, EC_CAPACITY_FACTOR * n // n_experts)
    gate, idx = lax.top_k(jnp.swapaxes(aff[..., :n_experts], 1, 2), cap)
    return idx, gate[..., None], moe_gather(h, idx, min(d, MOE_ROW_CHUNK))


def _rope_tables(n):
    t = jnp.arange(n, dtype=jnp.int32)
    row = (t // GRID_W).astype(F32)
    col = (t % GRID_W).astype(F32)
    n_freq = HEAD_DIM // 4
    inv = ROPE_THETA ** (-jnp.arange(n_freq, dtype=F32) / n_freq)
    ar = row[:, None] * inv
    ac = col[:, None] * inv
    ang = jnp.concatenate([ar, ar, ac, ac], axis=-1)
    sign = jnp.where((jnp.arange(HEAD_DIM) % (2 * n_freq)) < n_freq, -1.0, 1.0).astype(F32)
    return jnp.cos(ang), jnp.sin(ang) * sign


def _tile_codes(layer, rope):
    if layer % 2 == 0:
        groups = [(A_HEADS * HEAD_DIM, ROPE_BIT | SCALE_BIT), (A_KV_HEADS * HEAD_DIM, ROPE_BIT),
                  (A_KV_HEADS * HEAD_DIM, 0), (B_HEADS * 2 * HEAD_DIM, ROPE_BIT | SCALE_BIT),
                  (B_HEADS * 2 * HEAD_DIM, ROPE_BIT), (B_HEADS * 2 * HEAD_DIM, 0)]
    else:
        groups = [(C_HEADS * HEAD_DIM, QNORM_BIT | ROPE_BIT | SCALE_BIT), (C_KV_HEADS * HEAD_DIM, KNORM_BIT | ROPE_BIT),
                  (C_KV_HEADS * HEAD_DIM, 0), (D_HEADS * HEAD_DIM, SCALE_BIT), (D_HEADS * HEAD_DIM, 0),
                  (D_HEADS * HEAD_DIM, 0)]
    codes = []
    for width, code in groups:
        assert width % IN_TN == 0
        codes += [code if rope else code & ~ROPE_BIT] * (width // IN_TN)
    return codes


def _lambda_init(layer):
    return 0.8 - 0.6 * math.exp(-0.3 * layer)


def _row_tile(n, target):
    t = min(n, target)
    assert n % t == 0
    return t


def kernel(x, c, ctx, c_ctx, w_ada, b_ada, g_mix, g_ffn, w_in, w_out, a_sink, b_lam_q1, b_lam_k1, b_lam_q2,
           b_lam_k2, b_subln, c_q_norm, c_k_norm, d_rpb, w_router, w_gate, w_up, w_down, g_final):
    bsz, seq, d = x.shape
    n_ctx = ctx.shape[1]
    depth = w_ada.shape[0]
    n_experts = w_router.shape[2]
    cos, sin_signed = _rope_tables(seq)
    ones_head = jnp.ones((1, HEAD_DIM), F32)

    n_rows = -(-(bsz + 1) // 8) * 8
    c_all = jnp.zeros((n_rows, d), F32).at[:bsz].set(c).at[bsz].set(c_ctx)
    mod = ada_modulation(c_all, w_ada, b_ada)

    xc = ctx
    for l in range(depth):
        need_ctx = l < depth - 1
        e = l // 2
        m_lat = mod[l, :bsz].reshape(bsz, 1, 6, d)
        m_ctx = mod[l, bsz:bsz + 1].reshape(1, 1, 6, d)
        sh1, sc1, gt1, sh2, sc2, gt2 = (m_lat[:, :, i] for i in range(6))
        sh1x, sc1x, gt1x, sh2x, sc2x, gt2x = (m_ctx[:, :, i] for i in range(6))
        w_in_l = w_in[l].astype(BF16)
        w_out_l = w_out[l].astype(BF16).reshape(2, MIX_WIDTH // 2, d)
        if l % 2 == 0:
            qn = kn = ones_head
        else:
            qn, kn = c_q_norm[e].reshape(1, HEAD_DIM), c_k_norm[e].reshape(1, HEAD_DIM)
        tm_lat = _row_tile(seq, 512)
        tm_ctx = _row_tile(n_ctx, 512)
        p = in_projection(x, g_mix[l], sc1, sh1, w_in_l, _tile_codes(l, True), cos, sin_signed, qn, kn, tm_lat)
        px = in_projection(xc, g_mix[l], sc1x, sh1x, w_in_l, _tile_codes(l, False), cos[:tm_ctx],
                           sin_signed[:tm_ctx], qn, kn, tm_ctx)
        ox1 = ox2 = None
        if l % 2 == 0:
            lam_vecs = (b_lam_q1[e], b_lam_k1[e], b_lam_q2[e], b_lam_k2[e])
            ka = A_HEADS * HEAD_DIM
            va = ka + A_KV_HEADS * HEAD_DIM
            a_cfg = dict(kv_heads=A_KV_HEADS, rep=A_HEADS // A_KV_HEADS, q_col=0, k_col=ka, v_col=va,
                         sink=a_sink[e], groups=A_KV_HEADS)
            o1 = gqa_attention(p, px, p, tq=_row_tile(seq, 256), mode='band', **a_cfg)
            o2 = diff_attention(p, px, p, lam_vecs, b_subln[e], lam_init=_lambda_init(l),
                                tq=_row_tile(seq, 256), has_lat=True)
            if need_ctx:
                ox1 = gqa_attention(px, px, None, tq=_row_tile(n_ctx, 256), mode='none', **a_cfg)
                ox2 = diff_attention(px, px, None, lam_vecs, b_subln[e], lam_init=_lambda_init(l),
                                     tq=_row_tile(n_ctx, 256), has_lat=False)
        else:
            kc = C_HEADS * HEAD_DIM
            vc = kc + C_KV_HEADS * HEAD_DIM
            c_cfg = dict(kv_heads=C_KV_HEADS, rep=C_HEADS // C_KV_HEADS, q_col=0, k_col=kc, v_col=vc,
                         groups=C_KV_HEADS)
            o1 = gqa_attention(p, px, p, tq=_row_tile(seq, 256), mode='full', **c_cfg)
            o2 = natten_attention(p, px, natten_bias_slabs(d_rpb[e]))
            if need_ctx:
                qd = vc + C_KV_HEADS * HEAD_DIM
                kd = qd + D_HEADS * HEAD_DIM
                vd = kd + D_HEADS * HEAD_DIM
                ox1 = gqa_attention(px, px, None, tq=_row_tile(n_ctx, 256), mode='none', **c_cfg)
                ox2 = gqa_attention(px, px, None, kv_heads=D_HEADS, rep=1, q_col=qd, k_col=kd, v_col=vd,
                                    tq=_row_tile(n_ctx, 256), mode='none')
        w_router_pad = jnp.zeros((d, ROUTER_LANES), BF16).at[:, :n_experts].set(w_router[l].astype(BF16))
        x1, h, aff = out_projection(o1, o2, w_out_l, x, gt1, g_ffn[l], sc2, sh2, w_router_pad, n_experts,
                                    _row_tile(seq, 512))
        idx, gate, xe = moe_dispatch(h, aff, n_experts)
        groups = [(xe, gate)]
        if need_ctx:
            xc1, hx, affx = out_projection(ox1, ox2, w_out_l, xc, gt1x, g_ffn[l], sc2x, sh2x, w_router_pad,
                                           n_experts, _row_tile(n_ctx, 256))
            idx_x, gate_x, xe_x = moe_dispatch(hx, affx, n_experts)
            groups.append((xe_x, gate_x))
        ye = moe_experts(groups, w_gate, w_up, w_down, l)
        x = moe_combine(ye[0], idx, x1, gt2, g_final if l == depth - 1 else None)
        if need_ctx:
            xc = moe_combine(ye[1], idx_x, xc1, gt2x)
    return x
```

```python
import functools
import math

import jax
import jax.numpy as jnp
from jax import lax
from jax.experimental import pallas as pl
from jax.experimental.pallas import tpu as pltpu

F32 = jnp.float32
BF16 = jnp.bfloat16

HEAD_DIM = 128
GRID_W = 64
ROPE_THETA = 10000.0
EPS = 1e-6
NEG_INF = -1e30
LOG2E = math.log2(math.e)
A_HEADS, A_KV_HEADS, A_WINDOW = 8, 2, 128
B_HEADS = 4
C_HEADS, C_KV_HEADS = 8, 2
D_HEADS, NA_ROWS, NA_COLS = 8, 8, 16
N_EXPERTS = 16
EC_CAPACITY_FACTOR = 2
IN_WIDTH = 4608
MIX_WIDTH = 2048
ROUTER_LANES = 128

ROPE_BIT, QNORM_BIT, KNORM_BIT, SCALE_BIT = 1, 2, 4, 8
IN_TN = 256

V7X_VMEM_LIMIT = 56 * 1024 * 1024
V7X_VMEM_LIMIT_LARGE = 60 * 1024 * 1024


def _params(sem, vmem=V7X_VMEM_LIMIT):
    return pltpu.CompilerParams(dimension_semantics=sem, vmem_limit_bytes=vmem)


def _nt_dot(a, b):
    return lax.dot_general(a, b, (((1,), (1,)), ((), ())), preferred_element_type=F32)


def _ada_kernel(c_ref, w_ref, b_ref, o_ref):
    c = c_ref[...]
    cond = c * (1.0 / (1.0 + jnp.exp(-c)))
    o_ref[0] = jnp.dot(cond.astype(BF16), w_ref[0].astype(BF16), preferred_element_type=F32) + b_ref[0]


def ada_modulation(c_all, w_ada, b_ada):
    depth, d, n = w_ada.shape
    r = c_all.shape[0]
    tn = math.gcd(n, 1024)
    return pl.pallas_call(
        _ada_kernel,
        grid=(depth, n // tn),
        in_specs=[pl.BlockSpec((r, d), lambda l, j: (0, 0)),
                  pl.BlockSpec((1, d, tn), lambda l, j: (l, 0, j)),
                  pl.BlockSpec((1, 1, tn), lambda l, j: (l, 0, j))],
        out_specs=pl.BlockSpec((1, r, tn), lambda l, j: (l, 0, j)),
        out_shape=jax.ShapeDtypeStruct((depth, r, n), F32),
        compiler_params=_params(("arbitrary", "arbitrary")),
        name="ada_modulation",
    )(c_all, w_ada, b_ada.reshape(depth, 1, n))


def _rope_rotate(y, sin_signed):
    lane = lax.broadcasted_iota(jnp.int32, y.shape, 1)
    first = (lane & 63) < 32
    return jnp.where(first, pltpu.roll(y, 96, 1), pltpu.roll(y, 32, 1)) * sin_signed


def _in_proj_kernel(x_ref, g_ref, sc_ref, sh_ref, w_ref, cos_ref, sin_ref, qn_ref, kn_ref, o_ref, xn_ref,
                    *, codes, scale):
    x = x_ref[0]
    ms = jnp.mean(x * x, axis=-1, keepdims=True)
    y = x * lax.rsqrt(ms + EPS) * g_ref[...]
    xn_ref[...] = (y * (1.0 + sc_ref[0]) + sh_ref[0]).astype(BF16)
    for j, code in enumerate(codes):
        acc = jnp.dot(xn_ref[...], w_ref[:, j * IN_TN:(j + 1) * IN_TN], preferred_element_type=F32)
        for h in range(IN_TN // HEAD_DIM):
            y = acc[:, h * HEAD_DIM:(h + 1) * HEAD_DIM]
            if code & (QNORM_BIT | KNORM_BIT):
                gn_ref = qn_ref if code & QNORM_BIT else kn_ref
                ms = jnp.mean(y * y, axis=-1, keepdims=True)
                y = y * lax.rsqrt(ms + EPS) * gn_ref[...]
            if code & ROPE_BIT:
                y = y * cos_ref[...] + _rope_rotate(y, sin_ref[...])
            if code & SCALE_BIT:
                y = y * scale
            lo = j * IN_TN + h * HEAD_DIM
            o_ref[0, :, lo:lo + HEAD_DIM] = y.astype(BF16)


def in_projection(x, g, scale1, shift1, w_bf16, codes, cos, sin_signed, qn, kn, tm):
    b, n, d = x.shape
    width = w_bf16.shape[1]
    assert len(codes) * IN_TN == width
    per_batch = scale1.shape[0] > 1
    mod_map = (lambda bi, i: (bi, 0, 0)) if per_batch else (lambda bi, i: (0, 0, 0))
    const = lambda bi, i: (0, 0)
    return pl.pallas_call(
        functools.partial(_in_proj_kernel, codes=tuple(codes), scale=HEAD_DIM ** -0.5 * LOG2E),
        grid=(b, n // tm),
        in_specs=[pl.BlockSpec((1, tm, d), lambda bi, i: (bi, i, 0)),
                  pl.BlockSpec((1, d), const),
                  pl.BlockSpec((1, 1, d), mod_map),
                  pl.BlockSpec((1, 1, d), mod_map),
                  pl.BlockSpec((d, width), const, pipeline_mode=pl.Buffered(1)),
                  pl.BlockSpec((tm, HEAD_DIM), lambda bi, i: (i, 0)),
                  pl.BlockSpec((tm, HEAD_DIM), lambda bi, i: (i, 0)),
                  pl.BlockSpec((1, HEAD_DIM), const),
                  pl.BlockSpec((1, HEAD_DIM), const)],
        out_specs=pl.BlockSpec((1, tm, width), lambda bi, i: (bi, i, 0)),
        out_shape=jax.ShapeDtypeStruct((b, n, width), BF16),
        scratch_shapes=[pltpu.VMEM((tm, d), BF16)],
        compiler_params=_params(("arbitrary", "arbitrary")),
        name="in_projection",
    )(x, g.reshape(1, d), scale1, shift1, w_bf16, cos, sin_signed, qn, kn)


KEY_CHUNK = 256

class _SoftmaxChain:
    def __init__(self, q, chunks, s_ref, dv, sink=None):
        self.q, self.chunks, self.s_ref, self.sink = q, chunks, s_ref, sink
        m = q.shape[0]
        self.ones_col = dv == HEAD_DIM
        self.dv = dv
        self.mx = jnp.full((m, HEAD_DIM), NEG_INF, F32)
        self.ls = jnp.zeros((m, HEAD_DIM), F32)
        self.acc = jnp.zeros((m, 2 * HEAD_DIM if self.ones_col else dv), F32)

    def score_steps(self):
        return [functools.partial(self._score, j) for j in range(len(self.chunks))]

    def value_steps(self):
        return [functools.partial(self._value, j) for j in range(len(self.chunks))]

    def _score(self, j):
        k_fn, _, bias_fn = self.chunks[j]
        s = _nt_dot(self.q, k_fn())
        if bias_fn is not None:
            s = s + bias_fn()
        self.s_ref[:, j * KEY_CHUNK:(j + 1) * KEY_CHUNK] = s
        for t in range(KEY_CHUNK // HEAD_DIM):
            self.mx = jnp.maximum(self.mx, s[:, t * HEAD_DIM:(t + 1) * HEAD_DIM])

    def finish_scores(self):
        m = jnp.max(self.mx, axis=-1, keepdims=True)
        if self.sink is not None:
            m = jnp.maximum(m, self.sink)
        self.m = m
        self.m_wide = jnp.broadcast_to(m, self.mx.shape)

    def _value(self, j):
        _, v_fn, _ = self.chunks[j]
        parts = []
        for t in range(KEY_CHUNK // HEAD_DIM):
            lo = j * KEY_CHUNK + t * HEAD_DIM
            e = jnp.exp2(self.s_ref[:, lo:lo + HEAD_DIM] - self.m_wide)
            if not self.ones_col:
                self.ls = self.ls + e
            parts.append(e.astype(BF16))
        p = jnp.concatenate(parts, axis=1)
        v = v_fn()
        if self.ones_col:
            v = jnp.concatenate([v, jnp.ones(v.shape, BF16)], axis=1)
        self.acc = self.acc + jnp.dot(p, v, preferred_element_type=F32)

    def result(self):
        if self.ones_col:
            num, den = self.acc[:, :self.dv], self.acc[:, self.dv:]
        else:
            num, den = self.acc, jnp.sum(self.ls, axis=-1, keepdims=True)
        if self.sink is not None:
            den = den + jnp.exp2(self.sink - self.m)
        return num / den


def _run_chains(chains):
    prev = None
    for chain in chains:
        scores = chain.score_steps()
        values = prev.value_steps() if prev is not None else []
        for i in range(max(len(scores), len(values))):
            if i < len(values):
                values[i]()
            if i < len(scores):
                scores[i]()
        chain.finish_scores()
        prev = chain
    for step in prev.value_steps():
        step()


def _ref_chunks(k_ref, v_ref, start, length, bias=None, k_cols=slice(None), v_cols=slice(None)):
    assert length % KEY_CHUNK == 0
    out = []
    for j in range(length // KEY_CHUNK):
        if isinstance(start, int):
            rows = slice(start + j * KEY_CHUNK, start + (j + 1) * KEY_CHUNK)
        else:
            rows = pl.ds(start + j * KEY_CHUNK, KEY_CHUNK)
        out.append((lambda rows=rows: k_ref[0, rows, k_cols], lambda rows=rows: v_ref[0, rows, v_cols],
                    (lambda j=j: bias(j)) if bias is not None else None))
    return out


def _gqa_kernel(*refs, rep, groups, tq, mode, seq, use_sink):
    if use_sink:
        sink_ref, refs = refs[0], refs[1:]
    s_ref = refs[-1]
    refs = refs[:-1]
    if mode == 'none':
        q_ref, kx_ref, vx_ref, o_ref = refs
    else:
        q_ref, kx_ref, vx_ref, kl_ref, vl_ref, o_ref = refs
    gi = pl.program_id(1)
    qi = pl.program_id(2)
    bias = None
    if mode == 'band':
        band = tq + 2 * A_WINDOW
        start = pl.multiple_of(jnp.clip(qi * tq - A_WINDOW, 0, seq - band), HEAD_DIM)
        rel = (lax.broadcasted_iota(jnp.int32, (tq, KEY_CHUNK), 0)
               - lax.broadcasted_iota(jnp.int32, (tq, KEY_CHUNK), 1))

        def bias(j):
            dist = rel + (qi * tq - start - j * KEY_CHUNK)
            return jnp.where(jnp.abs(dist) <= A_WINDOW, 0.0, NEG_INF)

    chains = []
    for g in range(groups):
        kv_cols = slice(g * HEAD_DIM, (g + 1) * HEAD_DIM)
        chunks = _ref_chunks(kx_ref, vx_ref, 0, kx_ref.shape[1], k_cols=kv_cols, v_cols=kv_cols)
        if mode == 'full':
            chunks += _ref_chunks(kl_ref, vl_ref, 0, seq, k_cols=kv_cols, v_cols=kv_cols)
        elif mode == 'band':
            chunks += _ref_chunks(kl_ref, vl_ref, start, band, bias, k_cols=kv_cols, v_cols=kv_cols)
        for r in range(rep):
            head = g * rep + r
            q = q_ref[0, :, head * HEAD_DIM:(head + 1) * HEAD_DIM]
            sink = sink_ref[gi * groups * rep + head] * LOG2E if use_sink else None
            chains.append(_SoftmaxChain(q, chunks, s_ref.at[head], HEAD_DIM, sink=sink))
    _run_chains(chains)
    for head, chain in enumerate(chains):
        o_ref[0, :, head * HEAD_DIM:(head + 1) * HEAD_DIM] = chain.result().astype(BF16)


def gqa_attention(p_q, p_ctx, p_lat, *, kv_heads, rep, q_col, k_col, v_col, tq, mode, sink=None, groups=1):
    b, nq, _ = p_q.shape
    c = p_ctx.shape[1]
    qw = groups * rep * HEAD_DIM
    kw = groups * HEAD_DIM
    use_sink = sink is not None
    assert kv_heads % groups == 0 and q_col % qw == 0 and k_col % kw == 0 and v_col % kw == 0
    in_specs = [pl.BlockSpec((1, tq, qw), lambda bi, g, qi, *_: (bi, qi, q_col // qw + g)),
                pl.BlockSpec((1, c, kw), lambda bi, g, qi, *_: (bi, 0, k_col // kw + g)),
                pl.BlockSpec((1, c, kw), lambda bi, g, qi, *_: (bi, 0, v_col // kw + g))]
    args = [p_q, p_ctx, p_ctx]
    seq = 0
    if mode != 'none':
        seq = p_lat.shape[1]
        in_specs += [pl.BlockSpec((1, seq, kw), lambda bi, g, qi, *_: (bi, 0, k_col // kw + g)),
                     pl.BlockSpec((1, seq, kw), lambda bi, g, qi, *_: (bi, 0, v_col // kw + g))]
        args += [p_lat, p_lat]
    if use_sink:
        args = [sink] + args
    n_keys = c + {'none': 0, 'full': seq, 'band': tq + 2 * A_WINDOW}[mode]
    grid_spec = pltpu.PrefetchScalarGridSpec(
        num_scalar_prefetch=1 if use_sink else 0,
        grid=(b, kv_heads // groups, nq // tq),
        in_specs=in_specs,
        out_specs=pl.BlockSpec((1, tq, qw), lambda bi, g, qi, *_: (bi, qi, g)),
        scratch_shapes=[pltpu.VMEM((groups * rep, tq, n_keys), F32)])
    return pl.pallas_call(
        functools.partial(_gqa_kernel, rep=rep, groups=groups, tq=tq, mode=mode, seq=seq, use_sink=use_sink),
        grid_spec=grid_spec,
        out_shape=jax.ShapeDtypeStruct((b, nq, kv_heads * rep * HEAD_DIM), BF16),
        compiler_params=_params(("arbitrary", "arbitrary", "arbitrary")),
        name=f"gqa_attention_{mode}",
    )(*args)


def _diff_kernel(*refs, has_lat, lam_init):
    if has_lat:
        lq1, lk1, lq2, lk2, sub_ref, q_ref, kx_ref, vx_ref, kl_ref, vl_ref, o_ref, s_ref = refs
    else:
        lq1, lk1, lq2, lk2, sub_ref, q_ref, kx_ref, vx_ref, o_ref, s_ref = refs
    lam = (jnp.exp(jnp.sum(lq1[...] * lk1[...], axis=-1, keepdims=True))
           - jnp.exp(jnp.sum(lq2[...] * lk2[...], axis=-1, keepdims=True)) + lam_init)
    n_ctx = vx_ref.shape[1]
    dv = 2 * HEAD_DIM
    chains = []
    for h in range(DIFF_HEADS_PER_STEP):
        v_cols = slice(h * dv, (h + 1) * dv)
        for m in range(2):
            k_cols = slice((2 * h + m) * HEAD_DIM, (2 * h + m + 1) * HEAD_DIM)
            chunks = _ref_chunks(kx_ref, vx_ref, 0, n_ctx, k_cols=k_cols, v_cols=v_cols)
            if has_lat:
                chunks += _ref_chunks(kl_ref, vl_ref, 0, vl_ref.shape[1], k_cols=k_cols, v_cols=v_cols)
            chains.append(_SoftmaxChain(q_ref[0, :, k_cols], chunks, s_ref.at[2 * h + m], dv))
    _run_chains(chains)
    for h in range(DIFF_HEADS_PER_STEP):
        o = chains[2 * h].result() - lam * chains[2 * h + 1].result()
        ms = jnp.mean(o * o, axis=-1, keepdims=True)
        o_ref[0, :, h * dv:(h + 1) * dv] = (o * lax.rsqrt(ms + EPS) * sub_ref[...] * (1.0 - lam_init)).astype(BF16)


DIFF_HEADS_PER_STEP = 2


def diff_attention(p_q, p_ctx, p_lat, lam_vecs, subln, *, lam_init, tq, has_lat):
    b, nq, _ = p_q.shape
    c = p_ctx.shape[1]
    qb = (A_HEADS + 2 * A_KV_HEADS) * HEAD_DIM
    kb = qb + B_HEADS * 2 * HEAD_DIM
    vb = kb + B_HEADS * 2 * HEAD_DIM
    dv = 2 * HEAD_DIM
    wide = DIFF_HEADS_PER_STEP * dv
    assert qb % wide == 0 and kb % wide == 0 and vb % wide == 0 and B_HEADS % DIFF_HEADS_PER_STEP == 0
    vec = pl.BlockSpec((1, HEAD_DIM), lambda bi, h, qi: (0, 0))

    def cols(n, base):
        return pl.BlockSpec((1, n, wide), lambda bi, h, qi: (bi, 0, base // wide + h))

    in_specs = [vec, vec, vec, vec, pl.BlockSpec((1, dv), lambda bi, h, qi: (0, 0)),
                pl.BlockSpec((1, tq, wide), lambda bi, h, qi: (bi, qi, qb // wide + h)),
                cols(c, kb), cols(c, vb)]
    args = [v.reshape(1, HEAD_DIM) for v in lam_vecs] + [subln.reshape(1, dv), p_q, p_ctx, p_ctx]
    seq = 0
    if has_lat:
        seq = p_lat.shape[1]
        in_specs += [cols(seq, kb), cols(seq, vb)]
        args += [p_lat, p_lat]
    return pl.pallas_call(
        functools.partial(_diff_kernel, has_lat=has_lat, lam_init=lam_init),
        grid=(b, B_HEADS // DIFF_HEADS_PER_STEP, nq // tq),
        in_specs=in_specs,
        out_specs=pl.BlockSpec((1, tq, wide), lambda bi, h, qi: (bi, qi, h)),
        out_shape=jax.ShapeDtypeStruct((b, nq, B_HEADS * dv), BF16),
        scratch_shapes=[pltpu.VMEM((2 * DIFF_HEADS_PER_STEP, tq, c + seq), F32)],
        compiler_params=_params(("arbitrary", "arbitrary", "arbitrary")),
        name="diff_attention",
    )(*args)


NA_QROWS = 4
NA_WROWS = NA_QROWS + NA_ROWS
NA_BLOCKS_PER_STEP = 8


def _natten_kernel(q_ref, kx_ref, vx_ref, kl_ref, vl_ref, tb_ref, o_ref, s_ref, *, n_rows):
    n_blocks = n_rows // NA_QROWS
    tq = NA_QROWS * GRID_W
    n_ctx = kx_ref.shape[1]
    chains = []
    for i in range(NA_BLOCKS_PER_STEP):
        blk = pl.program_id(2) * NA_BLOCKS_PER_STEP + i
        first_row = jnp.clip(blk * NA_QROWS - NA_ROWS // 2, 0, n_rows - NA_WROWS)
        start = pl.multiple_of(first_row * GRID_W, GRID_W)
        variant = jnp.where(blk == 0, 0, jnp.where(blk == n_blocks - 1, 2, 1))

        def bias(j, variant=variant):
            return tb_ref[0, variant, :, j * KEY_CHUNK:(j + 1) * KEY_CHUNK]

        chunks = (_ref_chunks(kx_ref, vx_ref, 0, n_ctx)
                  + _ref_chunks(kl_ref, vl_ref, start, NA_WROWS * GRID_W, bias))
        chains.append(_SoftmaxChain(q_ref[0, i * tq:(i + 1) * tq, :], chunks, s_ref.at[i], HEAD_DIM))
    _run_chains(chains)
    for i, chain in enumerate(chains):
        o_ref[0, i * tq:(i + 1) * tq, :] = chain.result().astype(BF16)


def natten_bias_slabs(rpb):
    col = jnp.arange(GRID_W)
    cs = jnp.clip(col - NA_COLS // 2, 0, GRID_W - NA_COLS)
    col_mask = (col[None, :] >= cs[:, None]) & (col[None, :] < cs[:, None] + NA_COLS)
    col_idx = jnp.clip(col[None, :] - col[:, None], -(NA_COLS - 1), NA_COLS - 1) + NA_COLS - 1
    rpb_c = jnp.where(col_mask[None, None], rpb[:, :, col_idx], NEG_INF)
    i = jnp.arange(NA_QROWS)[:, None]
    a = jnp.arange(NA_WROWS)[None, :]
    half = NA_ROWS // 2
    kinds = [(a < NA_ROWS, a - i + NA_ROWS - 1),
             ((a >= i) & (a < i + NA_ROWS), a - i - half + NA_ROWS - 1),
             (a >= NA_WROWS - NA_ROWS, a - i - NA_ROWS + NA_ROWS - 1)]
    slabs = []
    for valid, roff in kinds:
        t = rpb_c[:, jnp.clip(roff, 0, 2 * NA_ROWS - 2)] * LOG2E
        t = jnp.where(valid[None, :, :, None, None], t, NEG_INF)
        slabs.append(jnp.transpose(t, (0, 1, 3, 2, 4)).reshape(rpb.shape[0], NA_QROWS * GRID_W, NA_WROWS * GRID_W))
    return jnp.stack(slabs, axis=1).astype(F32)


def natten_attention(p_lat, p_ctx, slabs):
    b, seq, _ = p_lat.shape
    c = p_ctx.shape[1]
    n_rows = seq // GRID_W
    rows_per_step = NA_QROWS * NA_BLOCKS_PER_STEP
    assert n_rows >= NA_WROWS and n_rows % rows_per_step == 0
    qd = (C_HEADS + 2 * C_KV_HEADS) * HEAD_DIM
    kd = qd + D_HEADS * HEAD_DIM
    vd = kd + D_HEADS * HEAD_DIM
    tq = rows_per_step * GRID_W

    def col(n, base):
        return pl.BlockSpec((1, n, HEAD_DIM), lambda bi, h, i: (bi, 0, base // HEAD_DIM + h))

    in_specs = [pl.BlockSpec((1, tq, HEAD_DIM), lambda bi, h, i: (bi, i, qd // HEAD_DIM + h)),
                col(c, kd), col(c, vd), col(seq, kd), col(seq, vd),
                pl.BlockSpec((1, 3, NA_QROWS * GRID_W, NA_WROWS * GRID_W), lambda bi, h, i: (h, 0, 0, 0))]
    return pl.pallas_call(
        functools.partial(_natten_kernel, n_rows=n_rows),
        grid=(b, D_HEADS, n_rows // rows_per_step),
        in_specs=in_specs,
        out_specs=pl.BlockSpec((1, tq, HEAD_DIM), lambda bi, h, i: (bi, i, h)),
        out_shape=jax.ShapeDtypeStruct((b, seq, D_HEADS * HEAD_DIM), BF16),
        scratch_shapes=[pltpu.VMEM((NA_BLOCKS_PER_STEP, NA_QROWS * GRID_W, c + NA_WROWS * GRID_W), F32)],
        compiler_params=_params(("arbitrary", "arbitrary", "arbitrary")),
        name="natten_attention",
    )(p_lat, p_ctx, p_ctx, p_lat, p_lat, slabs)


def _out_proj_kernel(o1_ref, o2_ref, w_ref, x_ref, gt_ref, g_ref, sc_ref, sh_ref, wr_ref, xo_ref, h_ref, aff_ref,
                     *, n_experts, sub_rows):
    for r0 in range(0, x_ref.shape[1], sub_rows):
        rows = slice(r0, r0 + sub_rows)
        y = (jnp.dot(o1_ref[0, rows, :], w_ref[0], preferred_element_type=F32)
             + jnp.dot(o2_ref[0, rows, :], w_ref[1], preferred_element_type=F32))
        x = x_ref[0, rows, :] + gt_ref[0] * y
        xo_ref[0, rows, :] = x
        ms = jnp.mean(x * x, axis=-1, keepdims=True)
        h = x * lax.rsqrt(ms + EPS) * g_ref[...]
        h = h * (1.0 + sc_ref[0]) + sh_ref[0]
        h_ref[0, rows, :] = h
        logits = jnp.dot(h.astype(BF16), wr_ref[...], preferred_element_type=F32)
        lane = lax.broadcasted_iota(jnp.int32, logits.shape, 1)
        logits = jnp.where(lane < n_experts, logits, NEG_INF)
        e = jnp.exp(logits - jnp.max(logits, axis=-1, keepdims=True))
        aff_ref[0, rows, :] = e / jnp.sum(e, axis=-1, keepdims=True)


def out_projection(o1, o2, w_bf16, x, gate1, g_ffn, scale2, shift2, w_router_pad, n_experts, tm):
    b, n, d = x.shape
    k = o1.shape[2]
    assert o2.shape[2] == k and w_bf16.shape == (2, k, d)
    per_batch = gate1.shape[0] > 1
    mod_map = (lambda bi, i: (bi, 0, 0)) if per_batch else (lambda bi, i: (0, 0, 0))
    row = lambda bi, i: (bi, i, 0)
    return pl.pallas_call(
        functools.partial(_out_proj_kernel, n_experts=n_experts, sub_rows=min(tm, 256)),
        grid=(b, n // tm),
        in_specs=[pl.BlockSpec((1, tm, k), row),
                  pl.BlockSpec((1, tm, k), row),
                  pl.BlockSpec((2, k, d), lambda bi, i: (0, 0, 0), pipeline_mode=pl.Buffered(1)),
                  pl.BlockSpec((1, tm, d), row),
                  pl.BlockSpec((1, 1, d), mod_map),
                  pl.BlockSpec((1, d), lambda bi, i: (0, 0)),
                  pl.BlockSpec((1, 1, d), mod_map),
                  pl.BlockSpec((1, 1, d), mod_map),
                  pl.BlockSpec((d, ROUTER_LANES), lambda bi, i: (0, 0))],
        out_specs=[pl.BlockSpec((1, tm, d), row), pl.BlockSpec((1, tm, d), row),
                   pl.BlockSpec((1, tm, ROUTER_LANES), row)],
        out_shape=[jax.ShapeDtypeStruct((b, n, d), F32), jax.ShapeDtypeStruct((b, n, d), F32),
                   jax.ShapeDtypeStruct((b, n, ROUTER_LANES), F32)],
        compiler_params=_params(("arbitrary", "arbitrary")),
        name="out_projection",
    )(o1, o2, w_bf16, x, gate1, g_ffn.reshape(1, d), scale2, shift2, w_router_pad)


MOE_ROW_CHUNK = 1024


SUBLANES = 8


def _split_rows(idx):
    b, e, cap = idx.shape
    return (idx // SUBLANES).reshape(b, e, 1, cap), (idx % SUBLANES).reshape(b, e, 1, cap)


def _gather_kernel(hi_ref, lo_ref, *refs, cap, dc):
    h_refs, o_ref, stage_ref = refs[:-2], refs[-2], refs[-1]

    def body(i, carry):
        for r in range(SUBLANES):
            c = i * SUBLANES + r
            hi, lo = hi_ref[0, 0, 0, c], lo_ref[0, 0, 0, c]
            for k, h_ref in enumerate(h_refs):
                stage_ref[i, r:r + 1, k * dc:(k + 1) * dc] = h_ref[0, hi, pl.ds(lo, 1), :]
        return carry
    lax.fori_loop(0, cap // SUBLANES, body, 0, unroll=4)
    o_ref[0, 0] = stage_ref[...].reshape(cap, stage_ref.shape[2]).astype(BF16)


def moe_gather(h, idx, dc):
    b, n, d = h.shape
    _, e, cap = idx.shape
    assert n % SUBLANES == 0 and cap % SUBLANES == 0
    n_chunks = d // dc
    h_specs = [pl.BlockSpec((1, n // SUBLANES, SUBLANES, dc), lambda bi, ei, k=k: (bi, 0, 0, k),
                            pipeline_mode=pl.Buffered(1)) for k in range(n_chunks)]
    smem = pl.BlockSpec((1, 1, 1, cap), lambda bi, ei: (bi, ei, 0, 0), memory_space=pltpu.SMEM)
    h_tiles = h.reshape(b, n // SUBLANES, SUBLANES, d)
    return pl.pallas_call(
        functools.partial(_gather_kernel, cap=cap, dc=dc),
        grid=(b, e),
        in_specs=[smem, smem] + h_specs,
        out_specs=pl.BlockSpec((1, 1, cap, d), lambda bi, ei: (bi, ei, 0, 0)),
        out_shape=jax.ShapeDtypeStruct((b, e, cap, d), BF16),
        scratch_shapes=[pltpu.VMEM((cap // SUBLANES, SUBLANES, d), F32)],
        compiler_params=_params(("arbitrary", "arbitrary")),
        name="moe_gather",
    )(*_split_rows(idx), *([h_tiles] * n_chunks))


EXPERT_F_CHUNK = 256


def _expert_kernel(*refs, n_chunks, step_ranges):
    n_groups = len(step_ranges)
    in_refs = refs[:2 * n_groups]
    wg_ref, wu_ref, wd_ref = refs[2 * n_groups:2 * n_groups + 3]
    out_refs = refs[2 * n_groups + 3:3 * n_groups + 3]
    wg_s, wu_s, wd_s = refs[-3:]
    step = pl.program_id(1)

    @pl.when(step < n_chunks)
    def _():
        wg_s[step] = wg_ref[0, 0].astype(BF16)
        wu_s[step] = wu_ref[0, 0].astype(BF16)
        wd_s[step] = wd_ref[0, 0].astype(BF16)

    for g, (lo, hi) in enumerate(step_ranges):
        xe_ref, gate_ref, o_ref = in_refs[2 * g], in_refs[2 * g + 1], out_refs[g]

        @pl.when((step >= lo) & (step < hi))
        def _(xe_ref=xe_ref, gate_ref=gate_ref, o_ref=o_ref):
            bs, _, cap, d = xe_ref.shape
            xe = xe_ref[...].reshape(bs * cap, d)
            ye = jnp.zeros((bs * cap, d), F32)
            for k in range(n_chunks):
                a = jnp.dot(xe, wg_s[k], preferred_element_type=F32)
                u = jnp.dot(xe, wu_s[k], preferred_element_type=F32)
                hid = (a * (1.0 / (1.0 + jnp.exp(-a))) * u).astype(BF16)
                ye = ye + jnp.dot(hid, wd_s[k], preferred_element_type=F32)
            o_ref[...] = (ye * gate_ref[...].reshape(bs * cap, 1)).reshape(bs, 1, cap, d)


def moe_experts(groups, wg, wu, wd, layer):
    _, e, d, f = wg.shape
    fc = math.gcd(f, EXPERT_F_CHUNK)
    n_chunks = f // fc
    in_specs, args, out_specs, out_shapes, step_ranges = [], [], [], [], []
    lo = n_chunks
    for xe, gate in groups:
        b, _, cap, _ = xe.shape
        bs = max(1, min(b, 512 // cap))
        assert b % bs == 0
        nb = b // bs
        tile = lambda ei, s, lo=lo, nb=nb: (jnp.clip(s - lo, 0, nb - 1), ei, 0, 0)
        in_specs += [pl.BlockSpec((bs, 1, cap, d), tile), pl.BlockSpec((bs, 1, cap, 1), tile)]
        args += [xe, gate]
        out_specs.append(pl.BlockSpec((bs, 1, cap, d), tile))
        out_shapes.append(jax.ShapeDtypeStruct((b, e, cap, d), F32))
        step_ranges.append((lo, lo + nb))
        lo += nb
    chunk = lambda ei, s: jnp.minimum(s, n_chunks - 1)
    in_specs += [pl.BlockSpec((1, 1, d, fc), lambda ei, s: (layer, ei, 0, chunk(ei, s))),
                 pl.BlockSpec((1, 1, d, fc), lambda ei, s: (layer, ei, 0, chunk(ei, s))),
                 pl.BlockSpec((1, 1, fc, d), lambda ei, s: (layer, ei, chunk(ei, s), 0))]
    return pl.pallas_call(
        functools.partial(_expert_kernel, n_chunks=n_chunks, step_ranges=tuple(step_ranges)),
        grid=(e, lo),
        in_specs=in_specs,
        out_specs=out_specs,
        out_shape=out_shapes,
        scratch_shapes=[pltpu.VMEM((n_chunks, d, fc), BF16), pltpu.VMEM((n_chunks, d, fc), BF16),
                        pltpu.VMEM((n_chunks, fc, d), BF16)],
        compiler_params=_params(("arbitrary", "arbitrary")),
        name="moe_experts",
    )(*args, wg, wu, wd)


def _combine_kernel(hi_ref, lo_ref, ye_ref, x_ref, gt_ref, *refs, cap, n_experts, rows_out, final_norm):
    gf_ref = refs[0] if final_norm else None
    o_ref, acc_ref = refs[-2:]
    step = pl.program_id(1)

    @pl.when(step == 0)
    def _():
        acc_ref[...] = jnp.zeros_like(acc_ref)

    @pl.when(step < n_experts)
    def _():
        def body(i, carry):
            rows = [(hi_ref[0, 0, 0, i * SUBLANES + r], lo_ref[0, 0, 0, i * SUBLANES + r])
                    for r in range(SUBLANES)]
            old = [acc_ref[hi, pl.ds(lo, 1), :] for hi, lo in rows]
            for r, (hi, lo) in enumerate(rows):
                acc_ref[hi, pl.ds(lo, 1), :] = old[r] + ye_ref[0, 0, i, r:r + 1, :]
            return carry
        lax.fori_loop(0, cap // SUBLANES, body, 0, unroll=4)

    @pl.when(step >= n_experts)
    def _():
        tiles_out = rows_out // SUBLANES
        acc = acc_ref[pl.ds((step - n_experts) * tiles_out, tiles_out)]
        x = x_ref[0] + gt_ref[0] * acc.reshape(rows_out, acc.shape[2])
        if final_norm:
            ms = jnp.mean(x * x, axis=-1, keepdims=True)
            x = x * lax.rsqrt(ms + EPS) * gf_ref[...]
        o_ref[0] = x


def moe_combine(ye, idx, x, gate2, final_gain=None):
    b, n, d = x.shape
    final_norm = final_gain is not None
    extra_specs = [pl.BlockSpec((1, d), lambda bi, s: (0, 0))] if final_norm else []
    extra_args = [final_gain.reshape(1, d)] if final_norm else []
    _, e, cap = idx.shape
    rows_out = min(n, 256)
    per_batch = gate2.shape[0] > 1
    mod_map = (lambda bi, s: (bi, 0, 0)) if per_batch else (lambda bi, s: (0, 0, 0))
    assert n % SUBLANES == 0 and cap % SUBLANES == 0 and rows_out % SUBLANES == 0
    expert = lambda bi, s: (bi, jnp.minimum(s, e - 1), 0, 0)
    expert_rows = lambda bi, s: (bi, jnp.minimum(s, e - 1), 0, 0, 0)
    tile = lambda bi, s: (bi, jnp.maximum(s - e, 0), 0)
    smem = pl.BlockSpec((1, 1, 1, cap), expert, memory_space=pltpu.SMEM)
    return pl.pallas_call(
        functools.partial(_combine_kernel, cap=cap, n_experts=e, rows_out=rows_out, final_norm=final_norm),
        grid=(b, e + n // rows_out),
        in_specs=[smem, smem,
                  pl.BlockSpec((1, 1, cap // SUBLANES, SUBLANES, d), expert_rows),
                  pl.BlockSpec((1, rows_out, d), tile),
                  pl.BlockSpec((1, 1, d), mod_map)] + extra_specs,
        out_specs=pl.BlockSpec((1, rows_out, d), tile),
        out_shape=jax.ShapeDtypeStruct((b, n, d), F32),
        scratch_shapes=[pltpu.VMEM((n // SUBLANES, SUBLANES, d), F32)],
        compiler_params=_params(("arbitrary", "arbitrary")),
        name="moe_combine",
    )(*_split_rows(idx), ye.reshape(b, e, cap // SUBLANES, SUBLANES, d), x, gate2, *extra_args)


def moe_dispatch(h, aff, n_experts):
    b, n, d = h.shape
    cap = max(1, EC_CAPACITY_FACTOR * n // n_experts)
    gate, idx = lax.top_k(jnp.swapaxes(aff[..., :n_experts], 1, 2), cap)
    return idx, gate[..., None], moe_gather(h, idx, min(d, MOE_ROW_CHUNK))


def _rope_tables(n):
    t = jnp.arange(n, dtype=jnp.int32)
    row = (t // GRID_W).astype(F32)
    col = (t % GRID_W).astype(F32)
    n_freq = HEAD_DIM // 4
    inv = ROPE_THETA ** (-jnp.arange(n_freq, dtype=F32) / n_freq)
    ar = row[:, None] * inv
    ac = col[:, None] * inv
    ang = jnp.concatenate([ar, ar, ac, ac], axis=-1)
    sign = jnp.where((jnp.arange(HEAD_DIM) % (2 * n_freq)) < n_freq, -1.0, 1.0).astype(F32)
    return jnp.cos(ang), jnp.sin(ang) * sign


def _tile_codes(layer, rope):
    if layer % 2 == 0:
        groups = [(A_HEADS * HEAD_DIM, ROPE_BIT | SCALE_BIT), (A_KV_HEADS * HEAD_DIM, ROPE_BIT),
                  (A_KV_HEADS * HEAD_DIM, 0), (B_HEADS * 2 * HEAD_DIM, ROPE_BIT | SCALE_BIT),
                  (B_HEADS * 2 * HEAD_DIM, ROPE_BIT), (B_HEADS * 2 * HEAD_DIM, 0)]
    else:
        groups = [(C_HEADS * HEAD_DIM, QNORM_BIT | ROPE_BIT | SCALE_BIT), (C_KV_HEADS * HEAD_DIM, KNORM_BIT | ROPE_BIT),
                  (C_KV_HEADS * HEAD_DIM, 0), (D_HEADS * HEAD_DIM, SCALE_BIT), (D_HEADS * HEAD_DIM, 0),
                  (D_HEADS * HEAD_DIM, 0)]
    codes = []
    for width, code in groups:
        assert width % IN_TN == 0
        codes += [code if rope else code & ~ROPE_BIT] * (width // IN_TN)
    return codes


def _lambda_init(layer):
    return 0.8 - 0.6 * math.exp(-0.3 * layer)


def _row_tile(n, target):
    t = min(n, target)
    assert n % t == 0
    return t


def kernel(x, c, ctx, c_ctx, w_ada, b_ada, g_mix, g_ffn, w_in, w_out, a_sink, b_lam_q1, b_lam_k1, b_lam_q2,
           b_lam_k2, b_subln, c_q_norm, c_k_norm, d_rpb, w_router, w_gate, w_up, w_down, g_final):
    bsz, seq, d = x.shape
    n_ctx = ctx.shape[1]
    depth = w_ada.shape[0]
    n_experts = w_router.shape[2]
    cos, sin_signed = _rope_tables(seq)
    ones_head = jnp.ones((1, HEAD_DIM), F32)

    n_rows = -(-(bsz + 1) // 8) * 8
    c_all = jnp.zeros((n_rows, d), F32).at[:bsz].set(c).at[bsz].set(c_ctx)
    mod = ada_modulation(c_all, w_ada, b_ada)

    xc = ctx
    for l in range(depth):
        need_ctx = l < depth - 1
        e = l // 2
        m_lat = mod[l, :bsz].reshape(bsz, 1, 6, d)
        m_ctx = mod[l, bsz:bsz + 1].reshape(1, 1, 6, d)
        sh1, sc1, gt1, sh2, sc2, gt2 = (m_lat[:, :, i] for i in range(6))
        sh1x, sc1x, gt1x, sh2x, sc2x, gt2x = (m_ctx[:, :, i] for i in range(6))
        w_in_l = w_in[l].astype(BF16)
        w_out_l = w_out[l].astype(BF16).reshape(2, MIX_WIDTH // 2, d)
        if l % 2 == 0:
            qn = kn = ones_head
        else:
            qn, kn = c_q_norm[e].reshape(1, HEAD_DIM), c_k_norm[e].reshape(1, HEAD_DIM)
        tm_lat = _row_tile(seq, 512)
        tm_ctx = _row_tile(n_ctx, 512)
        p = in_projection(x, g_mix[l], sc1, sh1, w_in_l, _tile_codes(l, True), cos, sin_signed, qn, kn, tm_lat)
        px = in_projection(xc, g_mix[l], sc1x, sh1x, w_in_l, _tile_codes(l, False), cos[:tm_ctx],
                           sin_signed[:tm_ctx], qn, kn, tm_ctx)
        ox1 = ox2 = None
        if l % 2 == 0:
            lam_vecs = (b_lam_q1[e], b_lam_k1[e], b_lam_q2[e], b_lam_k2[e])
            ka = A_HEADS * HEAD_DIM
            va = ka + A_KV_HEADS * HEAD_DIM
            a_cfg = dict(kv_heads=A_KV_HEADS, rep=A_HEADS // A_KV_HEADS, q_col=0, k_col=ka, v_col=va,
                         sink=a_sink[e], groups=A_KV_HEADS)
            o1 = gqa_attention(p, px, p, tq=_row_tile(seq, 256), mode='band', **a_cfg)
            o2 = diff_attention(p, px, p, lam_vecs, b_subln[e], lam_init=_lambda_init(l),
                                tq=_row_tile(seq, 256), has_lat=True)
            if need_ctx:
                ox1 = gqa_attention(px, px, None, tq=_row_tile(n_ctx, 256), mode='none', **a_cfg)
                ox2 = diff_attention(px, px, None, lam_vecs, b_subln[e], lam_init=_lambda_init(l),
                                     tq=_row_tile(n_ctx, 256), has_lat=False)
        else:
            kc = C_HEADS * HEAD_DIM
            vc = kc + C_KV_HEADS * HEAD_DIM
            c_cfg = dict(kv_heads=C_KV_HEADS, rep=C_HEADS // C_KV_HEADS, q_col=0, k_col=kc, v_col=vc,
                         groups=C_KV_HEADS)
            o1 = gqa_attention(p, px, p, tq=_row_tile(seq, 256), mode='full', **c_cfg)
            o2 = natten_attention(p, px, natten_bias_slabs(d_rpb[e]))
            if need_ctx:
                qd = vc + C_KV_HEADS * HEAD_DIM
                kd = qd + D_HEADS * HEAD_DIM
                vd = kd + D_HEADS * HEAD_DIM
                ox1 = gqa_attention(px, px, None, tq=_row_tile(n_ctx, 256), mode='none', **c_cfg)
                ox2 = gqa_attention(px, px, None, kv_heads=D_HEADS, rep=1, q_col=qd, k_col=kd, v_col=vd,
                                    tq=_row_tile(n_ctx, 256), mode='none')
        w_router_pad = jnp.zeros((d, ROUTER_LANES), BF16).at[:, :n_experts].set(w_router[l].astype(BF16))
        x1, h, aff = out_projection(o1, o2, w_out_l, x, gt1, g_ffn[l], sc2, sh2, w_router_pad, n_experts,
                                    _row_tile(seq, 512))
        idx, gate, xe = moe_dispatch(h, aff, n_experts)
        groups = [(xe, gate)]
        if need_ctx:
            xc1, hx, affx = out_projection(ox1, ox2, w_out_l, xc, gt1x, g_ffn[l], sc2x, sh2x, w_router_pad,
                                           n_experts, _row_tile(n_ctx, 256))
            idx_x, gate_x, xe_x = moe_dispatch(hx, affx, n_experts)
            groups.append((xe_x, gate_x))
        ye = moe_experts(groups, w_gate, w_up, w_down, l)
        x = moe_combine(ye[0], idx, x1, gt2, g_final if l == depth - 1 else None)
        if need_ctx:
            xc = moe_combine(ye[1], idx_x, xc1, gt2x)
    return x
```

```python
import functools
import math

import jax
import jax.numpy as jnp
from jax import lax
from jax.experimental import pallas as pl
from jax.experimental.pallas import tpu as pltpu

F32 = jnp.float32
BF16 = jnp.bfloat16

HEAD_DIM = 128
GRID_W = 64
ROPE_THETA = 10000.0
EPS = 1e-6
NEG_INF = -1e30
LOG2E = math.log2(math.e)
A_HEADS, A_KV_HEADS, A_WINDOW = 8, 2, 128
B_HEADS = 4
C_HEADS, C_KV_HEADS = 8, 2
D_HEADS, NA_ROWS, NA_COLS = 8, 8, 16
N_EXPERTS = 16
EC_CAPACITY_FACTOR = 2
IN_WIDTH = 4608
MIX_WIDTH = 2048
ROUTER_LANES = 128

ROPE_BIT, QNORM_BIT, KNORM_BIT, SCALE_BIT = 1, 2, 4, 8
IN_TN = 256

V7X_VMEM_LIMIT = 56 * 1024 * 1024
V7X_VMEM_LIMIT_LARGE = 60 * 1024 * 1024


def _params(sem, vmem=V7X_VMEM_LIMIT):
    return pltpu.CompilerParams(dimension_semantics=sem, vmem_limit_bytes=vmem)


def _nt_dot(a, b):
    return lax.dot_general(a, b, (((1,), (1,)), ((), ())), preferred_element_type=F32)


def _ada_kernel(c_ref, w_ref, b_ref, o_ref):
    c = c_ref[...]
    cond = c * (1.0 / (1.0 + jnp.exp(-c)))
    o_ref[0] = jnp.dot(cond.astype(BF16), w_ref[0].astype(BF16), preferred_element_type=F32) + b_ref[0]


def ada_modulation(c_all, w_ada, b_ada):
    depth, d, n = w_ada.shape
    r = c_all.shape[0]
    tn = math.gcd(n, 1024)
    return pl.pallas_call(
        _ada_kernel,
        grid=(depth, n // tn),
        in_specs=[pl.BlockSpec((r, d), lambda l, j: (0, 0)),
                  pl.BlockSpec((1, d, tn), lambda l, j: (l, 0, j)),
                  pl.BlockSpec((1, 1, tn), lambda l, j: (l, 0, j))],
        out_specs=pl.BlockSpec((1, r, tn), lambda l, j: (l, 0, j)),
        out_shape=jax.ShapeDtypeStruct((depth, r, n), F32),
        compiler_params=_params(("arbitrary", "arbitrary")),
        name="ada_modulation",
    )(c_all, w_ada, b_ada.reshape(depth, 1, n))


def _rope_rotate(y, sin_signed):
    lane = lax.broadcasted_iota(jnp.int32, y.shape, 1)
    first = (lane & 63) < 32
    return jnp.where(first, pltpu.roll(y, 96, 1), pltpu.roll(y, 32, 1)) * sin_signed


def _in_proj_kernel(x_ref, g_ref, sc_ref, sh_ref, w_ref, cos_ref, sin_ref, qn_ref, kn_ref, o_ref, xn_ref,
                    *, codes, scale):
    x = x_ref[0]
    ms = jnp.mean(x * x, axis=-1, keepdims=True)
    y = x * lax.rsqrt(ms + EPS) * g_ref[...]
    xn_ref[...] = (y * (1.0 + sc_ref[0]) + sh_ref[0]).astype(BF16)
    for j, code in enumerate(codes):
        acc = jnp.dot(xn_ref[...], w_ref[:, j * IN_TN:(j + 1) * IN_TN], preferred_element_type=F32)
        for h in range(IN_TN // HEAD_DIM):
            y = acc[:, h * HEAD_DIM:(h + 1) * HEAD_DIM]
            if code & (QNORM_BIT | KNORM_BIT):
                gn_ref = qn_ref if code & QNORM_BIT else kn_ref
                ms = jnp.mean(y * y, axis=-1, keepdims=True)
                y = y * lax.rsqrt(ms + EPS) * gn_ref[...]
            if code & ROPE_BIT:
                y = y * cos_ref[...] + _rope_rotate(y, sin_ref[...])
            if code & SCALE_BIT:
                y = y * scale
            lo = j * IN_TN + h * HEAD_DIM
            o_ref[0, :, lo:lo + HEAD_DIM] = y.astype(BF16)


def in_projection(x, g, scale1, shift1, w_bf16, codes, cos, sin_signed, qn, kn, tm):
    b, n, d = x.shape
    width = w_bf16.shape[1]
    assert len(codes) * IN_TN == width
    per_batch = scale1.shape[0] > 1
    mod_map = (lambda bi, i: (bi, 0, 0)) if per_batch else (lambda bi, i: (0, 0, 0))
    const = lambda bi, i: (0, 0)
    return pl.pallas_call(
        functools.partial(_in_proj_kernel, codes=tuple(codes), scale=HEAD_DIM ** -0.5 * LOG2E),
        grid=(b, n // tm),
        in_specs=[pl.BlockSpec((1, tm, d), lambda bi, i: (bi, i, 0)),
                  pl.BlockSpec((1, d), const),
                  pl.BlockSpec((1, 1, d), mod_map),
                  pl.BlockSpec((1, 1, d), mod_map),
                  pl.BlockSpec((d, width), const, pipeline_mode=pl.Buffered(1)),
                  pl.BlockSpec((tm, HEAD_DIM), lambda bi, i: (i, 0)),
                  pl.BlockSpec((tm, HEAD_DIM), lambda bi, i: (i, 0)),
                  pl.BlockSpec((1, HEAD_DIM), const),
                  pl.BlockSpec((1, HEAD_DIM), const)],
        out_specs=pl.BlockSpec((1, tm, width), lambda bi, i: (bi, i, 0)),
        out_shape=jax.ShapeDtypeStruct((b, n, width), BF16),
        scratch_shapes=[pltpu.VMEM((tm, d), BF16)],
        compiler_params=_params(("arbitrary", "arbitrary")),
        name="in_projection",
    )(x, g.reshape(1, d), scale1, shift1, w_bf16, cos, sin_signed, qn, kn)


KEY_CHUNK = 256

class _SoftmaxChain:
    def __init__(self, q, chunks, s_ref, dv, sink=None):
        self.q, self.chunks, self.s_ref, self.sink = q, chunks, s_ref, sink
        m = q.shape[0]
        self.ones_col = dv == HEAD_DIM
        self.dv = dv
        self.mx = jnp.full((m, HEAD_DIM), NEG_INF, F32)
        self.ls = jnp.zeros((m, HEAD_DIM), F32)
        self.acc = jnp.zeros((m, 2 * HEAD_DIM if self.ones_col else dv), F32)

    def score_steps(self):
        return [functools.partial(self._score, j) for j in range(len(self.chunks))]

    def value_steps(self):
        return [functools.partial(self._value, j) for j in range(len(self.chunks))]

    def _score(self, j):
        k_fn, _, bias_fn = self.chunks[j]
        s = _nt_dot(self.q, k_fn())
        if bias_fn is not None:
            s = s + bias_fn()
        self.s_ref[:, j * KEY_CHUNK:(j + 1) * KEY_CHUNK] = s
        for t in range(KEY_CHUNK // HEAD_DIM):
            self.mx = jnp.maximum(self.mx, s[:, t * HEAD_DIM:(t + 1) * HEAD_DIM])

    def finish_scores(self):
        m = jnp.max(self.mx, axis=-1, keepdims=True)
        if self.sink is not None:
            m = jnp.maximum(m, self.sink)
        self.m = m
        self.m_wide = jnp.broadcast_to(m, self.mx.shape)

    def _value(self, j):
        _, v_fn, _ = self.chunks[j]
        parts = []
        for t in range(KEY_CHUNK // HEAD_DIM):
            lo = j * KEY_CHUNK + t * HEAD_DIM
            e = jnp.exp2(self.s_ref[:, lo:lo + HEAD_DIM] - self.m_wide)
            if not self.ones_col:
                self.ls = self.ls + e
            parts.append(e.astype(BF16))
        p = jnp.concatenate(parts, axis=1)
        v = v_fn()
        if self.ones_col:
            v = jnp.concatenate([v, jnp.ones(v.shape, BF16)], axis=1)
        self.acc = self.acc + jnp.dot(p, v, preferred_element_type=F32)

    def result(self):
        if self.ones_col:
            num, den = self.acc[:, :self.dv], self.acc[:, self.dv:]
        else:
            num, den = self.acc, jnp.sum(self.ls, axis=-1, keepdims=True)
        if self.sink is not None:
            den = den + jnp.exp2(self.sink - self.m)
        return num / den


def _run_chains(chains):
    prev = None
    for chain in chains:
        scores = chain.score_steps()
        values = prev.value_steps() if prev is not None else []
        for i in range(max(len(scores), len(values))):
            if i < len(values):
                values[i]()
            if i < len(scores):
                scores[i]()
        chain.finish_scores()
        prev = chain
    for step in prev.value_steps():
        step()


def _ref_chunks(k_ref, v_ref, start, length, bias=None, k_cols=slice(None), v_cols=slice(None)):
    assert length % KEY_CHUNK == 0
    out = []
    for j in range(length // KEY_CHUNK):
        if isinstance(start, int):
            rows = slice(start + j * KEY_CHUNK, start + (j + 1) * KEY_CHUNK)
        else:
            rows = pl.ds(start + j * KEY_CHUNK, KEY_CHUNK)
        out.append((lambda rows=rows: k_ref[0, rows, k_cols], lambda rows=rows: v_ref[0, rows, v_cols],
                    (lambda j=j: bias(j)) if bias is not None else None))
    return out


def _gqa_kernel(*refs, rep, groups, tq, mode, seq, use_sink):
    if use_sink:
        sink_ref, refs = refs[0], refs[1:]
    s_ref = refs[-1]
    refs = refs[:-1]
    if mode == 'none':
        q_ref, kx_ref, vx_ref, o_ref = refs
    else:
        q_ref, kx_ref, vx_ref, kl_ref, vl_ref, o_ref = refs
    gi = pl.program_id(1)
    qi = pl.program_id(2)
    bias = None
    if mode == 'band':
        band = tq + 2 * A_WINDOW
        start = pl.multiple_of(jnp.clip(qi * tq - A_WINDOW, 0, seq - band), HEAD_DIM)
        rel = (lax.broadcasted_iota(jnp.int32, (tq, KEY_CHUNK), 0)
               - lax.broadcasted_iota(jnp.int32, (tq, KEY_CHUNK), 1))

        def bias(j):
            dist = rel + (qi * tq - start - j * KEY_CHUNK)
            return jnp.where(jnp.abs(dist) <= A_WINDOW, 0.0, NEG_INF)

    chains = []
    for g in range(groups):
        kv_cols = slice(g * HEAD_DIM, (g + 1) * HEAD_DIM)
        chunks = _ref_chunks(kx_ref, vx_ref, 0, kx_ref.shape[1], k_cols=kv_cols, v_cols=kv_cols)
        if mode == 'full':
            chunks += _ref_chunks(kl_ref, vl_ref, 0, seq, k_cols=kv_cols, v_cols=kv_cols)
        elif mode == 'band':
            chunks += _ref_chunks(kl_ref, vl_ref, start, band, bias, k_cols=kv_cols, v_cols=kv_cols)
        for r in range(rep):
            head = g * rep + r
            q = q_ref[0, :, head * HEAD_DIM:(head + 1) * HEAD_DIM]
            sink = sink_ref[gi * groups * rep + head] * LOG2E if use_sink else None
            chains.append(_SoftmaxChain(q, chunks, s_ref.at[head], HEAD_DIM, sink=sink))
    _run_chains(chains)
    for head, chain in enumerate(chains):
        o_ref[0, :, head * HEAD_DIM:(head + 1) * HEAD_DIM] = chain.result().astype(BF16)


def gqa_attention(p_q, p_ctx, p_lat, *, kv_heads, rep, q_col, k_col, v_col, tq, mode, sink=None, groups=1):
    b, nq, _ = p_q.shape
    c = p_ctx.shape[1]
    qw = groups * rep * HEAD_DIM
    kw = groups * HEAD_DIM
    use_sink = sink is not None
    assert kv_heads % groups == 0 and q_col % qw == 0 and k_col % kw == 0 and v_col % kw == 0
    in_specs = [pl.BlockSpec((1, tq, qw), lambda bi, g, qi, *_: (bi, qi, q_col // qw + g)),
                pl.BlockSpec((1, c, kw), lambda bi, g, qi, *_: (bi, 0, k_col // kw + g)),
                pl.BlockSpec((1, c, kw), lambda bi, g, qi, *_: (bi, 0, v_col // kw + g))]
    args = [p_q, p_ctx, p_ctx]
    seq = 0
    if mode != 'none':
        seq = p_lat.shape[1]
        in_specs += [pl.BlockSpec((1, seq, kw), lambda bi, g, qi, *_: (bi, 0, k_col // kw + g)),
                     pl.BlockSpec((1, seq, kw), lambda bi, g, qi, *_: (bi, 0, v_col // kw + g))]
        args += [p_lat, p_lat]
    if use_sink:
        args = [sink] + args
    n_keys = c + {'none': 0, 'full': seq, 'band': tq + 2 * A_WINDOW}[mode]
    grid_spec = pltpu.PrefetchScalarGridSpec(
        num_scalar_prefetch=1 if use_sink else 0,
        grid=(b, kv_heads // groups, nq // tq),
        in_specs=in_specs,
        out_specs=pl.BlockSpec((1, tq, qw), lambda bi, g, qi, *_: (bi, qi, g)),
        scratch_shapes=[pltpu.VMEM((groups * rep, tq, n_keys), F32)])
    return pl.pallas_call(
        functools.partial(_gqa_kernel, rep=rep, groups=groups, tq=tq, mode=mode, seq=seq, use_sink=use_sink),
        grid_spec=grid_spec,
        out_shape=jax.ShapeDtypeStruct((b, nq, kv_heads * rep * HEAD_DIM), BF16),
        compiler_params=_params(("arbitrary", "arbitrary", "arbitrary")),
        name=f"gqa_attention_{mode}",
    )(*args)


def _diff_kernel(*refs, has_lat, lam_init):
    if has_lat:
        lq1, lk1, lq2, lk2, sub_ref, q_ref, kx_ref, vx_ref, kl_ref, vl_ref, o_ref, s_ref = refs
    else:
        lq1, lk1, lq2, lk2, sub_ref, q_ref, kx_ref, vx_ref, o_ref, s_ref = refs
    lam = (jnp.exp(jnp.sum(lq1[...] * lk1[...], axis=-1, keepdims=True))
           - jnp.exp(jnp.sum(lq2[...] * lk2[...], axis=-1, keepdims=True)) + lam_init)
    n_ctx = vx_ref.shape[1]
    dv = 2 * HEAD_DIM
    chains = []
    for h in range(DIFF_HEADS_PER_STEP):
        v_cols = slice(h * dv, (h + 1) * dv)
        for m in range(2):
            k_cols = slice((2 * h + m) * HEAD_DIM, (2 * h + m + 1) * HEAD_DIM)
            chunks = _ref_chunks(kx_ref, vx_ref, 0, n_ctx, k_cols=k_cols, v_cols=v_cols)
            if has_lat:
                chunks += _ref_chunks(kl_ref, vl_ref, 0, vl_ref.shape[1], k_cols=k_cols, v_cols=v_cols)
            chains.append(_SoftmaxChain(q_ref[0, :, k_cols], chunks, s_ref.at[2 * h + m], dv))
    _run_chains(chains)
    for h in range(DIFF_HEADS_PER_STEP):
        o = chains[2 * h].result() - lam * chains[2 * h + 1].result()
        ms = jnp.mean(o * o, axis=-1, keepdims=True)
        o_ref[0, :, h * dv:(h + 1) * dv] = (o * lax.rsqrt(ms + EPS) * sub_ref[...] * (1.0 - lam_init)).astype(BF16)


DIFF_HEADS_PER_STEP = 2


def diff_attention(p_q, p_ctx, p_lat, lam_vecs, subln, *, lam_init, tq, has_lat):
    b, nq, _ = p_q.shape
    c = p_ctx.shape[1]
    qb = (A_HEADS + 2 * A_KV_HEADS) * HEAD_DIM
    kb = qb + B_HEADS * 2 * HEAD_DIM
    vb = kb + B_HEADS * 2 * HEAD_DIM
    dv = 2 * HEAD_DIM
    wide = DIFF_HEADS_PER_STEP * dv
    assert qb % wide == 0 and kb % wide == 0 and vb % wide == 0 and B_HEADS % DIFF_HEADS_PER_STEP == 0
    vec = pl.BlockSpec((1, HEAD_DIM), lambda bi, h, qi: (0, 0))

    def cols(n, base):
        return pl.BlockSpec((1, n, wide), lambda bi, h, qi: (bi, 0, base // wide + h))

    in_specs = [vec, vec, vec, vec, pl.BlockSpec((1, dv), lambda bi, h, qi: (0, 0)),
                pl.BlockSpec((1, tq, wide), lambda bi, h, qi: (bi, qi, qb // wide + h)),
                cols(c, kb), cols(c, vb)]
    args = [v.reshape(1, HEAD_DIM) for v in lam_vecs] + [subln.reshape(1, dv), p_q, p_ctx, p_ctx]
    seq = 0
    if has_lat:
        seq = p_lat.shape[1]
        in_specs += [cols(seq, kb), cols(seq, vb)]
        args += [p_lat, p_lat]
    return pl.pallas_call(
        functools.partial(_diff_kernel, has_lat=has_lat, lam_init=lam_init),
        grid=(b, B_HEADS // DIFF_HEADS_PER_STEP, nq // tq),
        in_specs=in_specs,
        out_specs=pl.BlockSpec((1, tq, wide), lambda bi, h, qi: (bi, qi, h)),
        out_shape=jax.ShapeDtypeStruct((b, nq, B_HEADS * dv), BF16),
        scratch_shapes=[pltpu.VMEM((2 * DIFF_HEADS_PER_STEP, tq, c + seq), F32)],
        compiler_params=_params(("arbitrary", "arbitrary", "arbitrary")),
        name="diff_attention",
    )(*args)


NA_QROWS = 4
NA_WROWS = NA_QROWS + NA_ROWS
NA_BLOCKS_PER_STEP = 8


def _natten_kernel(q_ref, kx_ref, vx_ref, kl_ref, vl_ref, tb_ref, o_ref, s_ref, *, n_rows):
    n_blocks = n_rows // NA_QROWS
    tq = NA_QROWS * GRID_W
    n_ctx = kx_ref.shape[1]
    chains = []
    for i in range(NA_BLOCKS_PER_STEP):
        blk = pl.program_id(2) * NA_BLOCKS_PER_STEP + i
        first_row = jnp.clip(blk * NA_QROWS - NA_ROWS // 2, 0, n_rows - NA_WROWS)
        start = pl.multiple_of(first_row * GRID_W, GRID_W)
        variant = jnp.where(blk == 0, 0, jnp.where(blk == n_blocks - 1, 2, 1))

        def bias(j, variant=variant):
            return tb_ref[0, variant, :, j * KEY_CHUNK:(j + 1) * KEY_CHUNK]

        chunks = (_ref_chunks(kx_ref, vx_ref, 0, n_ctx)
                  + _ref_chunks(kl_ref, vl_ref, start, NA_WROWS * GRID_W, bias))
        chains.append(_SoftmaxChain(q_ref[0, i * tq:(i + 1) * tq, :], chunks, s_ref.at[i], HEAD_DIM))
    _run_chains(chains)
    for i, chain in enumerate(chains):
        o_ref[0, i * tq:(i + 1) * tq, :] = chain.result().astype(BF16)


def natten_bias_slabs(rpb):
    col = jnp.arange(GRID_W)
    cs = jnp.clip(col - NA_COLS // 2, 0, GRID_W - NA_COLS)
    col_mask = (col[None, :] >= cs[:, None]) & (col[None, :] < cs[:, None] + NA_COLS)
    col_idx = jnp.clip(col[None, :] - col[:, None], -(NA_COLS - 1), NA_COLS - 1) + NA_COLS - 1
    rpb_c = jnp.where(col_mask[None, None], rpb[:, :, col_idx], NEG_INF)
    i = jnp.arange(NA_QROWS)[:, None]
    a = jnp.arange(NA_WROWS)[None, :]
    half = NA_ROWS // 2
    kinds = [(a < NA_ROWS, a - i + NA_ROWS - 1),
             ((a >= i) & (a < i + NA_ROWS), a - i - half + NA_ROWS - 1),
             (a >= NA_WROWS - NA_ROWS, a - i - NA_ROWS + NA_ROWS - 1)]
    slabs = []
    for valid, roff in kinds:
        t = rpb_c[:, jnp.clip(roff, 0, 2 * NA_ROWS - 2)] * LOG2E
        t = jnp.where(valid[None, :, :, None, None], t, NEG_INF)
        slabs.append(jnp.transpose(t, (0, 1, 3, 2, 4)).reshape(rpb.shape[0], NA_QROWS * GRID_W, NA_WROWS * GRID_W))
    return jnp.stack(slabs, axis=1).astype(F32)


def natten_attention(p_lat, p_ctx, slabs):
    b, seq, _ = p_lat.shape
    c = p_ctx.shape[1]
    n_rows = seq // GRID_W
    rows_per_step = NA_QROWS * NA_BLOCKS_PER_STEP
    assert n_rows >= NA_WROWS and n_rows % rows_per_step == 0
    qd = (C_HEADS + 2 * C_KV_HEADS) * HEAD_DIM
    kd = qd + D_HEADS * HEAD_DIM
    vd = kd + D_HEADS * HEAD_DIM
    tq = rows_per_step * GRID_W

    def col(n, base):
        return pl.BlockSpec((1, n, HEAD_DIM), lambda bi, h, i: (bi, 0, base // HEAD_DIM + h))

    in_specs = [pl.BlockSpec((1, tq, HEAD_DIM), lambda bi, h, i: (bi, i, qd // HEAD_DIM + h)),
                col(c, kd), col(c, vd), col(seq, kd), col(seq, vd),
                pl.BlockSpec((1, 3, NA_QROWS * GRID_W, NA_WROWS * GRID_W), lambda bi, h, i: (h, 0, 0, 0))]
    return pl.pallas_call(
        functools.partial(_natten_kernel, n_rows=n_rows),
        grid=(b, D_HEADS, n_rows // rows_per_step),
        in_specs=in_specs,
        out_specs=pl.BlockSpec((1, tq, HEAD_DIM), lambda bi, h, i: (bi, i, h)),
        out_shape=jax.ShapeDtypeStruct((b, seq, D_HEADS * HEAD_DIM), BF16),
        scratch_shapes=[pltpu.VMEM((NA_BLOCKS_PER_STEP, NA_QROWS * GRID_W, c + NA_WROWS * GRID_W), F32)],
        compiler_params=_params(("arbitrary", "arbitrary", "arbitrary")),
        name="natten_attention",
    )(p_lat, p_ctx, p_ctx, p_lat, p_lat, slabs)


def _out_proj_kernel(o1_ref, o2_ref, w_ref, x_ref, gt_ref, g_ref, sc_ref, sh_ref, wr_ref, xo_ref, h_ref, aff_ref,
                     *, n_experts, sub_rows):
    for r0 in range(0, x_ref.shape[1], sub_rows):
        rows = slice(r0, r0 + sub_rows)
        y = (jnp.dot(o1_ref[0, rows, :], w_ref[0], preferred_element_type=F32)
             + jnp.dot(o2_ref[0, rows, :], w_ref[1], preferred_element_type=F32))
        x = x_ref[0, rows, :] + gt_ref[0] * y
        xo_ref[0, rows, :] = x
        ms = jnp.mean(x * x, axis=-1, keepdims=True)
        h = x * lax.rsqrt(ms + EPS) * g_ref[...]
        h = h * (1.0 + sc_ref[0]) + sh_ref[0]
        h_ref[0, rows, :] = h
        logits = jnp.dot(h.astype(BF16), wr_ref[...], preferred_element_type=F32)
        lane = lax.broadcasted_iota(jnp.int32, logits.shape, 1)
        logits = jnp.where(lane < n_experts, logits, NEG_INF)
        e = jnp.exp(logits - jnp.max(logits, axis=-1, keepdims=True))
        aff_ref[0, rows, :] = e / jnp.sum(e, axis=-1, keepdims=True)


def out_projection(o1, o2, w_bf16, x, gate1, g_ffn, scale2, shift2, w_router_pad, n_experts, tm):
    b, n, d = x.shape
    k = o1.shape[2]
    assert o2.shape[2] == k and w_bf16.shape == (2, k, d)
    per_batch = gate1.shape[0] > 1
    mod_map = (lambda bi, i: (bi, 0, 0)) if per_batch else (lambda bi, i: (0, 0, 0))
    row = lambda bi, i: (bi, i, 0)
    return pl.pallas_call(
        functools.partial(_out_proj_kernel, n_experts=n_experts, sub_rows=min(tm, 256)),
        grid=(b, n // tm),
        in_specs=[pl.BlockSpec((1, tm, k), row),
                  pl.BlockSpec((1, tm, k), row),
                  pl.BlockSpec((2, k, d), lambda bi, i: (0, 0, 0), pipeline_mode=pl.Buffered(1)),
                  pl.BlockSpec((1, tm, d), row),
                  pl.BlockSpec((1, 1, d), mod_map),
                  pl.BlockSpec((1, d), lambda bi, i: (0, 0)),
                  pl.BlockSpec((1, 1, d), mod_map),
                  pl.BlockSpec((1, 1, d), mod_map),
                  pl.BlockSpec((d, ROUTER_LANES), lambda bi, i: (0, 0))],
        out_specs=[pl.BlockSpec((1, tm, d), row), pl.BlockSpec((1, tm, d), row),
                   pl.BlockSpec((1, tm, ROUTER_LANES), row)],
        out_shape=[jax.ShapeDtypeStruct((b, n, d), F32), jax.ShapeDtypeStruct((b, n, d), F32),
                   jax.ShapeDtypeStruct((b, n, ROUTER_LANES), F32)],
        compiler_params=_params(("arbitrary", "arbitrary")),
        name="out_projection",
    )(o1, o2, w_bf16, x, gate1, g_ffn.reshape(1, d), scale2, shift2, w_router_pad)


MOE_ROW_CHUNK = 1024


SUBLANES = 8


MOE_ROWS_PER_STEP = 512


def _experts_per_step(n_experts, cap):
    per_step = max(1, min(n_experts, MOE_ROWS_PER_STEP // cap))
    assert n_experts % per_step == 0 and cap % SUBLANES == 0
    return per_step


def _split_rows(idx):
    b, e, cap = idx.shape
    return (idx // SUBLANES).reshape(b, e, 1, cap), (idx % SUBLANES).reshape(b, e, 1, cap)


def _gather_kernel(hi_ref, lo_ref, *refs, cap, dc):
    h_refs, o_ref, stage_ref = refs[:-2], refs[-2], refs[-1]

    def body(i, carry):
        for r in range(SUBLANES):
            c = i * SUBLANES + r
            hi, lo = hi_ref[0, 0, 0, c], lo_ref[0, 0, 0, c]
            for k, h_ref in enumerate(h_refs):
                stage_ref[i, r:r + 1, k * dc:(k + 1) * dc] = h_ref[0, hi, pl.ds(lo, 1), :]
        return carry
    lax.fori_loop(0, cap // SUBLANES, body, 0, unroll=4)
    o_ref[0, 0] = stage_ref[...].reshape(cap, stage_ref.shape[2]).astype(BF16)


def moe_gather(h, idx, dc):
    b, n, d = h.shape
    n_experts, cap_e = idx.shape[1:]
    per_step = _experts_per_step(n_experts, cap_e)
    idx = idx.reshape(b, n_experts // per_step, per_step * cap_e)
    _, e, cap = idx.shape
    assert n % SUBLANES == 0
    n_chunks = d // dc
    h_specs = [pl.BlockSpec((1, n // SUBLANES, SUBLANES, dc), lambda bi, ei, k=k: (bi, 0, 0, k),
                            pipeline_mode=pl.Buffered(1)) for k in range(n_chunks)]
    smem = pl.BlockSpec((1, 1, 1, cap), lambda bi, ei: (bi, ei, 0, 0), memory_space=pltpu.SMEM)
    h_tiles = h.reshape(b, n // SUBLANES, SUBLANES, d)
    return pl.pallas_call(
        functools.partial(_gather_kernel, cap=cap, dc=dc),
        grid=(b, e),
        in_specs=[smem, smem] + h_specs,
        out_specs=pl.BlockSpec((1, 1, cap, d), lambda bi, ei: (bi, ei, 0, 0)),
        out_shape=jax.ShapeDtypeStruct((b, e, cap, d), BF16),
        scratch_shapes=[pltpu.VMEM((cap // SUBLANES, SUBLANES, d), F32)],
        compiler_params=_params(("arbitrary", "arbitrary")),
        name="moe_gather",
    )(*_split_rows(idx), *([h_tiles] * n_chunks)).reshape(b, n_experts, cap_e, d)


EXPERT_F_CHUNK = 256


def _expert_kernel(*refs, n_chunks, step_ranges):
    n_groups = len(step_ranges)
    in_refs = refs[:2 * n_groups]
    wg_ref, wu_ref, wd_ref = refs[2 * n_groups:2 * n_groups + 3]
    out_refs = refs[2 * n_groups + 3:3 * n_groups + 3]
    wg_s, wu_s, wd_s = refs[-3:]
    step = pl.program_id(1)

    @pl.when(step < n_chunks)
    def _():
        wg_s[step] = wg_ref[0, 0].astype(BF16)
        wu_s[step] = wu_ref[0, 0].astype(BF16)
        wd_s[step] = wd_ref[0, 0].astype(BF16)

    for g, (lo, hi) in enumerate(step_ranges):
        xe_ref, gate_ref, o_ref = in_refs[2 * g], in_refs[2 * g + 1], out_refs[g]

        @pl.when((step >= lo) & (step < hi))
        def _(xe_ref=xe_ref, gate_ref=gate_ref, o_ref=o_ref):
            bs, _, cap, d = xe_ref.shape
            xe = xe_ref[...].reshape(bs * cap, d)
            ye = jnp.zeros((bs * cap, d), F32)
            for k in range(n_chunks):
                a = jnp.dot(xe, wg_s[k], preferred_element_type=F32)
                u = jnp.dot(xe, wu_s[k], preferred_element_type=F32)
                hid = (a * (1.0 / (1.0 + jnp.exp(-a))) * u).astype(BF16)
                ye = ye + jnp.dot(hid, wd_s[k], preferred_element_type=F32)
            o_ref[...] = (ye * gate_ref[...].reshape(bs * cap, 1)).reshape(bs, 1, cap, d)


def moe_experts(groups, wg, wu, wd, layer):
    _, e, d, f = wg.shape
    fc = math.gcd(f, EXPERT_F_CHUNK)
    n_chunks = f // fc
    in_specs, args, out_specs, out_shapes, step_ranges = [], [], [], [], []
    lo = n_chunks
    for xe, gate in groups:
        b, _, cap, _ = xe.shape
        bs = max(1, min(b, 512 // cap))
        assert b % bs == 0
        nb = b // bs
        tile = lambda ei, s, lo=lo, nb=nb: (jnp.clip(s - lo, 0, nb - 1), ei, 0, 0)
        in_specs += [pl.BlockSpec((bs, 1, cap, d), tile), pl.BlockSpec((bs, 1, cap, 1), tile)]
        args += [xe, gate]
        out_specs.append(pl.BlockSpec((bs, 1, cap, d), tile))
        out_shapes.append(jax.ShapeDtypeStruct((b, e, cap, d), F32))
        step_ranges.append((lo, lo + nb))
        lo += nb
    chunk = lambda ei, s: jnp.minimum(s, n_chunks - 1)
    in_specs += [pl.BlockSpec((1, 1, d, fc), lambda ei, s: (layer, ei, 0, chunk(ei, s))),
                 pl.BlockSpec((1, 1, d, fc), lambda ei, s: (layer, ei, 0, chunk(ei, s))),
                 pl.BlockSpec((1, 1, fc, d), lambda ei, s: (layer, ei, chunk(ei, s), 0))]
    return pl.pallas_call(
        functools.partial(_expert_kernel, n_chunks=n_chunks, step_ranges=tuple(step_ranges)),
        grid=(e, lo),
        in_specs=in_specs,
        out_specs=out_specs,
        out_shape=out_shapes,
        scratch_shapes=[pltpu.VMEM((n_chunks, d, fc), BF16), pltpu.VMEM((n_chunks, d, fc), BF16),
                        pltpu.VMEM((n_chunks, fc, d), BF16)],
        compiler_params=_params(("arbitrary", "arbitrary")),
        name="moe_experts",
    )(*args, wg, wu, wd)


def _combine_kernel(hi_ref, lo_ref, ye_ref, x_ref, gt_ref, *refs, cap, n_experts, rows_out, final_norm):
    gf_ref = refs[0] if final_norm else None
    o_ref, acc_ref = refs[-2:]
    step = pl.program_id(1)

    @pl.when(step == 0)
    def _():
        acc_ref[...] = jnp.zeros_like(acc_ref)

    @pl.when(step < n_experts)
    def _():
        def body(i, carry):
            rows = [(hi_ref[0, 0, 0, i * SUBLANES + r], lo_ref[0, 0, 0, i * SUBLANES + r])
                    for r in range(SUBLANES)]
            old = [acc_ref[hi, pl.ds(lo, 1), :] for hi, lo in rows]
            for r, (hi, lo) in enumerate(rows):
                acc_ref[hi, pl.ds(lo, 1), :] = old[r] + ye_ref[0, 0, i, r:r + 1, :]
            return carry
        lax.fori_loop(0, cap // SUBLANES, body, 0, unroll=4)

    @pl.when(step >= n_experts)
    def _():
        tiles_out = rows_out // SUBLANES
        acc = acc_ref[pl.ds((step - n_experts) * tiles_out, tiles_out)]
        x = x_ref[0] + gt_ref[0] * acc.reshape(rows_out, acc.shape[2])
        if final_norm:
            ms = jnp.mean(x * x, axis=-1, keepdims=True)
            x = x * lax.rsqrt(ms + EPS) * gf_ref[...]
        o_ref[0] = x


def moe_combine(ye, idx, x, gate2, final_gain=None):
    b, n, d = x.shape
    final_norm = final_gain is not None
    extra_specs = [pl.BlockSpec((1, d), lambda bi, s: (0, 0))] if final_norm else []
    extra_args = [final_gain.reshape(1, d)] if final_norm else []
    n_experts, cap_e = idx.shape[1:]
    per_step = _experts_per_step(n_experts, cap_e)
    idx = idx.reshape(b, n_experts // per_step, per_step * cap_e)
    _, e, cap = idx.shape
    rows_out = min(n, 256)
    per_batch = gate2.shape[0] > 1
    mod_map = (lambda bi, s: (bi, 0, 0)) if per_batch else (lambda bi, s: (0, 0, 0))
    assert n % SUBLANES == 0 and rows_out % SUBLANES == 0
    expert = lambda bi, s: (bi, jnp.minimum(s, e - 1), 0, 0)
    expert_rows = lambda bi, s: (bi, jnp.minimum(s, e - 1), 0, 0, 0)
    tile = lambda bi, s: (bi, jnp.maximum(s - e, 0), 0)
    smem = pl.BlockSpec((1, 1, 1, cap), expert, memory_space=pltpu.SMEM)
    return pl.pallas_call(
        functools.partial(_combine_kernel, cap=cap, n_experts=e, rows_out=rows_out, final_norm=final_norm),
        grid=(b, e + n // rows_out),
        in_specs=[smem, smem,
                  pl.BlockSpec((1, 1, cap // SUBLANES, SUBLANES, d), expert_rows),
                  pl.BlockSpec((1, rows_out, d), tile),
                  pl.BlockSpec((1, 1, d), mod_map)] + extra_specs,
        out_specs=pl.BlockSpec((1, rows_out, d), tile),
        out_shape=jax.ShapeDtypeStruct((b, n, d), F32),
        scratch_shapes=[pltpu.VMEM((n // SUBLANES, SUBLANES, d), F32)],
        compiler_params=_params(("arbitrary", "arbitrary")),
        name="moe_combine",
    )(*_split_rows(idx), ye.reshape(b, e, cap // SUBLANES, SUBLANES, d), x, gate2, *extra_args)


def moe_dispatch(h, aff, n_experts):
    b, n, d = h.shape
    cap = max(1, EC_CAPACITY_FACTOR * n // n_experts)
    gate, idx = lax.top_k(jnp.swapaxes(aff[..., :n_experts], 1, 2), cap)
    return idx, gate[..., None], moe_gather(h, idx, min(d, MOE_ROW_CHUNK))


def _rope_tables(n):
    t = jnp.arange(n, dtype=jnp.int32)
    row = (t // GRID_W).astype(F32)
    col = (t % GRID_W).astype(F32)
    n_freq = HEAD_DIM // 4
    inv = ROPE_THETA ** (-jnp.arange(n_freq, dtype=F32) / n_freq)
    ar = row[:, None] * inv
    ac = col[:, None] * inv
    ang = jnp.concatenate([ar, ar, ac, ac], axis=-1)
    sign = jnp.where((jnp.arange(HEAD_DIM) % (2 * n_freq)) < n_freq, -1.0, 1.0).astype(F32)
    return jnp.cos(ang), jnp.sin(ang) * sign


def _tile_codes(layer, rope):
    if layer % 2 == 0:
        groups = [(A_HEADS * HEAD_DIM, ROPE_BIT | SCALE_BIT), (A_KV_HEADS * HEAD_DIM, ROPE_BIT),
                  (A_KV_HEADS * HEAD_DIM, 0), (B_HEADS * 2 * HEAD_DIM, ROPE_BIT | SCALE_BIT),
                  (B_HEADS * 2 * HEAD_DIM, ROPE_BIT), (B_HEADS * 2 * HEAD_DIM, 0)]
    else:
        groups = [(C_HEADS * HEAD_DIM, QNORM_BIT | ROPE_BIT | SCALE_BIT), (C_KV_HEADS * HEAD_DIM, KNORM_BIT | ROPE_BIT),
                  (C_KV_HEADS * HEAD_DIM, 0), (D_HEADS * HEAD_DIM, SCALE_BIT), (D_HEADS * HEAD_DIM, 0),
                  (D_HEADS * HEAD_DIM, 0)]
    codes = []
    for width, code in groups:
        assert width % IN_TN == 0
        codes += [code if rope else code & ~ROPE_BIT] * (width // IN_TN)
    return codes


def _lambda_init(layer):
    return 0.8 - 0.6 * math.exp(-0.3 * layer)


def _row_tile(n, target):
    t = min(n, target)
    assert n % t == 0
    return t


def kernel(x, c, ctx, c_ctx, w_ada, b_ada, g_mix, g_ffn, w_in, w_out, a_sink, b_lam_q1, b_lam_k1, b_lam_q2,
           b_lam_k2, b_subln, c_q_norm, c_k_norm, d_rpb, w_router, w_gate, w_up, w_down, g_final):
    bsz, seq, d = x.shape
    n_ctx = ctx.shape[1]
    depth = w_ada.shape[0]
    n_experts = w_router.shape[2]
    cos, sin_signed = _rope_tables(seq)
    ones_head = jnp.ones((1, HEAD_DIM), F32)

    n_rows = -(-(bsz + 1) // 8) * 8
    c_all = jnp.zeros((n_rows, d), F32).at[:bsz].set(c).at[bsz].set(c_ctx)
    mod = ada_modulation(c_all, w_ada, b_ada)

    xc = ctx
    for l in range(depth):
        need_ctx = l < depth - 1
        e = l // 2
        m_lat = mod[l, :bsz].reshape(bsz, 1, 6, d)
        m_ctx = mod[l, bsz:bsz + 1].reshape(1, 1, 6, d)
        sh1, sc1, gt1, sh2, sc2, gt2 = (m_lat[:, :, i] for i in range(6))
        sh1x, sc1x, gt1x, sh2x, sc2x, gt2x = (m_ctx[:, :, i] for i in range(6))
        w_in_l = w_in[l].astype(BF16)
        w_out_l = w_out[l].astype(BF16).reshape(2, MIX_WIDTH // 2, d)
        if l % 2 == 0:
            qn = kn = ones_head
        else:
            qn, kn = c_q_norm[e].reshape(1, HEAD_DIM), c_k_norm[e].reshape(1, HEAD_DIM)
        tm_lat = _row_tile(seq, 512)
        tm_ctx = _row_tile(n_ctx, 512)
        p = in_projection(x, g_mix[l], sc1, sh1, w_in_l, _tile_codes(l, True), cos, sin_signed, qn, kn, tm_lat)
        px = in_projection(xc, g_mix[l], sc1x, sh1x, w_in_l, _tile_codes(l, False), cos[:tm_ctx],
                           sin_signed[:tm_ctx], qn, kn, tm_ctx)
        ox1 = ox2 = None
        if l % 2 == 0:
            lam_vecs = (b_lam_q1[e], b_lam_k1[e], b_lam_q2[e], b_lam_k2[e])
            ka = A_HEADS * HEAD_DIM
            va = ka + A_KV_HEADS * HEAD_DIM
            a_cfg = dict(kv_heads=A_KV_HEADS, rep=A_HEADS // A_KV_HEADS, q_col=0, k_col=ka, v_col=va,
                         sink=a_sink[e], groups=A_KV_HEADS)
            o1 = gqa_attention(p, px, p, tq=_row_tile(seq, 256), mode='band', **a_cfg)
            o2 = diff_attention(p, px, p, lam_vecs, b_subln[e], lam_init=_lambda_init(l),
                                tq=_row_tile(seq, 256), has_lat=True)
            if need_ctx:
                ox1 = gqa_attention(px, px, None, tq=_row_tile(n_ctx, 256), mode='none', **a_cfg)
                ox2 = diff_attention(px, px, None, lam_vecs, b_subln[e], lam_init=_lambda_init(l),
                                     tq=_row_tile(n_ctx, 256), has_lat=False)
        else:
            kc = C_HEADS * HEAD_DIM
            vc = kc + C_KV_HEADS * HEAD_DIM
            c_cfg = dict(kv_heads=C_KV_HEADS, rep=C_HEADS // C_KV_HEADS, q_col=0, k_col=kc, v_col=vc,
                         groups=C_KV_HEADS)
            o1 = gqa_attention(p, px, p, tq=_row_tile(seq, 256), mode='full', **c_cfg)
            o2 = natten_attention(p, px, natten_bias_slabs(d_rpb[e]))
            if need_ctx:
                qd = vc + C_KV_HEADS * HEAD_DIM
                kd = qd + D_HEADS * HEAD_DIM
                vd = kd + D_HEADS * HEAD_DIM
                ox1 = gqa_attention(px, px, None, tq=_row_tile(n_ctx, 256), mode='none', **c_cfg)
                ox2 = gqa_attention(px, px, None, kv_heads=D_HEADS, rep=1, q_col=qd, k_col=kd, v_col=vd,
                                    tq=_row_tile(n_ctx, 256), mode='none')
        w_router_pad = jnp.zeros((d, ROUTER_LANES), BF16).at[:, :n_experts].set(w_router[l].astype(BF16))
        x1, h, aff = out_projection(o1, o2, w_out_l, x, gt1, g_ffn[l], sc2, sh2, w_router_pad, n_experts,
                                    _row_tile(seq, 512))
        idx, gate, xe = moe_dispatch(h, aff, n_experts)
        groups = [(xe, gate)]
        if need_ctx:
            xc1, hx, affx = out_projection(ox1, ox2, w_out_l, xc, gt1x, g_ffn[l], sc2x, sh2x, w_router_pad,
                                           n_experts, _row_tile(n_ctx, 256))
            idx_x, gate_x, xe_x = moe_dispatch(hx, affx, n_experts)
            groups.append((xe_x, gate_x))
        ye = moe_experts(groups, w_gate, w_up, w_down, l)
        x = moe_combine(ye[0], idx, x1, gt2, g_final if l == depth - 1 else None)
        if need_ctx:
            xc = moe_combine(ye[1], idx_x, xc1, gt2x)
    return x
```

```python
import functools
import math

import jax
import jax.numpy as jnp
from jax import lax
from jax.experimental import pallas as pl
from jax.experimental.pallas import tpu as pltpu

F32 = jnp.float32
BF16 = jnp.bfloat16

HEAD_DIM = 128
GRID_W = 64
ROPE_THETA = 10000.0
EPS = 1e-6
NEG_INF = -1e30
LOG2E = math.log2(math.e)
A_HEADS, A_KV_HEADS, A_WINDOW = 8, 2, 128
B_HEADS = 4
C_HEADS, C_KV_HEADS = 8, 2
D_HEADS, NA_ROWS, NA_COLS = 8, 8, 16
EC_CAPACITY_FACTOR = 2
IN_WIDTH = 4608
MIX_WIDTH = 2048
ROUTER_LANES = 128

ROPE_BIT, QNORM_BIT, KNORM_BIT, SCALE_BIT = 1, 2, 4, 8
IN_TN = 256

V7X_VMEM_LIMIT = 56 * 1024 * 1024


def _params(sem, vmem=V7X_VMEM_LIMIT):
    return pltpu.CompilerParams(dimension_semantics=sem, vmem_limit_bytes=vmem)


def _nt_dot(a, b):
    return lax.dot_general(a, b, (((1,), (1,)), ((), ())), preferred_element_type=F32)


def _ada_kernel(c_ref, w_ref, b_ref, o_ref):
    c = c_ref[...]
    cond = c * (1.0 / (1.0 + jnp.exp(-c)))
    o_ref[0] = jnp.dot(cond.astype(BF16), w_ref[0].astype(BF16), preferred_element_type=F32) + b_ref[0]


def ada_modulation(c_all, w_ada, b_ada):
    depth, d, n = w_ada.shape
    r = c_all.shape[0]
    tn = math.gcd(n, 1024)
    return pl.pallas_call(
        _ada_kernel,
        grid=(depth, n // tn),
        in_specs=[pl.BlockSpec((r, d), lambda l, j: (0, 0)),
                  pl.BlockSpec((1, d, tn), lambda l, j: (l, 0, j)),
                  pl.BlockSpec((1, 1, tn), lambda l, j: (l, 0, j))],
        out_specs=pl.BlockSpec((1, r, tn), lambda l, j: (l, 0, j)),
        out_shape=jax.ShapeDtypeStruct((depth, r, n), F32),
        compiler_params=_params(("arbitrary", "arbitrary")),
        name="ada_modulation",
    )(c_all, w_ada, b_ada.reshape(depth, 1, n))


def _rope_rotate(y, sin_signed):
    lane = lax.broadcasted_iota(jnp.int32, y.shape, 1)
    first = (lane & 63) < 32
    return jnp.where(first, pltpu.roll(y, 96, 1), pltpu.roll(y, 32, 1)) * sin_signed


def _in_proj_kernel(x_ref, g_ref, sc_ref, sh_ref, w_ref, cos_ref, sin_ref, qn_ref, kn_ref, o_ref, xn_ref,
                    *, codes, scale):
    x = x_ref[0]
    ms = jnp.mean(x * x, axis=-1, keepdims=True)
    y = x * lax.rsqrt(ms + EPS) * g_ref[...]
    xn_ref[...] = (y * (1.0 + sc_ref[0]) + sh_ref[0]).astype(BF16)
    for j, code in enumerate(codes):
        acc = jnp.dot(xn_ref[...], w_ref[:, j * IN_TN:(j + 1) * IN_TN], preferred_element_type=F32)
        for h in range(IN_TN // HEAD_DIM):
            y = acc[:, h * HEAD_DIM:(h + 1) * HEAD_DIM]
            if code & (QNORM_BIT | KNORM_BIT):
                gn_ref = qn_ref if code & QNORM_BIT else kn_ref
                ms = jnp.mean(y * y, axis=-1, keepdims=True)
                y = y * lax.rsqrt(ms + EPS) * gn_ref[...]
            if code & ROPE_BIT:
                y = y * cos_ref[...] + _rope_rotate(y, sin_ref[...])
            if code & SCALE_BIT:
                y = y * scale
            lo = j * IN_TN + h * HEAD_DIM
            o_ref[0, :, lo:lo + HEAD_DIM] = y.astype(BF16)


def in_projection(x, g, scale1, shift1, w_bf16, codes, cos, sin_signed, qn, kn, tm):
    b, n, d = x.shape
    width = w_bf16.shape[1]
    assert len(codes) * IN_TN == width
    per_batch = scale1.shape[0] > 1
    mod_map = (lambda bi, i: (bi, 0, 0)) if per_batch else (lambda bi, i: (0, 0, 0))
    const = lambda bi, i: (0, 0)
    return pl.pallas_call(
        functools.partial(_in_proj_kernel, codes=tuple(codes), scale=HEAD_DIM ** -0.5 * LOG2E),
        grid=(b, n // tm),
        in_specs=[pl.BlockSpec((1, tm, d), lambda bi, i: (bi, i, 0)),
                  pl.BlockSpec((1, d), const),
                  pl.BlockSpec((1, 1, d), mod_map),
                  pl.BlockSpec((1, 1, d), mod_map),
                  pl.BlockSpec((d, width), const, pipeline_mode=pl.Buffered(1)),
                  pl.BlockSpec((tm, HEAD_DIM), lambda bi, i: (i, 0)),
                  pl.BlockSpec((tm, HEAD_DIM), lambda bi, i: (i, 0)),
                  pl.BlockSpec((1, HEAD_DIM), const),
                  pl.BlockSpec((1, HEAD_DIM), const)],
        out_specs=pl.BlockSpec((1, tm, width), lambda bi, i: (bi, i, 0)),
        out_shape=jax.ShapeDtypeStruct((b, n, width), BF16),
        scratch_shapes=[pltpu.VMEM((tm, d), BF16)],
        compiler_params=_params(("arbitrary", "arbitrary")),
        name="in_projection",
    )(x, g.reshape(1, d), scale1, shift1, w_bf16, cos, sin_signed, qn, kn)


KEY_CHUNK = 256

class _SoftmaxChain:
    def __init__(self, q, chunks, s_ref, dv, sink=None):
        self.q, self.chunks, self.s_ref, self.sink = q, chunks, s_ref, sink
        m = q.shape[0]
        self.ones_col = dv == HEAD_DIM
        self.dv = dv
        self.mx = jnp.full((m, HEAD_DIM), NEG_INF, F32)
        self.ls = jnp.zeros((m, HEAD_DIM), F32)
        self.acc = jnp.zeros((m, 2 * HEAD_DIM if self.ones_col else dv), F32)

    def score_steps(self):
        return [functools.partial(self._score, j) for j in range(len(self.chunks))]

    def value_steps(self):
        return [functools.partial(self._value, j) for j in range(len(self.chunks))]

    def _score(self, j):
        k_fn, _, bias_fn = self.chunks[j]
        s = _nt_dot(self.q, k_fn())
        if bias_fn is not None:
            s = s + bias_fn()
        self.s_ref[:, j * KEY_CHUNK:(j + 1) * KEY_CHUNK] = s
        for t in range(KEY_CHUNK // HEAD_DIM):
            self.mx = jnp.maximum(self.mx, s[:, t * HEAD_DIM:(t + 1) * HEAD_DIM])

    def finish_scores(self):
        m = jnp.max(self.mx, axis=-1, keepdims=True)
        if self.sink is not None:
            m = jnp.maximum(m, self.sink)
        self.m = m
        self.m_wide = jnp.broadcast_to(m, self.mx.shape)

    def _value(self, j):
        _, v_fn, _ = self.chunks[j]
        parts = []
        for t in range(KEY_CHUNK // HEAD_DIM):
            lo = j * KEY_CHUNK + t * HEAD_DIM
            e = jnp.exp2(self.s_ref[:, lo:lo + HEAD_DIM] - self.m_wide)
            if not self.ones_col:
                self.ls = self.ls + e
            parts.append(e.astype(BF16))
        p = jnp.concatenate(parts, axis=1)
        v = v_fn()
        if self.ones_col:
            v = jnp.concatenate([v, jnp.ones(v.shape, BF16)], axis=1)
        self.acc = self.acc + jnp.dot(p, v, preferred_element_type=F32)

    def result(self):
        if self.ones_col:
            num, den = self.acc[:, :self.dv], self.acc[:, self.dv:]
        else:
            num, den = self.acc, jnp.sum(self.ls, axis=-1, keepdims=True)
        if self.sink is not None:
            den = den + jnp.exp2(self.sink - self.m)
        return num / den


def _run_chains(chains):
    prev = None
    for chain in chains:
        scores = chain.score_steps()
        values = prev.value_steps() if prev is not None else []
        for i in range(max(len(scores), len(values))):
            if i < len(values):
                values[i]()
            if i < len(scores):
                scores[i]()
        chain.finish_scores()
        prev = chain
    for step in prev.value_steps():
        step()


def _ref_chunks(k_ref, v_ref, start, length, bias=None, k_cols=slice(None), v_cols=slice(None)):
    assert length % KEY_CHUNK == 0
    out = []
    for j in range(length // KEY_CHUNK):
        if isinstance(start, int):
            rows = slice(start + j * KEY_CHUNK, start + (j + 1) * KEY_CHUNK)
        else:
            rows = pl.ds(start + j * KEY_CHUNK, KEY_CHUNK)
        out.append((lambda rows=rows: k_ref[0, rows, k_cols], lambda rows=rows: v_ref[0, rows, v_cols],
                    (lambda j=j: bias(j)) if bias is not None else None))
    return out


def _gqa_kernel(*refs, rep, groups, tq, mode, seq, use_sink):
    if use_sink:
        sink_ref, refs = refs[0], refs[1:]
    s_ref = refs[-1]
    refs = refs[:-1]
    if mode == 'none':
        q_ref, kx_ref, vx_ref, o_ref = refs
    else:
        q_ref, kx_ref, vx_ref, kl_ref, vl_ref, o_ref = refs
    gi = pl.program_id(1)
    qi = pl.program_id(2)
    bias = None
    if mode == 'band':
        band = tq + 2 * A_WINDOW
        start = pl.multiple_of(jnp.clip(qi * tq - A_WINDOW, 0, seq - band), HEAD_DIM)
        rel = (lax.broadcasted_iota(jnp.int32, (tq, KEY_CHUNK), 0)
               - lax.broadcasted_iota(jnp.int32, (tq, KEY_CHUNK), 1))

        def bias(j):
            dist = rel + (qi * tq - start - j * KEY_CHUNK)
            return jnp.where(jnp.abs(dist) <= A_WINDOW, 0.0, NEG_INF)

    chains = []
    for g in range(groups):
        kv_cols = slice(g * HEAD_DIM, (g + 1) * HEAD_DIM)
        chunks = _ref_chunks(kx_ref, vx_ref, 0, kx_ref.shape[1], k_cols=kv_cols, v_cols=kv_cols)
        if mode == 'full':
            chunks += _ref_chunks(kl_ref, vl_ref, 0, seq, k_cols=kv_cols, v_cols=kv_cols)
        elif mode == 'band':
            chunks += _ref_chunks(kl_ref, vl_ref, start, band, bias, k_cols=kv_cols, v_cols=kv_cols)
        for r in range(rep):
            head = g * rep + r
            q = q_ref[0, :, head * HEAD_DIM:(head + 1) * HEAD_DIM]
            sink = sink_ref[gi * groups * rep + head] * LOG2E if use_sink else None
            chains.append(_SoftmaxChain(q, chunks, s_ref.at[head], HEAD_DIM, sink=sink))
    _run_chains(chains)
    for head, chain in enumerate(chains):
        o_ref[0, :, head * HEAD_DIM:(head + 1) * HEAD_DIM] = chain.result().astype(BF16)


def gqa_attention(p_q, p_ctx, p_lat, *, kv_heads, rep, q_col, k_col, v_col, tq, mode, sink=None, groups=1):
    b, nq, _ = p_q.shape
    c = p_ctx.shape[1]
    qw = groups * rep * HEAD_DIM
    kw = groups * HEAD_DIM
    use_sink = sink is not None
    assert kv_heads % groups == 0 and q_col % qw == 0 and k_col % kw == 0 and v_col % kw == 0
    in_specs = [pl.BlockSpec((1, tq, qw), lambda bi, g, qi, *_: (bi, qi, q_col // qw + g)),
                pl.BlockSpec((1, c, kw), lambda bi, g, qi, *_: (bi, 0, k_col // kw + g)),
                pl.BlockSpec((1, c, kw), lambda bi, g, qi, *_: (bi, 0, v_col // kw + g))]
    args = [p_q, p_ctx, p_ctx]
    seq = 0
    if mode != 'none':
        seq = p_lat.shape[1]
        in_specs += [pl.BlockSpec((1, seq, kw), lambda bi, g, qi, *_: (bi, 0, k_col // kw + g)),
                     pl.BlockSpec((1, seq, kw), lambda bi, g, qi, *_: (bi, 0, v_col // kw + g))]
        args += [p_lat, p_lat]
    if use_sink:
        args = [sink] + args
    n_keys = c + {'none': 0, 'full': seq, 'band': tq + 2 * A_WINDOW}[mode]
    grid_spec = pltpu.PrefetchScalarGridSpec(
        num_scalar_prefetch=1 if use_sink else 0,
        grid=(b, kv_heads // groups, nq // tq),
        in_specs=in_specs,
        out_specs=pl.BlockSpec((1, tq, qw), lambda bi, g, qi, *_: (bi, qi, g)),
        scratch_shapes=[pltpu.VMEM((groups * rep, tq, n_keys), F32)])
    return pl.pallas_call(
        functools.partial(_gqa_kernel, rep=rep, groups=groups, tq=tq, mode=mode, seq=seq, use_sink=use_sink),
        grid_spec=grid_spec,
        out_shape=jax.ShapeDtypeStruct((b, nq, kv_heads * rep * HEAD_DIM), BF16),
        compiler_params=_params(("arbitrary", "arbitrary", "arbitrary")),
        name=f"gqa_attention_{mode}",
    )(*args)


DIFF_HEADS_PER_STEP = 2


def _diff_kernel(*refs, has_lat, lam_init):
    if has_lat:
        lq1, lk1, lq2, lk2, sub_ref, q_ref, kx_ref, vx_ref, kl_ref, vl_ref, o_ref, s_ref = refs
    else:
        lq1, lk1, lq2, lk2, sub_ref, q_ref, kx_ref, vx_ref, o_ref, s_ref = refs
    lam = (jnp.exp(jnp.sum(lq1[...] * lk1[...], axis=-1, keepdims=True))
           - jnp.exp(jnp.sum(lq2[...] * lk2[...], axis=-1, keepdims=True)) + lam_init)
    n_ctx = vx_ref.shape[1]
    dv = 2 * HEAD_DIM
    chains = []
    for h in range(DIFF_HEADS_PER_STEP):
        v_cols = slice(h * dv, (h + 1) * dv)
        for m in range(2):
            k_cols = slice((2 * h + m) * HEAD_DIM, (2 * h + m + 1) * HEAD_DIM)
            chunks = _ref_chunks(kx_ref, vx_ref, 0, n_ctx, k_cols=k_cols, v_cols=v_cols)
            if has_lat:
                chunks += _ref_chunks(kl_ref, vl_ref, 0, vl_ref.shape[1], k_cols=k_cols, v_cols=v_cols)
            chains.append(_SoftmaxChain(q_ref[0, :, k_cols], chunks, s_ref.at[2 * h + m], dv))
    _run_chains(chains)
    for h in range(DIFF_HEADS_PER_STEP):
        o = chains[2 * h].result() - lam * chains[2 * h + 1].result()
        ms = jnp.mean(o * o, axis=-1, keepdims=True)
        o_ref[0, :, h * dv:(h + 1) * dv] = (o * lax.rsqrt(ms + EPS) * sub_ref[...] * (1.0 - lam_init)).astype(BF16)


def diff_attention(p_q, p_ctx, p_lat, lam_vecs, subln, *, lam_init, tq, has_lat):
    b, nq, _ = p_q.shape
    c = p_ctx.shape[1]
    qb = (A_HEADS + 2 * A_KV_HEADS) * HEAD_DIM
    kb = qb + B_HEADS * 2 * HEAD_DIM
    vb = kb + B_HEADS * 2 * HEAD_DIM
    dv = 2 * HEAD_DIM
    wide = DIFF_HEADS_PER_STEP * dv
    assert qb % wide == 0 and kb % wide == 0 and vb % wide == 0 and B_HEADS % DIFF_HEADS_PER_STEP == 0
    vec = pl.BlockSpec((1, HEAD_DIM), lambda bi, h, qi: (0, 0))

    def cols(n, base):
        return pl.BlockSpec((1, n, wide), lambda bi, h, qi: (bi, 0, base // wide + h))

    in_specs = [vec, vec, vec, vec, pl.BlockSpec((1, dv), lambda bi, h, qi: (0, 0)),
                pl.BlockSpec((1, tq, wide), lambda bi, h, qi: (bi, qi, qb // wide + h)),
                cols(c, kb), cols(c, vb)]
    args = [v.reshape(1, HEAD_DIM) for v in lam_vecs] + [subln.reshape(1, dv), p_q, p_ctx, p_ctx]
    seq = 0
    if has_lat:
        seq = p_lat.shape[1]
        in_specs += [cols(seq, kb), cols(seq, vb)]
        args += [p_lat, p_lat]
    return pl.pallas_call(
        functools.partial(_diff_kernel, has_lat=has_lat, lam_init=lam_init),
        grid=(b, B_HEADS // DIFF_HEADS_PER_STEP, nq // tq),
        in_specs=in_specs,
        out_specs=pl.BlockSpec((1, tq, wide), lambda bi, h, qi: (bi, qi, h)),
        out_shape=jax.ShapeDtypeStruct((b, nq, B_HEADS * dv), BF16),
        scratch_shapes=[pltpu.VMEM((2 * DIFF_HEADS_PER_STEP, tq, c + seq), F32)],
        compiler_params=_params(("arbitrary", "arbitrary", "arbitrary")),
        name="diff_attention",
    )(*args)


NA_QROWS = 4
NA_WROWS = NA_QROWS + NA_ROWS
NA_BLOCKS_PER_STEP = 8


def _natten_kernel(q_ref, kx_ref, vx_ref, kl_ref, vl_ref, tb_ref, o_ref, s_ref, *, n_rows):
    n_blocks = n_rows // NA_QROWS
    tq = NA_QROWS * GRID_W
    n_ctx = kx_ref.shape[1]
    chains = []
    for i in range(NA_BLOCKS_PER_STEP):
        blk = pl.program_id(2) * NA_BLOCKS_PER_STEP + i
        first_row = jnp.clip(blk * NA_QROWS - NA_ROWS // 2, 0, n_rows - NA_WROWS)
        start = pl.multiple_of(first_row * GRID_W, GRID_W)
        variant = jnp.where(blk == 0, 0, jnp.where(blk == n_blocks - 1, 2, 1))

        def bias(j, variant=variant):
            return tb_ref[0, variant, :, j * KEY_CHUNK:(j + 1) * KEY_CHUNK]

        chunks = (_ref_chunks(kx_ref, vx_ref, 0, n_ctx)
                  + _ref_chunks(kl_ref, vl_ref, start, NA_WROWS * GRID_W, bias))
        chains.append(_SoftmaxChain(q_ref[0, i * tq:(i + 1) * tq, :], chunks, s_ref.at[i], HEAD_DIM))
    _run_chains(chains)
    for i, chain in enumerate(chains):
        o_ref[0, i * tq:(i + 1) * tq, :] = chain.result().astype(BF16)


def natten_bias_slabs(rpb):
    col = jnp.arange(GRID_W)
    cs = jnp.clip(col - NA_COLS // 2, 0, GRID_W - NA_COLS)
    col_mask = (col[None, :] >= cs[:, None]) & (col[None, :] < cs[:, None] + NA_COLS)
    col_idx = jnp.clip(col[None, :] - col[:, None], -(NA_COLS - 1), NA_COLS - 1) + NA_COLS - 1
    rpb_c = jnp.where(col_mask[None, None], rpb[:, :, col_idx], NEG_INF)
    i = jnp.arange(NA_QROWS)[:, None]
    a = jnp.arange(NA_WROWS)[None, :]
    half = NA_ROWS // 2
    kinds = [(a < NA_ROWS, a - i + NA_ROWS - 1),
             ((a >= i) & (a < i + NA_ROWS), a - i - half + NA_ROWS - 1),
             (a >= NA_WROWS - NA_ROWS, a - i - NA_ROWS + NA_ROWS - 1)]
    slabs = []
    for valid, roff in kinds:
        t = rpb_c[:, jnp.clip(roff, 0, 2 * NA_ROWS - 2)] * LOG2E
        t = jnp.where(valid[None, :, :, None, None], t, NEG_INF)
        slabs.append(jnp.transpose(t, (0, 1, 3, 2, 4)).reshape(rpb.shape[0], NA_QROWS * GRID_W, NA_WROWS * GRID_W))
    return jnp.stack(slabs, axis=1).astype(F32)


def natten_attention(p_lat, p_ctx, slabs):
    b, seq, _ = p_lat.shape
    c = p_ctx.shape[1]
    n_rows = seq // GRID_W
    rows_per_step = NA_QROWS * NA_BLOCKS_PER_STEP
    assert n_rows >= NA_WROWS and n_rows % rows_per_step == 0
    qd = (C_HEADS + 2 * C_KV_HEADS) * HEAD_DIM
    kd = qd + D_HEADS * HEAD_DIM
    vd = kd + D_HEADS * HEAD_DIM
    tq = rows_per_step * GRID_W

    def col(n, base):
        return pl.BlockSpec((1, n, HEAD_DIM), lambda bi, h, i: (bi, 0, base // HEAD_DIM + h))

    in_specs = [pl.BlockSpec((1, tq, HEAD_DIM), lambda bi, h, i: (bi, i, qd // HEAD_DIM + h)),
                col(c, kd), col(c, vd), col(seq, kd), col(seq, vd),
                pl.BlockSpec((1, 3, NA_QROWS * GRID_W, NA_WROWS * GRID_W), lambda bi, h, i: (h, 0, 0, 0))]
    return pl.pallas_call(
        functools.partial(_natten_kernel, n_rows=n_rows),
        grid=(b, D_HEADS, n_rows // rows_per_step),
        in_specs=in_specs,
        out_specs=pl.BlockSpec((1, tq, HEAD_DIM), lambda bi, h, i: (bi, i, h)),
        out_shape=jax.ShapeDtypeStruct((b, seq, D_HEADS * HEAD_DIM), BF16),
        scratch_shapes=[pltpu.VMEM((NA_BLOCKS_PER_STEP, NA_QROWS * GRID_W, c + NA_WROWS * GRID_W), F32)],
        compiler_params=_params(("arbitrary", "arbitrary", "arbitrary")),
        name="natten_attention",
    )(p_lat, p_ctx, p_ctx, p_lat, p_lat, slabs)


def _out_proj_kernel(o1_ref, o2_ref, w_ref, x_ref, gt_ref, g_ref, sc_ref, sh_ref, wr_ref, xo_ref, h_ref, aff_ref,
                     *, n_experts, sub_rows):
    for r0 in range(0, x_ref.shape[1], sub_rows):
        rows = slice(r0, r0 + sub_rows)
        y = (jnp.dot(o1_ref[0, rows, :], w_ref[0], preferred_element_type=F32)
             + jnp.dot(o2_ref[0, rows, :], w_ref[1], preferred_element_type=F32))
        x = x_ref[0, rows, :] + gt_ref[0] * y
        xo_ref[0, rows, :] = x
        ms = jnp.mean(x * x, axis=-1, keepdims=True)
        h = x * lax.rsqrt(ms + EPS) * g_ref[...]
        h = h * (1.0 + sc_ref[0]) + sh_ref[0]
        h_ref[0, rows, :] = h
        logits = jnp.dot(h.astype(BF16), wr_ref[...], preferred_element_type=F32)
        lane = lax.broadcasted_iota(jnp.int32, logits.shape, 1)
        logits = jnp.where(lane < n_experts, logits, NEG_INF)
        e = jnp.exp(logits - jnp.max(logits, axis=-1, keepdims=True))
        aff_ref[0, rows, :] = e / jnp.sum(e, axis=-1, keepdims=True)


def out_projection(o1, o2, w_bf16, x, gate1, g_ffn, scale2, shift2, w_router_pad, n_experts, tm):
    b, n, d = x.shape
    k = o1.shape[2]
    assert o2.shape[2] == k and w_bf16.shape == (2, k, d)
    per_batch = gate1.shape[0] > 1
    mod_map = (lambda bi, i: (bi, 0, 0)) if per_batch else (lambda bi, i: (0, 0, 0))
    row = lambda bi, i: (bi, i, 0)
    return pl.pallas_call(
        functools.partial(_out_proj_kernel, n_experts=n_experts, sub_rows=min(tm, 256)),
        grid=(b, n // tm),
        in_specs=[pl.BlockSpec((1, tm, k), row),
                  pl.BlockSpec((1, tm, k), row),
                  pl.BlockSpec((2, k, d), lambda bi, i: (0, 0, 0), pipeline_mode=pl.Buffered(1)),
                  pl.BlockSpec((1, tm, d), row),
                  pl.BlockSpec((1, 1, d), mod_map),
                  pl.BlockSpec((1, d), lambda bi, i: (0, 0)),
                  pl.BlockSpec((1, 1, d), mod_map),
                  pl.BlockSpec((1, 1, d), mod_map),
                  pl.BlockSpec((d, ROUTER_LANES), lambda bi, i: (0, 0))],
        out_specs=[pl.BlockSpec((1, tm, d), row), pl.BlockSpec((1, tm, d), row),
                   pl.BlockSpec((1, tm, ROUTER_LANES), row)],
        out_shape=[jax.ShapeDtypeStruct((b, n, d), F32), jax.ShapeDtypeStruct((b, n, d), F32),
                   jax.ShapeDtypeStruct((b, n, ROUTER_LANES), F32)],
        compiler_params=_params(("arbitrary", "arbitrary")),
        name="out_projection",
    )(o1, o2, w_bf16, x, gate1, g_ffn.reshape(1, d), scale2, shift2, w_router_pad)


MOE_ROW_CHUNK = 1024


SUBLANES = 8


MOE_ROWS_PER_STEP = 512


def _experts_per_step(n_experts, cap):
    per_step = max(1, min(n_experts, MOE_ROWS_PER_STEP // cap))
    assert n_experts % per_step == 0 and cap % SUBLANES == 0
    return per_step


def _split_rows(idx):
    b, e, cap = idx.shape
    return (idx // SUBLANES).reshape(b, e, 1, cap), (idx % SUBLANES).reshape(b, e, 1, cap)


def _gather_kernel(hi_ref, lo_ref, *refs, cap, dc):
    h_refs, o_ref, stage_ref = refs[:-2], refs[-2], refs[-1]

    def body(i, carry):
        for r in range(SUBLANES):
            c = i * SUBLANES + r
            hi, lo = hi_ref[0, 0, 0, c], lo_ref[0, 0, 0, c]
            for k, h_ref in enumerate(h_refs):
                stage_ref[i, r:r + 1, k * dc:(k + 1) * dc] = h_ref[0, hi, pl.ds(lo, 1), :]
        return carry
    lax.fori_loop(0, cap // SUBLANES, body, 0, unroll=4)
    o_ref[0, 0] = stage_ref[...].reshape(cap, stage_ref.shape[2]).astype(BF16)


def moe_gather(h, idx, dc):
    b, n, d = h.shape
    n_experts, cap_e = idx.shape[1:]
    per_step = _experts_per_step(n_experts, cap_e)
    idx = idx.reshape(b, n_experts // per_step, per_step * cap_e)
    _, e, cap = idx.shape
    assert n % SUBLANES == 0
    n_chunks = d // dc
    h_specs = [pl.BlockSpec((1, n // SUBLANES, SUBLANES, dc), lambda bi, ei, k=k: (bi, 0, 0, k),
                            pipeline_mode=pl.Buffered(1)) for k in range(n_chunks)]
    smem = pl.BlockSpec((1, 1, 1, cap), lambda bi, ei: (bi, ei, 0, 0), memory_space=pltpu.SMEM)
    h_tiles = h.reshape(b, n // SUBLANES, SUBLANES, d)
    return pl.pallas_call(
        functools.partial(_gather_kernel, cap=cap, dc=dc),
        grid=(b, e),
        in_specs=[smem, smem] + h_specs,
        out_specs=pl.BlockSpec((1, 1, cap, d), lambda bi, ei: (bi, ei, 0, 0)),
        out_shape=jax.ShapeDtypeStruct((b, e, cap, d), BF16),
        scratch_shapes=[pltpu.VMEM((cap // SUBLANES, SUBLANES, d), F32)],
        compiler_params=_params(("arbitrary", "arbitrary")),
        name="moe_gather",
    )(*_split_rows(idx), *([h_tiles] * n_chunks)).reshape(b, n_experts, cap_e, d)


EXPERT_F_CHUNK = 256


def _expert_kernel(*refs, n_chunks, step_ranges):
    n_groups = len(step_ranges)
    in_refs = refs[:2 * n_groups]
    wg_ref, wu_ref, wd_ref = refs[2 * n_groups:2 * n_groups + 3]
    out_refs = refs[2 * n_groups + 3:3 * n_groups + 3]
    wg_s, wu_s, wd_s = refs[-3:]
    step = pl.program_id(1)

    @pl.when(step < n_chunks)
    def _():
        wg_s[step] = wg_ref[0, 0].astype(BF16)
        wu_s[step] = wu_ref[0, 0].astype(BF16)
        wd_s[step] = wd_ref[0, 0].astype(BF16)

    for g, (lo, hi) in enumerate(step_ranges):
        xe_ref, gate_ref, o_ref = in_refs[2 * g], in_refs[2 * g + 1], out_refs[g]

        @pl.when((step >= lo) & (step < hi))
        def _(xe_ref=xe_ref, gate_ref=gate_ref, o_ref=o_ref):
            bs, _, cap, d = xe_ref.shape
            xe = xe_ref[...].reshape(bs * cap, d)
            ye = jnp.zeros((bs * cap, d), F32)
            for k in range(n_chunks):
                a = jnp.dot(xe, wg_s[k], preferred_element_type=F32)
                u = jnp.dot(xe, wu_s[k], preferred_element_type=F32)
                hid = (a * (1.0 / (1.0 + jnp.exp(-a))) * u).astype(BF16)
                ye = ye + jnp.dot(hid, wd_s[k], preferred_element_type=F32)
            o_ref[...] = (ye * gate_ref[...].reshape(bs * cap, 1)).reshape(bs, 1, cap, d)


def moe_experts(groups, wg, wu, wd, layer):
    _, e, d, f = wg.shape
    fc = math.gcd(f, EXPERT_F_CHUNK)
    n_chunks = f // fc
    in_specs, args, out_specs, out_shapes, step_ranges = [], [], [], [], []
    lo = n_chunks
    for xe, gate in groups:
        b, _, cap, _ = xe.shape
        bs = max(1, min(b, 512 // cap))
        assert b % bs == 0
        nb = b // bs
        tile = lambda ei, s, lo=lo, nb=nb: (jnp.clip(s - lo, 0, nb - 1), ei, 0, 0)
        in_specs += [pl.BlockSpec((bs, 1, cap, d), tile), pl.BlockSpec((bs, 1, cap, 1), tile)]
        args += [xe, gate]
        out_specs.append(pl.BlockSpec((bs, 1, cap, d), tile))
        out_shapes.append(jax.ShapeDtypeStruct((b, e, cap, d), F32))
        step_ranges.append((lo, lo + nb))
        lo += nb
    chunk = lambda ei, s: jnp.minimum(s, n_chunks - 1)
    in_specs += [pl.BlockSpec((1, 1, d, fc), lambda ei, s: (layer, ei, 0, chunk(ei, s))),
                 pl.BlockSpec((1, 1, d, fc), lambda ei, s: (layer, ei, 0, chunk(ei, s))),
                 pl.BlockSpec((1, 1, fc, d), lambda ei, s: (layer, ei, chunk(ei, s), 0))]
    return pl.pallas_call(
        functools.partial(_expert_kernel, n_chunks=n_chunks, step_ranges=tuple(step_ranges)),
        grid=(e, lo),
        in_specs=in_specs,
        out_specs=out_specs,
        out_shape=out_shapes,
        scratch_shapes=[pltpu.VMEM((n_chunks, d, fc), BF16), pltpu.VMEM((n_chunks, d, fc), BF16),
                        pltpu.VMEM((n_chunks, fc, d), BF16)],
        compiler_params=_params(("arbitrary", "arbitrary")),
        name="moe_experts",
    )(*args, wg, wu, wd)


def _combine_kernel(hi_ref, lo_ref, ye_ref, x_ref, gt_ref, *refs, cap, n_experts, rows_out, final_norm):
    gf_ref = refs[0] if final_norm else None
    o_ref, acc_ref = refs[-2:]
    step = pl.program_id(1)

    @pl.when(step == 0)
    def _():
        acc_ref[...] = jnp.zeros_like(acc_ref)

    @pl.when(step < n_experts)
    def _():
        def body(i, carry):
            rows = [(hi_ref[0, 0, 0, i * SUBLANES + r], lo_ref[0, 0, 0, i * SUBLANES + r])
                    for r in range(SUBLANES)]
            old = [acc_ref[hi, pl.ds(lo, 1), :] for hi, lo in rows]
            for r, (hi, lo) in enumerate(rows):
                acc_ref[hi, pl.ds(lo, 1), :] = old[r] + ye_ref[0, 0, i, r:r + 1, :]
            return carry
        lax.fori_loop(0, cap // SUBLANES, body, 0, unroll=4)

    @pl.when(step >= n_experts)
    def _():
        tiles_out = rows_out // SUBLANES
        acc = acc_ref[pl.ds((step - n_experts) * tiles_out, tiles_out)]
        x = x_ref[0] + gt_ref[0] * acc.reshape(rows_out, acc.shape[2])
        if final_norm:
            ms = jnp.mean(x * x, axis=-1, keepdims=True)
            x = x * lax.rsqrt(ms + EPS) * gf_ref[...]
        o_ref[0] = x


def moe_combine(ye, idx, x, gate2, final_gain=None):
    b, n, d = x.shape
    final_norm = final_gain is not None
    extra_specs = [pl.BlockSpec((1, d), lambda bi, s: (0, 0))] if final_norm else []
    extra_args = [final_gain.reshape(1, d)] if final_norm else []
    n_experts, cap_e = idx.shape[1:]
    per_step = _experts_per_step(n_experts, cap_e)
    idx = idx.reshape(b, n_experts // per_step, per_step * cap_e)
    _, e, cap = idx.shape
    rows_out = min(n, 256)
    per_batch = gate2.shape[0] > 1
    mod_map = (lambda bi, s: (bi, 0, 0)) if per_batch else (lambda bi, s: (0, 0, 0))
    assert n % SUBLANES == 0 and rows_out % SUBLANES == 0
    expert = lambda bi, s: (bi, jnp.minimum(s, e - 1), 0, 0)
    expert_rows = lambda bi, s: (bi, jnp.minimum(s, e - 1), 0, 0, 0)
    tile = lambda bi, s: (bi, jnp.maximum(s - e, 0), 0)
    smem = pl.BlockSpec((1, 1, 1, cap), expert, memory_space=pltpu.SMEM)
    return pl.pallas_call(
        functools.partial(_combine_kernel, cap=cap, n_experts=e, rows_out=rows_out, final_norm=final_norm),
        grid=(b, e + n // rows_out),
        in_specs=[smem, smem,
                  pl.BlockSpec((1, 1, cap // SUBLANES, SUBLANES, d), expert_rows),
                  pl.BlockSpec((1, rows_out, d), tile),
                  pl.BlockSpec((1, 1, d), mod_map)] + extra_specs,
        out_specs=pl.BlockSpec((1, rows_out, d), tile),
        out_shape=jax.ShapeDtypeStruct((b, n, d), F32),
        scratch_shapes=[pltpu.VMEM((n // SUBLANES, SUBLANES, d), F32)],
        compiler_params=_params(("arbitrary", "arbitrary")),
        name="moe_combine",
    )(*_split_rows(idx), ye.reshape(b, e, cap // SUBLANES, SUBLANES, d), x, gate2, *extra_args)


def moe_dispatch(h, aff, n_experts):
    b, n, d = h.shape
    cap = max(1, EC_CAPACITY_FACTOR * n // n_experts)
    gate, idx = lax.top_k(jnp.swapaxes(aff[..., :n_experts], 1, 2), cap)
    return idx, gate[..., None], moe_gather(h, idx, min(d, MOE_ROW_CHUNK))


def _rope_tables(n):
    t = jnp.arange(n, dtype=jnp.int32)
    row = (t // GRID_W).astype(F32)
    col = (t % GRID_W).astype(F32)
    n_freq = HEAD_DIM // 4
    inv = ROPE_THETA ** (-jnp.arange(n_freq, dtype=F32) / n_freq)
    ar = row[:, None] * inv
    ac = col[:, None] * inv
    ang = jnp.concatenate([ar, ar, ac, ac], axis=-1)
    sign = jnp.where((jnp.arange(HEAD_DIM) % (2 * n_freq)) < n_freq, -1.0, 1.0).astype(F32)
    return jnp.cos(ang), jnp.sin(ang) * sign


def _tile_codes(layer, rope):
    if layer % 2 == 0:
        groups = [(A_HEADS * HEAD_DIM, ROPE_BIT | SCALE_BIT), (A_KV_HEADS * HEAD_DIM, ROPE_BIT),
                  (A_KV_HEADS * HEAD_DIM, 0), (B_HEADS * 2 * HEAD_DIM, ROPE_BIT | SCALE_BIT),
                  (B_HEADS * 2 * HEAD_DIM, ROPE_BIT), (B_HEADS * 2 * HEAD_DIM, 0)]
    else:
        groups = [(C_HEADS * HEAD_DIM, QNORM_BIT | ROPE_BIT | SCALE_BIT), (C_KV_HEADS * HEAD_DIM, KNORM_BIT | ROPE_BIT),
                  (C_KV_HEADS * HEAD_DIM, 0), (D_HEADS * HEAD_DIM, SCALE_BIT), (D_HEADS * HEAD_DIM, 0),
                  (D_HEADS * HEAD_DIM, 0)]
    codes = []
    for width, code in groups:
        assert width % IN_TN == 0
        codes += [code if rope else code & ~ROPE_BIT] * (width // IN_TN)
    return codes


def _lambda_init(layer):
    return 0.8 - 0.6 * math.exp(-0.3 * layer)


def _row_tile(n, target):
    t = min(n, target)
    assert n % t == 0
    return t


def kernel(x, c, ctx, c_ctx, w_ada, b_ada, g_mix, g_ffn, w_in, w_out, a_sink, b_lam_q1, b_lam_k1, b_lam_q2,
           b_lam_k2, b_subln, c_q_norm, c_k_norm, d_rpb, w_router, w_gate, w_up, w_down, g_final):
    bsz, seq, d = x.shape
    n_ctx = ctx.shape[1]
    depth = w_ada.shape[0]
    n_experts = w_router.shape[2]
    cos, sin_signed = _rope_tables(seq)
    ones_head = jnp.ones((1, HEAD_DIM), F32)

    n_rows = -(-(bsz + 1) // 8) * 8
    c_all = jnp.zeros((n_rows, d), F32).at[:bsz].set(c).at[bsz].set(c_ctx)
    mod = ada_modulation(c_all, w_ada, b_ada)

    xc = ctx
    for l in range(depth):
        need_ctx = l < depth - 1
        e = l // 2
        m_lat = mod[l, :bsz].reshape(bsz, 1, 6, d)
        m_ctx = mod[l, bsz:bsz + 1].reshape(1, 1, 6, d)
        sh1, sc1, gt1, sh2, sc2, gt2 = (m_lat[:, :, i] for i in range(6))
        sh1x, sc1x, gt1x, sh2x, sc2x, gt2x = (m_ctx[:, :, i] for i in range(6))
        w_in_l = w_in[l].astype(BF16)
        w_out_l = w_out[l].astype(BF16).reshape(2, MIX_WIDTH // 2, d)
        if l % 2 == 0:
            qn = kn = ones_head
        else:
            qn, kn = c_q_norm[e].reshape(1, HEAD_DIM), c_k_norm[e].reshape(1, HEAD_DIM)
        tm_lat = _row_tile(seq, 512)
        tm_ctx = _row_tile(n_ctx, 512)
        p = in_projection(x, g_mix[l], sc1, sh1, w_in_l, _tile_codes(l, True), cos, sin_signed, qn, kn, tm_lat)
        px = in_projection(xc, g_mix[l], sc1x, sh1x, w_in_l, _tile_codes(l, False), cos[:tm_ctx],
                           sin_signed[:tm_ctx], qn, kn, tm_ctx)
        ox1 = ox2 = None
        if l % 2 == 0:
            lam_vecs = (b_lam_q1[e], b_lam_k1[e], b_lam_q2[e], b_lam_k2[e])
            ka = A_HEADS * HEAD_DIM
            va = ka + A_KV_HEADS * HEAD_DIM
            a_cfg = dict(kv_heads=A_KV_HEADS, rep=A_HEADS // A_KV_HEADS, q_col=0, k_col=ka, v_col=va,
                         sink=a_sink[e], groups=A_KV_HEADS)
            o1 = gqa_attention(p, px, p, tq=_row_tile(seq, 256), mode='band', **a_cfg)
            o2 = diff_attention(p, px, p, lam_vecs, b_subln[e], lam_init=_lambda_init(l),
                                tq=_row_tile(seq, 256), has_lat=True)
            if need_ctx:
                ox1 = gqa_attention(px, px, None, tq=_row_tile(n_ctx, 256), mode='none', **a_cfg)
                ox2 = diff_attention(px, px, None, lam_vecs, b_subln[e], lam_init=_lambda_init(l),
                                     tq=_row_tile(n_ctx, 256), has_lat=False)
        else:
            kc = C_HEADS * HEAD_DIM
            vc = kc + C_KV_HEADS * HEAD_DIM
            c_cfg = dict(kv_heads=C_KV_HEADS, rep=C_HEADS // C_KV_HEADS, q_col=0, k_col=kc, v_col=vc,
                         groups=C_KV_HEADS)
            o1 = gqa_attention(p, px, p, tq=_row_tile(seq, 256), mode='full', **c_cfg)
            o2 = natten_attention(p, px, natten_bias_slabs(d_rpb[e]))
            if need_ctx:
                qd = vc + C_KV_HEADS * HEAD_DIM
                kd = qd + D_HEADS * HEAD_DIM
                vd = kd + D_HEADS * HEAD_DIM
                ox1 = gqa_attention(px, px, None, tq=_row_tile(n_ctx, 256), mode='none', **c_cfg)
                ox2 = gqa_attention(px, px, None, kv_heads=D_HEADS, rep=1, q_col=qd, k_col=kd, v_col=vd,
                                    tq=_row_tile(n_ctx, 256), mode='none')
        w_router_pad = jnp.zeros((d, ROUTER_LANES), BF16).at[:, :n_experts].set(w_router[l].astype(BF16))
        x1, h, aff = out_projection(o1, o2, w_out_l, x, gt1, g_ffn[l], sc2, sh2, w_router_pad, n_experts,
                                    _row_tile(seq, 512))
        idx, gate, xe = moe_dispatch(h, aff, n_experts)
        groups = [(xe, gate)]
        if need_ctx:
            xc1, hx, affx = out_projection(ox1, ox2, w_out_l, xc, gt1x, g_ffn[l], sc2x, sh2x, w_router_pad,
                                           n_experts, _row_tile(n_ctx, 256))
            idx_x, gate_x, xe_x = moe_dispatch(hx, affx, n_experts)
            groups.append((xe_x, gate_x))
        ye = moe_experts(groups, w_gate, w_up, w_down, l)
        x = moe_combine(ye[0], idx, x1, gt2, g_final if l == depth - 1 else None)
        if need_ctx:
            xc = moe_combine(ye[1], idx_x, xc1, gt2x)
    return x
```
